```python
import jax, jax.numpy as jnp
from jax import lax
import numpy as np

D_MODEL = 1024
BATCH = 16
SEQ = 2048
DEPTH = 2
DEC_BATCH = 8
DEC_SEQ = 32
PAST_LEN = 2048

CHUNK = 64
W_ATTN = D_MODEL // 4
W_CONV = D_MODEL // 4
W_GMLP = D_MODEL // 4
W_POOL = D_MODEL // 4
MIX_WIDTH = W_ATTN + W_CONV + W_GMLP + W_POOL
HEAD_DIM = 64
N_Q_HEADS = W_ATTN // HEAD_DIM
N_KV_HEADS = 2
Q_PER_KV = N_Q_HEADS // N_KV_HEADS
KV_WIDTH = N_KV_HEADS * HEAD_DIM
WINDOW = 128
WINDOW_CHUNKS = WINDOW // CHUNK
ROPE_THETA = 500000.0
ROT_DIM = HEAD_DIM // 4
CONV_WIDTH = 3
GMLP_CHUNK = 128
GMLP_GROUPS = 4
GMLP_GROUP_DIM = W_GMLP // GMLP_GROUPS
POOL_WINDOWS = (2, 4, 8, 16)
POOL_GROUP_DIM = W_POOL // len(POOL_WINDOWS)
POOL_HIST = max(POOL_WINDOWS) - 1
D_FF = 2816
FFN_CONV_WIDTH = 3
NORM_EPS = 1e-6
PROJ_WIDTH = W_ATTN + 2 * KV_WIDTH + 3 * W_CONV + 2 * W_GMLP + W_POOL

kernel_name = 'hybrid_streaming_encoder_step'


def split_points():
    sizes = (W_ATTN, KV_WIDTH, KV_WIDTH, W_CONV, W_CONV, W_CONV, W_GMLP, W_GMLP, W_POOL)
    pts, acc = [], 0
    for s in sizes[:-1]:
        acc += s
        pts.append(acc)
    return pts


def rmsnorm(x, g):
    xf = x.astype(jnp.float32)
    y = xf * lax.rsqrt(jnp.mean(xf * xf, axis=-1, keepdims=True) + NORM_EPS)
    return (y * g.astype(jnp.float32)).astype(x.dtype)


def layernorm(x, g, b):
    xf = x.astype(jnp.float32)
    mu = jnp.mean(xf, axis=-1, keepdims=True)
    var = jnp.mean(jnp.square(xf - mu), axis=-1, keepdims=True)
    y = (xf - mu) * lax.rsqrt(var + NORM_EPS) * g.astype(jnp.float32) + b.astype(jnp.float32)
    return y.astype(x.dtype)


def partial_rope(x, pos):
    half = ROT_DIM // 2
    inv_freq = jnp.power(jnp.float32(ROPE_THETA), -jnp.arange(half, dtype=jnp.float32) * (2.0 / ROT_DIM))
    ang = pos.astype(jnp.float32)[:, None] * inv_freq[None, :]
    cos = jnp.cos(ang)[None, :, None, :]
    sin = jnp.sin(ang)[None, :, None, :]
    xf = x.astype(jnp.float32)
    x1 = xf[..., :half]
    x2 = xf[..., half:ROT_DIM]
    out = jnp.concatenate([x1 * cos - x2 * sin, x2 * cos + x1 * sin, xf[..., ROT_DIM:]], axis=-1)
    return out.astype(x.dtype)


def sink_attention(q, k, v, sink, mask):
    s = jnp.einsum('...qkrd,...skd->...krqs', q, k).astype(jnp.float32) * (HEAD_DIM ** -0.5)
    if mask is not None:
        s = jnp.where(mask, s, -jnp.inf)
    sk = sink.astype(jnp.float32)[:, :, None, None]
    m = jnp.maximum(jnp.max(s, axis=-1, keepdims=True), sk)
    p = jnp.exp(s - m)
    w = p / (jnp.sum(p, axis=-1, keepdims=True) + jnp.exp(sk - m))
    return jnp.einsum('...krqs,...skd->...qkrd', w.astype(v.dtype), v)


def window_attention_prompt(q, k, v, sink):
    bsz, t_len = q.shape[:2]
    nc = t_len // CHUNK
    pad = WINDOW_CHUNKS * CHUNK
    kp = jnp.pad(k, ((0, 0), (pad, 0), (0, 0), (0, 0)))
    vp = jnp.pad(v, ((0, 0), (pad, 0), (0, 0), (0, 0)))
    band = [(j * CHUNK, j * CHUNK + t_len) for j in range(WINDOW_CHUNKS + 1)]
    kb = jnp.concatenate([kp[:, a:b].reshape(bsz, nc, CHUNK, N_KV_HEADS, HEAD_DIM) for a, b in band], axis=2)
    vb = jnp.concatenate([vp[:, a:b].reshape(bsz, nc, CHUNK, N_KV_HEADS, HEAD_DIM) for a, b in band], axis=2)
    key_pos = (jnp.arange(nc)[:, None] - WINDOW_CHUNKS) * CHUNK + jnp.arange((WINDOW_CHUNKS + 1) * CHUNK)[None, :]
    mask = (key_pos >= 0)[:, None, None, None, :]
    qb = q.reshape(bsz, nc, CHUNK, N_KV_HEADS, Q_PER_KV, HEAD_DIM)
    o = sink_attention(qb, kb, vb, sink.reshape(N_KV_HEADS, Q_PER_KV), mask)
    return o.reshape(bsz, t_len, W_ATTN)


def window_attention_sample(q, k, v, k_past, v_past, sink):
    bsz, t_len = q.shape[:2]
    kk = jnp.concatenate([k_past, k], axis=1)
    vv = jnp.concatenate([v_past, v], axis=1)
    qb = q.reshape(bsz, t_len, N_KV_HEADS, Q_PER_KV, HEAD_DIM)
    o = sink_attention(qb, kk, vv, sink.reshape(N_KV_HEADS, Q_PER_KV), None)
    return o.reshape(bsz, t_len, W_ATTN)


def causal_dwconv(x, past, w):
    width = w.shape[0]
    t_len = x.shape[1]
    ext = jnp.concatenate([past, x], axis=1)
    y = ext[:, 0:t_len] * w[0]
    for j in range(1, width):
        y = y + ext[:, j:j + t_len] * w[j]
    return y, ext[:, -(width - 1):]


def spatial_gating(u, v, ln_g, ln_b, w_s, b_s, first_chunk):
    u = jax.nn.gelu(u)
    v = layernorm(jax.nn.gelu(v), ln_g, ln_b)
    bsz, t_len, _ = v.shape
    ii = jnp.arange(GMLP_CHUNK)
    mask = (ii[None, :] // CHUNK) <= (ii[:, None] // CHUNK)
    wm = jnp.where(mask[None], w_s, jnp.zeros_like(w_s))
    if first_chunk:
        vb = v.reshape(bsz, t_len, GMLP_GROUPS, GMLP_GROUP_DIM)
        s = jnp.einsum('gij,bjgd->bigd', wm[:, :t_len, :t_len], vb) + b_s[:, :t_len].T[None, :, :, None]
    else:
        nc = t_len // GMLP_CHUNK
        vb = v.reshape(bsz, nc, GMLP_CHUNK, GMLP_GROUPS, GMLP_GROUP_DIM)
        s = jnp.einsum('gij,bcjgd->bcigd', wm, vb) + b_s.T[None, None, :, :, None]
    return u * s.reshape(bsz, t_len, W_GMLP), v


def multiscale_pool(p, past, pos0, w_pool, scale):
    bsz, t_len, _ = p.shape
    ext_in = jnp.concatenate([past, p], axis=1)
    ext = ext_in.astype(jnp.float32)
    cs = jnp.concatenate([jnp.zeros((bsz, 1, W_POOL), jnp.float32), jnp.cumsum(ext, axis=1)], axis=1)
    pos = pos0 + jnp.arange(t_len)
    outs = []
    for g, win in enumerate(POOL_WINDOWS):
        sl = slice(g * POOL_GROUP_DIM, (g + 1) * POOL_GROUP_DIM)
        wsum = cs[:, POOL_HIST + 1:POOL_HIST + 1 + t_len, sl] - cs[:, POOL_HIST + 1 - win:POOL_HIST + 1 - win + t_len, sl]
        cnt = jnp.minimum(pos + 1, win).astype(jnp.float32)[None, :, None]
        outs.append(wsum / cnt - ext[:, POOL_HIST:, sl])
    pooled = jnp.stack(outs, axis=2).astype(p.dtype)
    mixed = jnp.einsum('btgc,gcd->btgd', pooled, w_pool).reshape(bsz, t_len, W_POOL)
    return mixed * scale, ext_in[:, -POOL_HIST:]


def trunk_layer(x, lp, pos0, past):
    bsz, t_len, _ = x.shape
    h = rmsnorm(x, lp['g_mix'])
    z = h @ lp['w_in']
    q, k, v, c_b, c_c, c_h, g_u, g_v, p_in = jnp.split(z, split_points(), axis=-1)
    pos = pos0 + jnp.arange(t_len)
    q = partial_rope(q.reshape(bsz, t_len, N_Q_HEADS, HEAD_DIM), pos)
    k = partial_rope(k.reshape(bsz, t_len, N_KV_HEADS, HEAD_DIM), pos)
    v = v.reshape(bsz, t_len, N_KV_HEADS, HEAD_DIM)
    if past is None:
        o_attn = window_attention_prompt(q, k, v, lp['sink'])
        k_state, v_state = k[:, -WINDOW:], v[:, -WINDOW:]
        conv_past = jnp.zeros((bsz, CONV_WIDTH - 1, W_CONV), x.dtype)
        pool_past = jnp.zeros((bsz, POOL_HIST, W_POOL), x.dtype)
        ffn_past = jnp.zeros((bsz, FFN_CONV_WIDTH - 1, 2 * D_FF), x.dtype)
    else:
        k_past, v_past, conv_past, pool_past, ffn_past = past
        o_attn = window_attention_sample(q, k, v, k_past, v_past, lp['sink'])
        k_state, v_state = k, v
    zc, conv_state = causal_dwconv(c_c * c_h, conv_past, lp['conv_w'])
    o_conv = c_b * zc
    o_gmlp, gmlp_rows = spatial_gating(g_u, g_v, lp['gmlp_ln_g'], lp['gmlp_ln_b'], lp['gmlp_w'], lp['gmlp_b'], past is not None)
    o_pool, pool_state = multiscale_pool(p_in, pool_past, pos0, lp['pool_w'], lp['pool_scale'])
    x = x + jnp.concatenate([o_attn, o_conv, o_gmlp, o_pool], axis=-1) @ lp['w_out']
    up = rmsnorm(x, lp['g_ffn']) @ lp['w_up']
    up, ffn_state = causal_dwconv(up, ffn_past, lp['ffn_conv_w'])
    gate, val = jnp.split(up, 2, axis=-1)
    x = x + (jax.nn.silu(gate) * val) @ lp['w_down']
    return x, (k_state, v_state, conv_state, pool_state, ffn_state, gmlp_rows)


def setup_inputs(seed: int = 0) -> dict:
    key = jax.random.key(seed)
    ks = jax.random.split(key, 24)
    f32 = jnp.float32

    def nrm(k, shape, s):
        return jax.random.normal(k, shape, f32) * s

    return {
        'x_prompt': nrm(ks[0], (BATCH, SEQ, D_MODEL), 1.0),
        'x_sample': nrm(ks[1], (DEC_BATCH, DEC_SEQ, D_MODEL), 1.0),
        'cache_attn_k': nrm(ks[2], (DEC_BATCH, DEPTH, WINDOW, N_KV_HEADS, HEAD_DIM), 1.0),
        'cache_attn_v': nrm(ks[3], (DEC_BATCH, DEPTH, WINDOW, N_KV_HEADS, HEAD_DIM), 1.0),
        'state_conv': nrm(ks[4], (DEC_BATCH, DEPTH, CONV_WIDTH - 1, W_CONV), 1.0),
        'state_pool': nrm(ks[5], (DEC_BATCH, DEPTH, POOL_HIST, W_POOL), 1.0),
        'state_ffn_conv': nrm(ks[6], (DEC_BATCH, DEPTH, FFN_CONV_WIDTH - 1, 2 * D_FF), 1.0),
        'g_mix': 1.0 + nrm(ks[7], (DEPTH, D_MODEL), 0.01),
        'w_in': nrm(ks[8], (DEPTH, D_MODEL, PROJ_WIDTH), D_MODEL ** -0.5),
        'attn_sink': nrm(ks[9], (DEPTH, N_Q_HEADS), 0.5),
        'conv_w': nrm(ks[10], (DEPTH, CONV_WIDTH, W_CONV), CONV_WIDTH ** -0.5),
        'gmlp_ln_g': 1.0 + nrm(ks[11], (DEPTH, W_GMLP), 0.01),
        'gmlp_ln_b': nrm(ks[12], (DEPTH, W_GMLP), 0.01),
        'gmlp_w': nrm(ks[13], (DEPTH, GMLP_GROUPS, GMLP_CHUNK, GMLP_CHUNK), GMLP_CHUNK ** -0.5),
        'gmlp_b': nrm(ks[14], (DEPTH, GMLP_GROUPS, GMLP_CHUNK), 0.01),
        'pool_w': nrm(ks[15], (DEPTH, len(POOL_WINDOWS), POOL_GROUP_DIM, POOL_GROUP_DIM), POOL_GROUP_DIM ** -0.5),
        'pool_scale': 1.0 + nrm(ks[16], (DEPTH, W_POOL), 0.01),
        'w_out': nrm(ks[17], (DEPTH, MIX_WIDTH, D_MODEL), MIX_WIDTH ** -0.5),
        'g_ffn': 1.0 + nrm(ks[18], (DEPTH, D_MODEL), 0.01),
        'w_up': nrm(ks[19], (DEPTH, D_MODEL, 2 * D_FF), D_MODEL ** -0.5),
        'ffn_conv_w': nrm(ks[20], (DEPTH, FFN_CONV_WIDTH, 2 * D_FF), FFN_CONV_WIDTH ** -0.5),
        'w_down': nrm(ks[21], (DEPTH, D_FF, D_MODEL), D_FF ** -0.5),
        'g_final': 1.0 + nrm(ks[22], (D_MODEL,), 0.01),
    }


def reference(x_prompt, x_sample, cache_attn_k, cache_attn_v, state_conv, state_pool, state_ffn_conv,
              g_mix, w_in, attn_sink, conv_w, gmlp_ln_g, gmlp_ln_b, gmlp_w, gmlp_b, pool_w, pool_scale,
              w_out, g_ffn, w_up, ffn_conv_w, w_down, g_final):
    def run(x, pos0, pasts):
        per_layer = []
        for l in range(DEPTH):
            lp = {'g_mix': g_mix[l], 'w_in': w_in[l], 'sink': attn_sink[l], 'conv_w': conv_w[l],
                  'gmlp_ln_g': gmlp_ln_g[l], 'gmlp_ln_b': gmlp_ln_b[l], 'gmlp_w': gmlp_w[l], 'gmlp_b': gmlp_b[l],
                  'pool_w': pool_w[l], 'pool_scale': pool_scale[l], 'w_out': w_out[l], 'g_ffn': g_ffn[l],
                  'w_up': w_up[l], 'ffn_conv_w': ffn_conv_w[l], 'w_down': w_down[l]}
            past = None if pasts is None else tuple(a[:, l] for a in pasts)
            x, st = trunk_layer(x, lp, pos0, past)
            per_layer.append(st)
        states = [jnp.stack([st[i] for st in per_layer], axis=1) for i in range(6)]
        return rmsnorm(x, g_final), states

    y_prompt, sp = run(x_prompt, 0, None)
    y_sample, ss = run(x_sample, PAST_LEN, (cache_attn_k, cache_attn_v, state_conv, state_pool, state_ffn_conv))
    return (y_prompt, y_sample, sp[0], sp[1], sp[2], sp[3], sp[4], ss[0], ss[1], ss[2], ss[3], ss[4], ss[5])
```

```python
import functools

import jax
import jax.numpy as jnp
from jax import lax
from jax.experimental import pallas as pl
from jax.experimental.pallas import tpu as pltpu

D_MODEL = 1024
DEPTH = 2
CHUNK = 64
HEAD_DIM = 64
N_Q_HEADS = 4
N_KV_HEADS = 2
KV_WIDTH = N_KV_HEADS * HEAD_DIM
W_ATTN = 256
W_CONV = 256
W_GMLP = 256
W_POOL = 256
WINDOW = 128
ROPE_THETA = 500000.0
ROT_DIM = 16
CONV_WIDTH = 3
GMLP_CHUNK = 128
GMLP_GROUPS = 4
GMLP_GROUP_DIM = 64
POOL_WINDOWS = (2, 4, 8, 16)
POOL_HIST = 15
D_FF = 2816
NORM_EPS = 1e-6
PROJ_WIDTH = 2048

LANES = 128
SUBLANES = 8
FF_CHUNK = 256
N_FF_CHUNKS = D_FF // FF_CHUNK
POOL_PAD = 16
VMEM_LIMIT = 56 * 1024 * 1024

_Q0, _K0, _V0 = 0, 256, 384
_CB0, _CC0, _CH0 = 512, 768, 1024
_GU0, _GV0 = 1280, 1536
_P0 = 1792


def _rmsnorm(x, g):
    ms = jnp.mean(x * x, axis=-1, keepdims=True)
    return x * lax.rsqrt(ms + NORM_EPS) * g


def _dot(a, b):
    return jnp.dot(a, b, preferred_element_type=jnp.float32)


def _dot_nt(a, b):
    return lax.dot_general(a, b, (((1,), (1,)), ((), ())), preferred_element_type=jnp.float32)


def _rope(x, cos_t, sin_lo, sin_hi):
    up = pltpu.roll(x, LANES - ROT_DIM // 2, axis=1)
    dn = pltpu.roll(x, ROT_DIM // 2, axis=1)
    return x * cos_t + up * sin_lo + dn * sin_hi


def _gelu(x):
    return jax.nn.gelu(x)


def _mixer_kernel(*refs, tm, nt, has_past, pos0):
    it = iter(refs)
    x_ref = next(it)
    cos_ref, sinlo_ref, sinhi_ref = next(it), next(it), next(it)
    if has_past:
        kpast_ref, vpast_ref, cpast_ref, ppast_ref = next(it), next(it), next(it), next(it)
    gmix_ref, win_ref, sink_ref, convw_ref = next(it), next(it), next(it), next(it)
    lng_ref, lnb_ref, gw_ref, gb_ref = next(it), next(it), next(it), next(it)
    pw_ref, pscale_ref, wout_ref = next(it), next(it), next(it)
    xo_ref, kst_ref, vst_ref, cst_ref, pst_ref = next(it), next(it), next(it), next(it), next(it)
    if has_past:
        grows_ref = next(it)
    kext, vext, mext, pext, mix = next(it), next(it), next(it), next(it), next(it)

    t = pl.program_id(1)
    cq = min(CHUNK, tm)
    nch = tm // cq
    kr = min(WINDOW, tm)
    gl = min(GMLP_CHUNK, tm)

    @pl.when(t == 0)
    def _init():
        mext[0:SUBLANES, :] = jnp.zeros((SUBLANES, W_CONV), jnp.float32)
        pext[0:POOL_PAD, :] = jnp.zeros((POOL_PAD, W_POOL), jnp.float32)
        if has_past:
            kext[0:WINDOW, :] = kpast_ref[...].astype(jnp.bfloat16)
            vext[0:WINDOW, :] = vpast_ref[...].astype(jnp.bfloat16)
            mext[SUBLANES - (CONV_WIDTH - 1):SUBLANES, :] = cpast_ref[...]
            pext[POOL_PAD - POOL_HIST:POOL_PAD, :] = ppast_ref[...]
        else:
            kext[0:WINDOW, :] = jnp.zeros((WINDOW, KV_WIDTH), jnp.bfloat16)
            vext[0:WINDOW, :] = jnp.zeros((WINDOW, KV_WIDTH), jnp.bfloat16)

    x = x_ref[...]
    h = _rmsnorm(x, gmix_ref[...]).astype(jnp.bfloat16)

    zq = _dot(h, win_ref[:, _Q0:_CB0])
    cos_t, sin_lo, sin_hi = cos_ref[...], sinlo_ref[...], sinhi_ref[...]
    scale = HEAD_DIM ** -0.5
    qa = _rope(zq[:, 0:128], cos_t, sin_lo, sin_hi) * scale
    qb = _rope(zq[:, 128:256], cos_t, sin_lo, sin_hi) * scale
    k = _rope(zq[:, 256:384], cos_t, sin_lo, sin_hi)
    v = zq[:, 384:512]
    kext[WINDOW:WINDOW + tm, :] = k.astype(jnp.bfloat16)
    vext[WINDOW:WINDOW + tm, :] = v.astype(jnp.bfloat16)
    kst_ref[...] = k[tm - kr:tm, :]
    vst_ref[...] = v[tm - kr:tm, :]

    lane = lax.broadcasted_iota(jnp.int32, (1, LANES), 1)
    lo = lane < HEAD_DIM
    row2 = lax.broadcasted_iota(jnp.int32, (2 * cq, 1), 0)
    nkeys = WINDOW + cq
    kidx = lax.broadcasted_iota(jnp.int32, (1, nkeys), 1)
    for c in range(nch):
        r0 = c * cq
        keys = kext[r0:r0 + nkeys, :]
        vals = vext[r0:r0 + nkeys, :]
        qa_c = qa[r0:r0 + cq, :]
        qb_c = qb[r0:r0 + cq, :]
        outs = []
        for j in range(N_KV_HEADS):
            msk = lo if j == 0 else jnp.logical_not(lo)
            qs = jnp.concatenate([jnp.where(msk, qa_c, 0.0), jnp.where(msk, qb_c, 0.0)], axis=0)
            s = _dot_nt(qs.astype(jnp.bfloat16), keys)
            if not has_past:
                first_valid = (2 - (t * nch + c)) * CHUNK
                s = jnp.where(kidx >= first_valid, s, -jnp.inf)
            sk = jnp.where(row2 < cq, sink_ref[2 * j], sink_ref[2 * j + 1])
            m = jnp.maximum(jnp.max(s, axis=-1, keepdims=True), sk)
            p = jnp.exp(s - m)
            den = jnp.sum(p, axis=-1, keepdims=True) + jnp.exp(sk - m)
            outs.append(_dot(p.astype(jnp.bfloat16), vals) / den)
        oa = jnp.where(lo, outs[0][0:cq], outs[1][0:cq])
        ob = jnp.where(lo, outs[0][cq:2 * cq], outs[1][cq:2 * cq])
        mix[r0:r0 + cq, 0:128] = oa.astype(jnp.bfloat16)
        mix[r0:r0 + cq, 128:256] = ob.astype(jnp.bfloat16)
    if nt > 1:
        kext[0:WINDOW, :] = kext[tm:tm + WINDOW, :]
        vext[0:WINDOW, :] = vext[tm:tm + WINDOW, :]

    zc = _dot(h, win_ref[:, _CB0:_GU0])
    mprod = zc[:, 256:512] * zc[:, 512:768]
    mext[SUBLANES:SUBLANES + tm, :] = mprod
    cw = convw_ref[...]
    yc = (mext[SUBLANES - 2:SUBLANES - 2 + tm, :] * cw[0:1, :]
          + mext[SUBLANES - 1:SUBLANES - 1 + tm, :] * cw[1:2, :]
          + mprod * cw[2:3, :])
    mix[:, 256:512] = (zc[:, 0:256] * yc).astype(jnp.bfloat16)
    cst_ref[...] = mext[SUBLANES + tm - 2:SUBLANES + tm, :]
    if nt > 1:
        mext[0:SUBLANES, :] = mext[tm:tm + SUBLANES, :]

    zg = _dot(h, win_ref[:, _GU0:_P0])
    u = _gelu(zg[:, 0:256])
    gv = _gelu(zg[:, 256:512])
    mu = jnp.mean(gv, axis=-1, keepdims=True)
    var = jnp.mean(jnp.square(gv - mu), axis=-1, keepdims=True)
    vn = (gv - mu) * lax.rsqrt(var + NORM_EPS) * lng_ref[...] + lnb_ref[...]
    if has_past:
        grows_ref[...] = vn
    gi = lax.broadcasted_iota(jnp.int32, (gl, gl), 0)
    gj = lax.broadcasted_iota(jnp.int32, (gl, gl), 1)
    tri = (gj // CHUNK) <= (gi // CHUNK)
    lane256 = lax.broadcasted_iota(jnp.int32, (1, W_GMLP), 1) // GMLP_GROUP_DIM
    wms = [jnp.where(tri, gw_ref[g, 0:gl, 0:gl], 0.0).astype(jnp.bfloat16) for g in range(GMLP_GROUPS)]
    bias = gb_ref[0:gl, :]
    for c in range(tm // gl):
        r0 = c * gl
        vc = vn[r0:r0 + gl, :]
        s = bias
        for g in range(GMLP_GROUPS):
            s = s + _dot(wms[g], jnp.where(lane256 == g, vc, 0.0).astype(jnp.bfloat16))
        mix[r0:r0 + gl, 512:768] = (u[r0:r0 + gl, :] * s).astype(jnp.bfloat16)

    zp = _dot(h, win_ref[:, _P0:PROJ_WIDTH])
    pext[POOL_PAD:POOL_PAD + tm, :] = zp
    acc = zp
    wsum = None
    for g, win in enumerate(POOL_WINDOWS):
        for back in range(win // 2 if g else 1, win):
            acc = acc + pext[POOL_PAD - back:POOL_PAD - back + tm, :]
        wsum = acc if wsum is None else jnp.where(lane256 >= g, acc, wsum)
    prow = lax.broadcasted_iota(jnp.int32, (tm, 1), 0) + (pos0 + 1) + t * tm
    winl = jnp.left_shift(2, lane256).astype(jnp.float32)
    cnt = jnp.minimum(prow.astype(jnp.float32), winl)
    pooled = wsum / cnt - zp
    mix[:, 768:1024] = (_dot(pooled.astype(jnp.bfloat16), pw_ref[...]) * pscale_ref[...]).astype(jnp.bfloat16)
    pst_ref[...] = pext[POOL_PAD + tm - POOL_HIST:POOL_PAD + tm, :]
    if nt > 1:
        pext[0:POOL_PAD, :] = pext[tm:tm + POOL_PAD, :]

    xo_ref[...] = x + _dot(mix[...], wout_ref[...])


def _const_spec(shape):
    nd = len(shape)
    return pl.BlockSpec(shape, lambda b, t, _nd=nd: (0,) * _nd, pipeline_mode=pl.Buffered(1))


def _mixer_call(x, tabs, past, lw, *, tm, pos0):
    bsz, t_len, _ = x.shape
    nt = t_len // tm
    has_past = past is not None
    kr = min(WINDOW, tm)
    in_specs = [pl.BlockSpec((None, tm, D_MODEL), lambda b, t: (b, t, 0))]
    in_specs += [pl.BlockSpec((tm, LANES), lambda b, t: (t, 0))] * 3
    args = [x, *tabs]
    if has_past:
        kp, vp, cp, pp = past
        in_specs += [pl.BlockSpec((None, WINDOW, KV_WIDTH), lambda b, t: (b, 0, 0)),
                     pl.BlockSpec((None, WINDOW, KV_WIDTH), lambda b, t: (b, 0, 0)),
                     pl.BlockSpec((None, CONV_WIDTH - 1, W_CONV), lambda b, t: (b, 0, 0)),
                     pl.BlockSpec((None, POOL_HIST, W_POOL), lambda b, t: (b, 0, 0))]
        args += [kp, vp, cp, pp]
    wnames = ('g_mix', 'w_in', 'sink', 'conv_w', 'ln_g', 'ln_b', 'gmlp_w', 'gmlp_b', 'pool_w', 'pool_scale', 'w_out')
    for n in wnames:
        a = lw[n]
        if n == 'sink':
            in_specs.append(pl.BlockSpec(memory_space=pltpu.SMEM))
        else:
            in_specs.append(_const_spec(a.shape))
        args.append(a)
    out_shape = [jax.ShapeDtypeStruct((bsz, t_len, D_MODEL), jnp.float32),
                 jax.ShapeDtypeStruct((bsz, kr, KV_WIDTH), jnp.float32),
                 jax.ShapeDtypeStruct((bsz, kr, KV_WIDTH), jnp.float32),
                 jax.ShapeDtypeStruct((bsz, CONV_WIDTH - 1, W_CONV), jnp.float32),
                 jax.ShapeDtypeStruct((bsz, POOL_HIST, W_POOL), jnp.float32)]
    out_specs = [pl.BlockSpec((None, tm, D_MODEL), lambda b, t: (b, t, 0)),
                 pl.BlockSpec((None, kr, KV_WIDTH), lambda b, t: (b, 0, 0)),
                 pl.BlockSpec((None, kr, KV_WIDTH), lambda b, t: (b, 0, 0)),
                 pl.BlockSpec((None, CONV_WIDTH - 1, W_CONV), lambda b, t: (b, 0, 0)),
                 pl.BlockSpec((None, POOL_HIST, W_POOL), lambda b, t: (b, 0, 0))]
    if has_past:
        out_shape.append(jax.ShapeDtypeStruct((bsz, t_len, W_GMLP), jnp.float32))
        out_specs.append(pl.BlockSpec((None, tm, W_GMLP), lambda b, t: (b, t, 0)))
    scratch = [pltpu.VMEM((WINDOW + tm, KV_WIDTH), jnp.bfloat16),
               pltpu.VMEM((WINDOW + tm, KV_WIDTH), jnp.bfloat16),
               pltpu.VMEM((SUBLANES + tm, W_CONV), jnp.float32),
               pltpu.VMEM((POOL_PAD + tm, W_POOL), jnp.float32),
               pltpu.VMEM((tm, D_MODEL), jnp.bfloat16)]
    return pl.pallas_call(
        functools.partial(_mixer_kernel, tm=tm, nt=nt, has_past=has_past, pos0=pos0),
        grid=(bsz, nt),
        in_specs=in_specs,
        out_specs=out_specs,
        out_shape=out_shape,
        scratch_shapes=scratch,
        compiler_params=pltpu.CompilerParams(
            dimension_semantics=("arbitrary", "arbitrary"), vmem_limit_bytes=VMEM_LIMIT),
        name="mixer_past" if has_past else "mixer",
    )(*args)


def _ffn_kernel(*refs, tm, nt, has_past, final):
    it = iter(refs)
    x_ref = next(it)
    if has_past:
        fpast_ref = next(it)
    gffn_ref, wup_ref, fcw_ref, wdown_ref = next(it), next(it), next(it), next(it)
    if final:
        gfin_ref = next(it)
    xo_ref, fst_ref = next(it), next(it)
    upext, carry, act = next(it), next(it), next(it)

    t = pl.program_id(1)
    cw2 = 2 * FF_CHUNK

    @pl.when(t == 0)
    def _init():
        carry[...] = jnp.zeros(carry.shape, jnp.float32)
        if has_past:
            for c in range(N_FF_CHUNKS):
                carry[c, SUBLANES - 2:SUBLANES, :] = fpast_ref[:, c * cw2:(c + 1) * cw2]

    x = x_ref[...]
    h = _rmsnorm(x, gffn_ref[...]).astype(jnp.bfloat16)
    for c in range(N_FF_CHUNKS):
        up = _dot(h, wup_ref[:, c * cw2:(c + 1) * cw2])
        upext[0:SUBLANES, :] = carry[c]
        upext[SUBLANES:SUBLANES + tm, :] = up
        cw = fcw_ref[:, c * cw2:(c + 1) * cw2]
        y = (upext[SUBLANES - 2:SUBLANES - 2 + tm, :] * cw[0:1, :]
             + upext[SUBLANES - 1:SUBLANES - 1 + tm, :] * cw[1:2, :]
             + up * cw[2:3, :])
        fst_ref[:, c * cw2:(c + 1) * cw2] = upext[SUBLANES + tm - 2:SUBLANES + tm, :]
        if nt > 1:
            carry[c] = upext[tm:tm + SUBLANES, :]
        gate = y[:, 0:FF_CHUNK]
        val = y[:, FF_CHUNK:cw2]
        act[:, c * FF_CHUNK:(c + 1) * FF_CHUNK] = (gate * jax.nn.sigmoid(gate) * val).astype(jnp.bfloat16)
    out = x + _dot(act[...], wdown_ref[...])
    if final:
        out = _rmsnorm(out, gfin_ref[...])
    xo_ref[...] = out


def _ffn_call(x, fpast, lw, g_final, *, tm):
    bsz, t_len, _ = x.shape
    nt = t_len // tm
    has_past = fpast is not None
    final = g_final is not None
    in_specs = [pl.BlockSpec((None, tm, D_MODEL), lambda b, t: (b, t, 0))]
    args = [x]
    if has_past:
        in_specs.append(pl.BlockSpec((None, CONV_WIDTH - 1, 2 * D_FF), lambda b, t: (b, 0, 0)))
        args.append(fpast)
    for n in ('g_ffn', 'w_up', 'ffn_conv_w', 'w_down'):
        in_specs.append(_const_spec(lw[n].shape))
        args.append(lw[n])
    if final:
        in_specs.append(_const_spec(g_final.shape))
        args.append(g_final)
    out_shape = [jax.ShapeDtypeStruct((bsz, t_len, D_MODEL), jnp.float32),
                 jax.ShapeDtypeStruct((bsz, CONV_WIDTH - 1, 2 * D_FF), jnp.float32)]
    out_specs = [pl.BlockSpec((None, tm, D_MODEL), lambda b, t: (b, t, 0)),
                 pl.BlockSpec((None, CONV_WIDTH - 1, 2 * D_FF), lambda b, t: (b, 0, 0))]
    scratch = [pltpu.VMEM((SUBLANES + tm, 2 * FF_CHUNK), jnp.float32),
               pltpu.VMEM((N_FF_CHUNKS, SUBLANES, 2 * FF_CHUNK), jnp.float32),
               pltpu.VMEM((tm, D_FF), jnp.bfloat16)]
    return pl.pallas_call(
        functools.partial(_ffn_kernel, tm=tm, nt=nt, has_past=has_past, final=final),
        grid=(bsz, nt),
        in_specs=in_specs,
        out_specs=out_specs,
        out_shape=out_shape,
        scratch_shapes=scratch,
        compiler_params=pltpu.CompilerParams(
            dimension_semantics=("arbitrary", "arbitrary"), vmem_limit_bytes=VMEM_LIMIT),
        name=("ffn_past" if has_past else "ffn") + ("_final" if final else ""),
    )(*args)


def _swap_mid_heads(a, axis):
    parts = jnp.split(a, N_Q_HEADS, axis=axis)
    return jnp.concatenate([parts[0], parts[2], parts[1], parts[3]], axis=axis)


def _ff_interleave(a):
    lead = a.shape[:-1]
    a = a.reshape(*lead, 2, N_FF_CHUNKS, FF_CHUNK)
    return jnp.swapaxes(a, -3, -2).reshape(*lead, 2 * D_FF)


def _ff_deinterleave(a):
    lead = a.shape[:-1]
    a = a.reshape(*lead, N_FF_CHUNKS, 2, FF_CHUNK)
    return jnp.swapaxes(a, -3, -2).reshape(*lead, 2 * D_FF)


def _rope_tables(pos0, t_len):
    half = ROT_DIM // 2
    inv_freq = jnp.power(jnp.float32(ROPE_THETA), -jnp.arange(half, dtype=jnp.float32) * (2.0 / ROT_DIM))
    ang = (pos0 + jnp.arange(t_len)).astype(jnp.float32)[:, None] * inv_freq[None, :]
    cos, sin = jnp.cos(ang), jnp.sin(ang)
    ones = jnp.ones((t_len, HEAD_DIM - ROT_DIM), jnp.float32)
    zeros8 = jnp.zeros((t_len, half), jnp.float32)
    zeros = jnp.zeros((t_len, HEAD_DIM - ROT_DIM), jnp.float32)
    cos_h = jnp.concatenate([cos, cos, ones], axis=1)
    sinlo_h = jnp.concatenate([-sin, zeros8, zeros], axis=1)
    sinhi_h = jnp.concatenate([zeros8, sin, zeros], axis=1)
    return tuple(jnp.concatenate([a, a], axis=1) for a in (cos_h, sinlo_h, sinhi_h))


def _layer_weights(l, g_mix, w_in, attn_sink, conv_w, gmlp_ln_g, gmlp_ln_b, gmlp_w, gmlp_b, pool_w, pool_scale,
                   w_out, g_ffn, w_up, ffn_conv_w, w_down):
    win = w_in[l]
    win = jnp.concatenate([_swap_mid_heads(win[:, :W_ATTN], 1), win[:, W_ATTN:]], axis=1).astype(jnp.bfloat16)
    wout = w_out[l]
    wout = jnp.concatenate([_swap_mid_heads(wout[:W_ATTN], 0), wout[W_ATTN:]], axis=0).astype(jnp.bfloat16)
    pw = jnp.zeros((W_POOL, W_POOL), jnp.float32)
    for g in range(len(POOL_WINDOWS)):
        sl = slice(g * 64, (g + 1) * 64)
        pw = pw.at[sl, sl].set(pool_w[l, g])
    return {
        'g_mix': g_mix[l][None, :],
        'w_in': win,
        'sink': attn_sink[l],
        'conv_w': conv_w[l],
        'ln_g': gmlp_ln_g[l][None, :],
        'ln_b': gmlp_ln_b[l][None, :],
        'gmlp_w': gmlp_w[l],
        'gmlp_b': jnp.repeat(gmlp_b[l].T, GMLP_GROUP_DIM, axis=1),
        'pool_w': pw.astype(jnp.bfloat16),
        'pool_scale': pool_scale[l][None, :],
        'w_out': wout,
        'g_ffn': g_ffn[l][None, :],
        'w_up': _ff_interleave(w_up[l]).astype(jnp.bfloat16),
        'ffn_conv_w': _ff_interleave(ffn_conv_w[l]),
        'w_down': w_down[l].astype(jnp.bfloat16),
    }


def _run(x, pos0, pasts, lws, g_final, tm):
    bsz, t_len, _ = x.shape
    tabs = _rope_tables(pos0, t_len)
    states = []
    for l in range(DEPTH):
        lw = lws[l]
        if pasts is None:
            mpast, fpast = None, None
        else:
            ck, cv, sc, sp, sf = pasts
            mpast = (ck[:, l].reshape(bsz, WINDOW, KV_WIDTH), cv[:, l].reshape(bsz, WINDOW, KV_WIDTH),
                     sc[:, l], sp[:, l])
            fpast = _ff_interleave(sf[:, l])
        mo = _mixer_call(x, tabs, mpast, lw, tm=tm, pos0=pos0)
        x_mid, kst, vst, cst, pst = mo[:5]
        gfin = g_final[None, :] if l == DEPTH - 1 else None
        x, fst = _ffn_call(x_mid, fpast, lw, gfin, tm=tm)
        st = [kst.reshape(bsz, -1, N_KV_HEADS, HEAD_DIM), vst.reshape(bsz, -1, N_KV_HEADS, HEAD_DIM),
              cst, pst, _ff_deinterleave(fst)]
        if pasts is not None:
            st.append(mo[5])
        states.append(st)
    stacked = [jnp.stack([st[i] for st in states], axis=1) for i in range(len(states[0]))]
    return x, stacked


def kernel(x_prompt, x_sample, cache_attn_k, cache_attn_v, state_conv, state_pool, state_ffn_conv, g_mix, w_in,
           attn_sink, conv_w, gmlp_ln_g, gmlp_ln_b, gmlp_w, gmlp_b, pool_w, pool_scale, w_out, g_ffn, w_up,
           ffn_conv_w, w_down, g_final):
    lws = [_layer_weights(l, g_mix, w_in, attn_sink, conv_w, gmlp_ln_g, gmlp_ln_b, gmlp_w, gmlp_b, pool_w,
                          pool_scale, w_out, g_ffn, w_up, ffn_conv_w, w_down) for l in range(DEPTH)]
    y_prompt, sp = _run(x_prompt, 0, None, lws, g_final, tm=256)
    past_len = 2048
    y_sample, ss = _run(x_sample, past_len, (cache_attn_k, cache_attn_v, state_conv, state_pool, state_ffn_conv),
                        lws, g_final, tm=x_sample.shape[1])
    return (y_prompt, y_sample, sp[0], sp[1], sp[2], sp[3], sp[4], ss[0], ss[1], ss[2], ss[3], ss[4], ss[5])
```

```python
import functools

import numpy as np
import jax
import jax.numpy as jnp
from jax import lax
from jax.experimental import pallas as pl
from jax.experimental.pallas import tpu as pltpu

D_MODEL = 1024
DEPTH = 2
CHUNK = 64
HEAD_DIM = 64
N_Q_HEADS = 4
N_KV_HEADS = 2
KV_WIDTH = N_KV_HEADS * HEAD_DIM
W_ATTN = 256
W_CONV = 256
W_GMLP = 256
W_POOL = 256
WINDOW = 128
ROPE_THETA = 500000.0
ROT_DIM = 16
CONV_WIDTH = 3
GMLP_CHUNK = 128
GMLP_GROUPS = 4
GMLP_GROUP_DIM = 64
POOL_WINDOWS = (2, 4, 8, 16)
POOL_HIST = 15
D_FF = 2816
NORM_EPS = 1e-6
PROJ_WIDTH = 2048
PAST_LEN = 2048

LANES = 128
SUBLANES = 8
FF_CHUNK = 256
N_FF_CHUNKS = D_FF // FF_CHUNK
POOL_PAD = 24
ATTN_BLOCK = 2 * CHUNK
VMEM_LIMIT = 56 * 1024 * 1024

_Q0, _K0, _V0 = 0, 256, 384
_CB0, _CC0, _CH0 = 512, 768, 1024
_GU0, _GV0 = 1280, 1536
_P0 = 1792


def _rmsnorm(x, g):
    ms = jnp.mean(x * x, axis=-1, keepdims=True)
    return x * lax.rsqrt(ms + NORM_EPS) * g


def _dot(a, b):
    return jnp.dot(a, b, preferred_element_type=jnp.float32)


def _dot_nt(a, b):
    return lax.dot_general(a, b, (((1,), (1,)), ((), ())), preferred_element_type=jnp.float32)


def _rope(x, cos_t, sin_lo, sin_hi):
    up = pltpu.roll(x, LANES - ROT_DIM // 2, axis=1)
    dn = pltpu.roll(x, ROT_DIM // 2, axis=1)
    return x * cos_t + up * sin_lo + dn * sin_hi


def _mixer_kernel(*refs, tm, nt, has_past, pos0):
    it = iter(refs)
    x_ref = next(it)
    cos_ref, sinlo_ref, sinhi_ref = next(it), next(it), next(it)
    if has_past:
        kpast_ref, vpast_ref, cpast_ref, ppast_ref = next(it), next(it), next(it), next(it)
    else:
        bias_ref = next(it)
    gmix_ref, win_ref, sink_ref, convw_ref = next(it), next(it), next(it), next(it)
    lng_ref, lnb_ref, gw_ref, gb_ref = next(it), next(it), next(it), next(it)
    pw_ref, pscale_ref, wout_ref = next(it), next(it), next(it)
    xo_ref, kst_ref, vst_ref, cst_ref, pst_ref = next(it), next(it), next(it), next(it), next(it)
    if has_past:
        grows_ref = next(it)
    kext, vext, mext, pext, pa, pb, mix = (next(it), next(it), next(it), next(it), next(it), next(it), next(it))

    t = pl.program_id(1)
    qn = min(ATTN_BLOCK, tm)
    nblk = tm // qn
    nkeys = WINDOW + qn
    kr = min(WINDOW, tm)
    gl = min(GMLP_CHUNK, tm)

    @pl.when(t == 0)
    def _init():
        mext[0:SUBLANES, :] = jnp.zeros((SUBLANES, W_CONV), jnp.float32)
        pext[0:POOL_PAD, :] = jnp.zeros((POOL_PAD, W_POOL), jnp.float32)
        pa[0:SUBLANES, :] = jnp.zeros((SUBLANES, W_POOL), jnp.float32)
        pb[0:SUBLANES, :] = jnp.zeros((SUBLANES, W_POOL), jnp.float32)
        if has_past:
            kext[0:WINDOW, :] = kpast_ref[...].astype(jnp.bfloat16)
            vext[0:WINDOW, :] = vpast_ref[...].astype(jnp.bfloat16)
            mext[SUBLANES - (CONV_WIDTH - 1):SUBLANES, :] = cpast_ref[...]
            pext[POOL_PAD - POOL_HIST:POOL_PAD, :] = ppast_ref[...]
        else:
            kext[0:WINDOW, :] = jnp.zeros((WINDOW, KV_WIDTH), jnp.bfloat16)
            vext[0:WINDOW, :] = jnp.zeros((WINDOW, KV_WIDTH), jnp.bfloat16)

    x = x_ref[...]
    h = _rmsnorm(x, gmix_ref[...]).astype(jnp.bfloat16)

    zq = _dot(h, win_ref[:, _Q0:_CB0])
    cos_t, sin_lo, sin_hi = cos_ref[...], sinlo_ref[...], sinhi_ref[...]
    scale = HEAD_DIM ** -0.5
    qa = _rope(zq[:, 0:128], cos_t, sin_lo, sin_hi) * scale
    qb = _rope(zq[:, 128:256], cos_t, sin_lo, sin_hi) * scale
    k = _rope(zq[:, 256:384], cos_t, sin_lo, sin_hi)
    v = zq[:, 384:512]
    kext[WINDOW:WINDOW + tm, :] = k.astype(jnp.bfloat16)
    vext[WINDOW:WINDOW + tm, :] = v.astype(jnp.bfloat16)
    kst_ref[...] = k[tm - kr:tm, :]
    vst_ref[...] = v[tm - kr:tm, :]

    lane = lax.broadcasted_iota(jnp.int32, (1, LANES), 1)
    lo = lane < HEAD_DIM
    row4 = lax.broadcasted_iota(jnp.int32, (4 * qn, 1), 0)
    sk = jnp.where(row4 < qn, sink_ref[0],
                   jnp.where(row4 < 2 * qn, sink_ref[1], jnp.where(row4 < 3 * qn, sink_ref[2], sink_ref[3])))
    scores = []
    for b in range(nblk):
        r0 = b * qn
        qa_b, qb_b = qa[r0:r0 + qn, :], qb[r0:r0 + qn, :]
        qs = jnp.concatenate([jnp.where(lo, qa_b, 0.0), jnp.where(lo, qb_b, 0.0),
                              jnp.where(lo, 0.0, qa_b), jnp.where(lo, 0.0, qb_b)], axis=0)
        s = _dot_nt(qs.astype(jnp.bfloat16), kext[r0:r0 + nkeys, :])
        if not has_past:
            s = s + (bias_ref[0] if b > 0 else bias_ref[jnp.where(t == 0, 1, 0)])
        scores.append(s)
    probs, dens = [], []
    for s in scores:
        m = jnp.maximum(jnp.max(s, axis=-1, keepdims=True), sk)
        p = jnp.exp(s - m)
        dens.append(jnp.sum(p, axis=-1, keepdims=True) + jnp.exp(sk - m))
        probs.append(p.astype(jnp.bfloat16))
    for b in range(nblk):
        r0 = b * qn
        o = _dot(probs[b], vext[r0:r0 + nkeys, :]) / dens[b]
        oa = jnp.where(lo, o[0:qn], o[2 * qn:3 * qn])
        ob = jnp.where(lo, o[qn:2 * qn], o[3 * qn:4 * qn])
        mix[r0:r0 + qn, 0:128] = oa.astype(jnp.bfloat16)
        mix[r0:r0 + qn, 128:256] = ob.astype(jnp.bfloat16)
    if nt > 1:
        kext[0:WINDOW, :] = kext[tm:tm + WINDOW, :]
        vext[0:WINDOW, :] = vext[tm:tm + WINDOW, :]

    zc = _dot(h, win_ref[:, _CB0:_GU0])
    mprod = zc[:, 256:512] * zc[:, 512:768]
    mext[SUBLANES:SUBLANES + tm, :] = mprod
    cw = convw_ref[...]
    yc = (mext[SUBLANES - 2:SUBLANES - 2 + tm, :] * cw[0:1, :]
          + mext[SUBLANES - 1:SUBLANES - 1 + tm, :] * cw[1:2, :]
          + mprod * cw[2:3, :])
    mix[:, 256:512] = (zc[:, 0:256] * yc).astype(jnp.bfloat16)
    cst_ref[...] = mext[SUBLANES + tm - 2:SUBLANES + tm, :]
    if nt > 1:
        mext[0:SUBLANES, :] = mext[tm:tm + SUBLANES, :]

    zg = _dot(h, win_ref[:, _GU0:_P0])
    u = jax.nn.gelu(zg[:, 0:256])
    gv = jax.nn.gelu(zg[:, 256:512])
    mu = jnp.mean(gv, axis=-1, keepdims=True)
    var = jnp.mean(jnp.square(gv - mu), axis=-1, keepdims=True)
    vn = (gv - mu) * lax.rsqrt(var + NORM_EPS) * lng_ref[...] + lnb_ref[...]
    if has_past:
        grows_ref[...] = vn
    vnb = vn.astype(jnp.bfloat16)
    gi = lax.broadcasted_iota(jnp.int32, (gl, gl), 0)
    gj = lax.broadcasted_iota(jnp.int32, (gl, gl), 1)
    tri = (gj // CHUNK) <= (gi // CHUNK)
    lane256 = lax.broadcasted_iota(jnp.int32, (1, W_GMLP), 1) // GMLP_GROUP_DIM
    wstack = jnp.concatenate([jnp.where(tri, gw_ref[g, 0:gl, 0:gl], 0.0).astype(jnp.bfloat16)
                              for g in range(GMLP_GROUPS)], axis=0)
    bias_g = gb_ref[0:gl, :]
    for c in range(tm // gl):
        r0 = c * gl
        full = _dot(wstack, vnb[r0:r0 + gl, :])
        s = full[0:gl]
        for g in range(1, GMLP_GROUPS):
            s = jnp.where(lane256 >= g, full[g * gl:(g + 1) * gl], s)
        mix[r0:r0 + gl, 512:768] = (u[r0:r0 + gl, :] * (s + bias_g)).astype(jnp.bfloat16)

    zp = _dot(h, win_ref[:, _P0:PROJ_WIDTH])
    pext[POOL_PAD:POOL_PAD + tm, :] = zp
    n = POOL_PAD - SUBLANES + tm
    pa[SUBLANES:SUBLANES + n, :] = pext[SUBLANES:SUBLANES + n, :] + pext[SUBLANES - 1:SUBLANES - 1 + n, :]
    pb[SUBLANES:SUBLANES + n, :] = pa[SUBLANES:SUBLANES + n, :] + pa[SUBLANES - 2:SUBLANES - 2 + n, :]
    s2 = pa[POOL_PAD:POOL_PAD + tm, 0:LANES]
    s4 = pb[POOL_PAD:POOL_PAD + tm, 0:LANES]
    pa[SUBLANES:SUBLANES + n, LANES:] = pb[SUBLANES:SUBLANES + n, LANES:] + pb[SUBLANES - 4:SUBLANES - 4 + n, LANES:]
    s8 = pa[POOL_PAD:POOL_PAD + tm, LANES:]
    s16 = s8 + pa[POOL_PAD - 8:POOL_PAD - 8 + tm, LANES:]
    wsum = jnp.concatenate([jnp.where(lo, s2, s4), jnp.where(lo, s8, s16)], axis=1)
    prow = lax.broadcasted_iota(jnp.int32, (tm, 1), 0) + (pos0 + 1) + t * tm
    winl = jnp.left_shift(2, lane256).astype(jnp.float32)
    cnt = jnp.minimum(prow.astype(jnp.float32), winl)
    pooled = wsum / cnt - zp
    mix[:, 768:1024] = (_dot(pooled.astype(jnp.bfloat16), pw_ref[...]) * pscale_ref[...]).astype(jnp.bfloat16)
    pst_ref[...] = pext[POOL_PAD + tm - POOL_HIST:POOL_PAD + tm, :]
    if nt > 1:
        pext[0:POOL_PAD, :] = pext[tm:tm + POOL_PAD, :]

    xo_ref[...] = x + _dot(mix[...], wout_ref[...])


def _const_spec(shape):
    nd = len(shape)
    return pl.BlockSpec(shape, lambda b, t, _nd=nd: (0,) * _nd, pipeline_mode=pl.Buffered(1))


def _window_bias(qn):
    rows = (np.arange(4 * qn) % qn) // CHUNK
    keys = np.arange(WINDOW + qn)
    ok = (keys[None, :] >= rows[:, None] * CHUNK) & (keys[None, :] < (rows[:, None] + 3) * CHUNK)
    first = ok & (keys[None, :] >= WINDOW)
    return np.where(np.stack([ok, first]), 0.0, -np.inf).astype(np.float32)


def _mixer_call(x, tabs, past, lw, *, tm, pos0):
    bsz, t_len, _ = x.shape
    nt = t_len // tm
    has_past = past is not None
    kr = min(WINDOW, tm)
    qn = min(ATTN_BLOCK, tm)
    in_specs = [pl.BlockSpec((None, tm, D_MODEL), lambda b, t: (b, t, 0))]
    in_specs += [pl.BlockSpec((tm, LANES), lambda b, t: (t, 0))] * 3
    args = [x, *tabs]
    if has_past:
        kp, vp, cp, pp = past
        in_specs += [pl.BlockSpec((None, WINDOW, KV_WIDTH), lambda b, t: (b, 0, 0)),
                     pl.BlockSpec((None, WINDOW, KV_WIDTH), lambda b, t: (b, 0, 0)),
                     pl.BlockSpec((None, CONV_WIDTH - 1, W_CONV), lambda b, t: (b, 0, 0)),
                     pl.BlockSpec((None, POOL_HIST, W_POOL), lambda b, t: (b, 0, 0))]
        args += [kp, vp, cp, pp]
    else:
        bias = jnp.asarray(_window_bias(qn))
        in_specs.append(_const_spec(bias.shape))
        args.append(bias)
    wnames = ('g_mix', 'w_in', 'sink', 'conv_w', 'ln_g', 'ln_b', 'gmlp_w', 'gmlp_b', 'pool_w', 'pool_scale', 'w_out')
    for n in wnames:
        a = lw[n]
        if n == 'sink':
            in_specs.append(pl.BlockSpec(memory_space=pltpu.SMEM))
        else:
            in_specs.append(_const_spec(a.shape))
        args.append(a)
    out_shape = [jax.ShapeDtypeStruct((bsz, t_len, D_MODEL), jnp.float32),
                 jax.ShapeDtypeStruct((bsz, kr, KV_WIDTH), jnp.float32),
                 jax.ShapeDtypeStruct((bsz, kr, KV_WIDTH), jnp.float32),
                 jax.ShapeDtypeStruct((bsz, CONV_WIDTH - 1, W_CONV), jnp.float32),
                 jax.ShapeDtypeStruct((bsz, POOL_HIST, W_POOL), jnp.float32)]
    out_specs = [pl.BlockSpec((None, tm, D_MODEL), lambda b, t: (b, t, 0)),
                 pl.BlockSpec((None, kr, KV_WIDTH), lambda b, t: (b, 0, 0)),
                 pl.BlockSpec((None, kr, KV_WIDTH), lambda b, t: (b, 0, 0)),
                 pl.BlockSpec((None, CONV_WIDTH - 1, W_CONV), lambda b, t: (b, 0, 0)),
                 pl.BlockSpec((None, POOL_HIST, W_POOL), lambda b, t: (b, 0, 0))]
    if has_past:
        out_shape.append(jax.ShapeDtypeStruct((bsz, t_len, W_GMLP), jnp.float32))
        out_specs.append(pl.BlockSpec((None, tm, W_GMLP), lambda b, t: (b, t, 0)))
    scratch = [pltpu.VMEM((WINDOW + tm, KV_WIDTH), jnp.bfloat16),
               pltpu.VMEM((WINDOW + tm, KV_WIDTH), jnp.bfloat16),
               pltpu.VMEM((SUBLANES + tm, W_CONV), jnp.float32),
               pltpu.VMEM((POOL_PAD + tm, W_POOL), jnp.float32),
               pltpu.VMEM((POOL_PAD + tm, W_POOL), jnp.float32),
               pltpu.VMEM((POOL_PAD + tm, W_POOL), jnp.float32),
               pltpu.VMEM((tm, D_MODEL), jnp.bfloat16)]
    return pl.pallas_call(
        functools.partial(_mixer_kernel, tm=tm, nt=nt, has_past=has_past, pos0=pos0),
        grid=(bsz, nt),
        in_specs=in_specs,
        out_specs=out_specs,
        out_shape=out_shape,
        scratch_shapes=scratch,
        compiler_params=pltpu.CompilerParams(
            dimension_semantics=("arbitrary", "arbitrary"), vmem_limit_bytes=VMEM_LIMIT),
        name="mixer_past" if has_past else "mixer",
    )(*args)


def _ffn_kernel(*refs, tm, nt, has_past, final):
    it = iter(refs)
    x_ref = next(it)
    if has_past:
        fpast_ref = next(it)
    gffn_ref, wup_ref, fcw_ref, wdown_ref = next(it), next(it), next(it), next(it)
    if final:
        gfin_ref = next(it)
    xo_ref, fst_ref = next(it), next(it)
    upext, carry, act = next(it), next(it), next(it)

    t = pl.program_id(1)
    fc = FF_CHUNK
    halves = (0, D_FF)

    @pl.when(t == 0)
    def _init():
        carry[...] = jnp.zeros(carry.shape, jnp.float32)
        if has_past:
            for c in range(N_FF_CHUNKS):
                for hi, off in enumerate(halves):
                    carry[c, SUBLANES - 2:SUBLANES, hi * fc:(hi + 1) * fc] = fpast_ref[:, off + c * fc:off + (c + 1) * fc]

    x = x_ref[...]
    h = _rmsnorm(x, gffn_ref[...]).astype(jnp.bfloat16)
    for c in range(N_FF_CHUNKS):
        upext[0:SUBLANES, :] = carry[c]
        ys = []
        for hi, off in enumerate(halves):
            cols = slice(off + c * fc, off + (c + 1) * fc)
            lanes = slice(hi * fc, (hi + 1) * fc)
            up = _dot(h, wup_ref[:, cols])
            upext[SUBLANES:SUBLANES + tm, lanes] = up
            cw = fcw_ref[:, cols]
            ys.append(upext[SUBLANES - 2:SUBLANES - 2 + tm, lanes] * cw[0:1, :]
                      + upext[SUBLANES - 1:SUBLANES - 1 + tm, lanes] * cw[1:2, :]
                      + up * cw[2:3, :])
            fst_ref[:, cols] = upext[SUBLANES + tm - 2:SUBLANES + tm, lanes]
        if nt > 1:
            carry[c] = upext[tm:tm + SUBLANES, :]
        gate, val = ys
        act[:, c * fc:(c + 1) * fc] = (gate * jax.nn.sigmoid(gate) * val).astype(jnp.bfloat16)
    out = x + _dot(act[...], wdown_ref[...])
    if final:
        out = _rmsnorm(out, gfin_ref[...])
    xo_ref[...] = out


def _ffn_call(x, fpast, lw, g_final, *, tm):
    bsz, t_len, _ = x.shape
    nt = t_len // tm
    has_past = fpast is not None
    final = g_final is not None
    in_specs = [pl.BlockSpec((None, tm, D_MODEL), lambda b, t: (b, t, 0))]
    args = [x]
    if has_past:
        in_specs.append(pl.BlockSpec((None, CONV_WIDTH - 1, 2 * D_FF), lambda b, t: (b, 0, 0)))
        args.append(fpast)
    for n in ('g_ffn', 'w_up', 'ffn_conv_w', 'w_down'):
        in_specs.append(_const_spec(lw[n].shape))
        args.append(lw[n])
    if final:
        in_specs.append(_const_spec(g_final.shape))
        args.append(g_final)
    out_shape = [jax.ShapeDtypeStruct((bsz, t_len, D_MODEL), jnp.float32),
                 jax.ShapeDtypeStruct((bsz, CONV_WIDTH - 1, 2 * D_FF), jnp.float32)]
    out_specs = [pl.BlockSpec((None, tm, D_MODEL), lambda b, t: (b, t, 0)),
                 pl.BlockSpec((None, CONV_WIDTH - 1, 2 * D_FF), lambda b, t: (b, 0, 0))]
    scratch = [pltpu.VMEM((SUBLANES + tm, 2 * FF_CHUNK), jnp.float32),
               pltpu.VMEM((N_FF_CHUNKS, SUBLANES, 2 * FF_CHUNK), jnp.float32),
               pltpu.VMEM((tm, D_FF), jnp.bfloat16)]
    return pl.pallas_call(
        functools.partial(_ffn_kernel, tm=tm, nt=nt, has_past=has_past, final=final),
        grid=(bsz, nt),
        in_specs=in_specs,
        out_specs=out_specs,
        out_shape=out_shape,
        scratch_shapes=scratch,
        compiler_params=pltpu.CompilerParams(
            dimension_semantics=("arbitrary", "arbitrary"), vmem_limit_bytes=VMEM_LIMIT),
        name=("ffn_past" if has_past else "ffn") + ("_final" if final else ""),
    )(*args)


def _swap_mid_heads(a, axis):
    parts = jnp.split(a, N_Q_HEADS, axis=axis)
    return jnp.concatenate([parts[0], parts[2], parts[1], parts[3]], axis=axis)


def _rope_tables(pos0, t_len):
    half = ROT_DIM // 2
    inv_freq = jnp.power(jnp.float32(ROPE_THETA), -jnp.arange(half, dtype=jnp.float32) * (2.0 / ROT_DIM))
    ang = (pos0 + jnp.arange(t_len)).astype(jnp.float32)[:, None] * inv_freq[None, :]
    cos, sin = jnp.cos(ang), jnp.sin(ang)
    ones = jnp.ones((t_len, HEAD_DIM - ROT_DIM), jnp.float32)
    zeros8 = jnp.zeros((t_len, half), jnp.float32)
    zeros = jnp.zeros((t_len, HEAD_DIM - ROT_DIM), jnp.float32)
    cos_h = jnp.concatenate([cos, cos, ones], axis=1)
    sinlo_h = jnp.concatenate([-sin, zeros8, zeros], axis=1)
    sinhi_h = jnp.concatenate([zeros8, sin, zeros], axis=1)
    return tuple(jnp.concatenate([a, a], axis=1) for a in (cos_h, sinlo_h, sinhi_h))


def _layer_weights(l, g_mix, w_in, attn_sink, conv_w, gmlp_ln_g, gmlp_ln_b, gmlp_w, gmlp_b, pool_w, pool_scale,
                   w_out, g_ffn, w_up, ffn_conv_w, w_down):
    win = w_in[l]
    win = jnp.concatenate([_swap_mid_heads(win[:, :W_ATTN], 1), win[:, W_ATTN:]], axis=1).astype(jnp.bfloat16)
    wout = w_out[l]
    wout = jnp.concatenate([_swap_mid_heads(wout[:W_ATTN], 0), wout[W_ATTN:]], axis=0).astype(jnp.bfloat16)
    pw = jnp.zeros((W_POOL, W_POOL), jnp.float32)
    for g in range(len(POOL_WINDOWS)):
        sl = slice(g * 64, (g + 1) * 64)
        pw = pw.at[sl, sl].set(pool_w[l, g])
    return {
        'g_mix': g_mix[l][None, :],
        'w_in': win,
        'sink': attn_sink[l],
        'conv_w': conv_w[l],
        'ln_g': gmlp_ln_g[l][None, :],
        'ln_b': gmlp_ln_b[l][None, :],
        'gmlp_w': gmlp_w[l],
        'gmlp_b': jnp.repeat(gmlp_b[l].T, GMLP_GROUP_DIM, axis=1),
        'pool_w': pw.astype(jnp.bfloat16),
        'pool_scale': pool_scale[l][None, :],
        'w_out': wout,
        'g_ffn': g_ffn[l][None, :],
        'w_up': w_up[l].astype(jnp.bfloat16),
        'ffn_conv_w': ffn_conv_w[l],
        'w_down': w_down[l].astype(jnp.bfloat16),
    }


def _run(x, pos0, pasts, lws, g_final, tm):
    bsz, t_len, _ = x.shape
    tabs = _rope_tables(pos0, t_len)
    states = []
    for l in range(DEPTH):
        lw = lws[l]
        if pasts is None:
            mpast, fpast = None, None
        else:
            ck, cv, sc, sp, sf = pasts
            mpast = (ck[:, l].reshape(bsz, WINDOW, KV_WIDTH), cv[:, l].reshape(bsz, WINDOW, KV_WIDTH),
                     sc[:, l], sp[:, l])
            fpast = sf[:, l]
        mo = _mixer_call(x, tabs, mpast, lw, tm=tm, pos0=pos0)
        x_mid, kst, vst, cst, pst = mo[:5]
        gfin = g_final[None, :] if l == DEPTH - 1 else None
        x, fst = _ffn_call(x_mid, fpast, lw, gfin, tm=tm)
        st = [kst.reshape(bsz, -1, N_KV_HEADS, HEAD_DIM), vst.reshape(bsz, -1, N_KV_HEADS, HEAD_DIM),
              cst, pst, fst]
        if pasts is not None:
            st.append(mo[5])
        states.append(st)
    stacked = [jnp.stack([st[i] for st in states], axis=1) for i in range(len(states[0]))]
    return x, stacked


def kernel(x_prompt, x_sample, cache_attn_k, cache_attn_v, state_conv, state_pool, state_ffn_conv, g_mix, w_in,
           attn_sink, conv_w, gmlp_ln_g, gmlp_ln_b, gmlp_w, gmlp_b, pool_w, pool_scale, w_out, g_ffn, w_up,
           ffn_conv_w, w_down, g_final):
    lws = [_layer_weights(l, g_mix, w_in, attn_sink, conv_w, gmlp_ln_g, gmlp_ln_b, gmlp_w, gmlp_b, pool_w,
                          pool_scale, w_out, g_ffn, w_up, ffn_conv_w, w_down) for l in range(DEPTH)]
    y_prompt, sp = _run(x_prompt, 0, None, lws, g_final, tm=512)
    y_sample, ss = _run(x_sample, PAST_LEN, (cache_attn_k, cache_attn_v, state_conv, state_pool, state_ffn_conv),
                        lws, g_final, tm=x_sample.shape[1])
    return (y_prompt, y_sample, sp[0], sp[1], sp[2], sp[3], sp[4], ss[0], ss[1], ss[2], ss[3], ss[4], ss[5])
```

```python
import functools

import numpy as np
import jax
import jax.numpy as jnp
from jax import lax
from jax.experimental import pallas as pl
from jax.experimental.pallas import tpu as pltpu

D_MODEL = 1024
DEPTH = 2
CHUNK = 64
HEAD_DIM = 64
N_Q_HEADS = 4
N_KV_HEADS = 2
KV_WIDTH = N_KV_HEADS * HEAD_DIM
W_ATTN = 256
W_CONV = 256
W_GMLP = 256
W_POOL = 256
WINDOW = 128
ROPE_THETA = 500000.0
ROT_DIM = 16
CONV_WIDTH = 3
GMLP_CHUNK = 128
GMLP_GROUPS = 4
GMLP_GROUP_DIM = 64
POOL_WINDOWS = (2, 4, 8, 16)
POOL_HIST = 15
D_FF = 2816
NORM_EPS = 1e-6
PROJ_WIDTH = 2048
PAST_LEN = 2048

LANES = 128
SUBLANES = 8
FF_CHUNK = 256
N_FF_CHUNKS = D_FF // FF_CHUNK
POOL_PAD = 24
ATTN_BLOCK = 2 * CHUNK
VMEM_LIMIT = 56 * 1024 * 1024

_Q0, _K0, _V0 = 0, 256, 384
_CB0, _CC0, _CH0 = 512, 768, 1024
_GU0, _GV0 = 1280, 1536
_P0 = 1792


def _rmsnorm(x, g):
    ms = jnp.mean(x * x, axis=-1, keepdims=True)
    return x * lax.rsqrt(ms + NORM_EPS) * g


def _dot(a, b):
    return jnp.dot(a, b, preferred_element_type=jnp.float32)


def _dot_nt(a, b):
    return lax.dot_general(a, b, (((1,), (1,)), ((), ())), preferred_element_type=jnp.float32)


def _rope(x, cos_t, sin_lo, sin_hi):
    up = pltpu.roll(x, LANES - ROT_DIM // 2, axis=1)
    dn = pltpu.roll(x, ROT_DIM // 2, axis=1)
    return x * cos_t + up * sin_lo + dn * sin_hi


def _mixer_kernel(*refs, tm, nt, has_past, pos0):
    it = iter(refs)
    x_ref = next(it)
    cos_ref, sinlo_ref, sinhi_ref = next(it), next(it), next(it)
    if has_past:
        kpast_ref, vpast_ref, cpast_ref, ppast_ref = next(it), next(it), next(it), next(it)
    else:
        bias_ref = next(it)
    gmix_ref, win_ref, sink_ref, convw_ref = next(it), next(it), next(it), next(it)
    lng_ref, lnb_ref, gw_ref, gb_ref = next(it), next(it), next(it), next(it)
    pw_ref, pscale_ref, wout_ref = next(it), next(it), next(it)
    xo_ref, kst_ref, vst_ref, cst_ref, pst_ref = next(it), next(it), next(it), next(it), next(it)
    if has_past:
        grows_ref = next(it)
    kext, vext, mext, pext, pa, pb, mix = (next(it), next(it), next(it), next(it), next(it), next(it), next(it))

    t = pl.program_id(1)
    qn = min(ATTN_BLOCK, tm)
    nblk = tm // qn
    nkeys = WINDOW + qn
    kr = min(WINDOW, tm)
    gl = min(GMLP_CHUNK, tm)

    @pl.when(t == 0)
    def _init():
        mext[0:SUBLANES, :] = jnp.zeros((SUBLANES, W_CONV), jnp.float32)
        pext[0:POOL_PAD, :] = jnp.zeros((POOL_PAD, W_POOL), jnp.float32)
        pa[0:SUBLANES, :] = jnp.zeros((SUBLANES, W_POOL), jnp.float32)
        pb[0:SUBLANES, :] = jnp.zeros((SUBLANES, W_POOL), jnp.float32)
        if has_past:
            kext[0:WINDOW, :] = kpast_ref[...].astype(jnp.bfloat16)
            vext[0:WINDOW, :] = vpast_ref[...].astype(jnp.bfloat16)
            mext[SUBLANES - (CONV_WIDTH - 1):SUBLANES, :] = cpast_ref[...]
            pext[POOL_PAD - POOL_HIST:POOL_PAD, :] = ppast_ref[...]
        else:
            kext[0:WINDOW, :] = jnp.zeros((WINDOW, KV_WIDTH), jnp.bfloat16)
            vext[0:WINDOW, :] = jnp.zeros((WINDOW, KV_WIDTH), jnp.bfloat16)

    x = x_ref[...]
    h = _rmsnorm(x, gmix_ref[...]).astype(jnp.bfloat16)

    zq = _dot(h, win_ref[:, _Q0:_CB0])
    cos_t, sin_lo, sin_hi = cos_ref[...], sinlo_ref[...], sinhi_ref[...]
    scale = HEAD_DIM ** -0.5
    qa = _rope(zq[:, 0:128], cos_t, sin_lo, sin_hi) * scale
    qb = _rope(zq[:, 128:256], cos_t, sin_lo, sin_hi) * scale
    k = _rope(zq[:, 256:384], cos_t, sin_lo, sin_hi)
    v = zq[:, 384:512]
    kext[WINDOW:WINDOW + tm, :] = k.astype(jnp.bfloat16)
    vext[WINDOW:WINDOW + tm, :] = v.astype(jnp.bfloat16)
    kst_ref[...] = k[tm - kr:tm, :]
    vst_ref[...] = v[tm - kr:tm, :]

    lane = lax.broadcasted_iota(jnp.int32, (1, LANES), 1)
    lo = lane < HEAD_DIM
    row4 = lax.broadcasted_iota(jnp.int32, (4 * qn, 1), 0)
    sk = jnp.where(row4 < qn, sink_ref[0],
                   jnp.where(row4 < 2 * qn, sink_ref[1], jnp.where(row4 < 3 * qn, sink_ref[2], sink_ref[3])))
    scores = []
    for b in range(nblk):
        r0 = b * qn
        qa_b, qb_b = qa[r0:r0 + qn, :], qb[r0:r0 + qn, :]
        qs = jnp.concatenate([jnp.where(lo, qa_b, 0.0), jnp.where(lo, qb_b, 0.0),
                              jnp.where(lo, 0.0, qa_b), jnp.where(lo, 0.0, qb_b)], axis=0)
        s = _dot_nt(qs.astype(jnp.bfloat16), kext[r0:r0 + nkeys, :])
        if not has_past:
            s = s + (bias_ref[0] if b > 0 else bias_ref[jnp.where(t == 0, 1, 0)])
        scores.append(s)
    probs, dens = [], []
    for s in scores:
        m = jnp.maximum(jnp.max(s, axis=-1, keepdims=True), sk)
        p = jnp.exp(s - m)
        dens.append(jnp.sum(p, axis=-1, keepdims=True) + jnp.exp(sk - m))
        probs.append(p.astype(jnp.bfloat16))
    for b in range(nblk):
        r0 = b * qn
        o = _dot(probs[b], vext[r0:r0 + nkeys, :]) / dens[b]
        oa = jnp.where(lo, o[0:qn], o[2 * qn:3 * qn])
        ob = jnp.where(lo, o[qn:2 * qn], o[3 * qn:4 * qn])
        mix[r0:r0 + qn, 0:128] = oa.astype(jnp.bfloat16)
        mix[r0:r0 + qn, 128:256] = ob.astype(jnp.bfloat16)
    if nt > 1:
        kext[0:WINDOW, :] = kext[tm:tm + WINDOW, :]
        vext[0:WINDOW, :] = vext[tm:tm + WINDOW, :]

    zc = _dot(h, win_ref[:, _CB0:_GU0])
    mprod = zc[:, 256:512] * zc[:, 512:768]
    mext[SUBLANES:SUBLANES + tm, :] = mprod
    cw = convw_ref[...]
    yc = (mext[SUBLANES - 2:SUBLANES - 2 + tm, :] * cw[0:1, :]
          + mext[SUBLANES - 1:SUBLANES - 1 + tm, :] * cw[1:2, :]
          + mprod * cw[2:3, :])
    mix[:, 256:512] = (zc[:, 0:256] * yc).astype(jnp.bfloat16)
    cst_ref[...] = mext[SUBLANES + tm - 2:SUBLANES + tm, :]
    if nt > 1:
        mext[0:SUBLANES, :] = mext[tm:tm + SUBLANES, :]

    zg = _dot(h, win_ref[:, _GU0:_P0])
    u = jax.nn.gelu(zg[:, 0:256])
    gv = jax.nn.gelu(zg[:, 256:512])
    mu = jnp.mean(gv, axis=-1, keepdims=True)
    var = jnp.mean(jnp.square(gv - mu), axis=-1, keepdims=True)
    vn = (gv - mu) * lax.rsqrt(var + NORM_EPS) * lng_ref[...] + lnb_ref[...]
    if has_past:
        grows_ref[...] = vn
    vnb = vn.astype(jnp.bfloat16)
    gi = lax.broadcasted_iota(jnp.int32, (gl, gl), 0)
    gj = lax.broadcasted_iota(jnp.int32, (gl, gl), 1)
    tri = (gj // CHUNK) <= (gi // CHUNK)
    lane256 = lax.broadcasted_iota(jnp.int32, (1, W_GMLP), 1) // GMLP_GROUP_DIM
    wstack = jnp.concatenate([jnp.where(tri, gw_ref[g, 0:gl, 0:gl], 0.0).astype(jnp.bfloat16)
                              for g in range(GMLP_GROUPS)], axis=0)
    bias_g = gb_ref[0:gl, :]
    for c in range(tm // gl):
        r0 = c * gl
        full = _dot(wstack, vnb[r0:r0 + gl, :])
        s = full[0:gl]
        for g in range(1, GMLP_GROUPS):
            s = jnp.where(lane256 >= g, full[g * gl:(g + 1) * gl], s)
        mix[r0:r0 + gl, 512:768] = (u[r0:r0 + gl, :] * (s + bias_g)).astype(jnp.bfloat16)

    zp = _dot(h, win_ref[:, _P0:PROJ_WIDTH])
    pext[POOL_PAD:POOL_PAD + tm, :] = zp
    n = POOL_PAD - SUBLANES + tm
    pa[SUBLANES:SUBLANES + n, :] = pext[SUBLANES:SUBLANES + n, :] + pext[SUBLANES - 1:SUBLANES - 1 + n, :]
    pb[SUBLANES:SUBLANES + n, :] = pa[SUBLANES:SUBLANES + n, :] + pa[SUBLANES - 2:SUBLANES - 2 + n, :]
    s2 = pa[POOL_PAD:POOL_PAD + tm, 0:LANES]
    s4 = pb[POOL_PAD:POOL_PAD + tm, 0:LANES]
    pa[SUBLANES:SUBLANES + n, LANES:] = pb[SUBLANES:SUBLANES + n, LANES:] + pb[SUBLANES - 4:SUBLANES - 4 + n, LANES:]
    s8 = pa[POOL_PAD:POOL_PAD + tm, LANES:]
    s16 = s8 + pa[POOL_PAD - 8:POOL_PAD - 8 + tm, LANES:]
    wsum = jnp.concatenate([jnp.where(lo, s2, s4), jnp.where(lo, s8, s16)], axis=1)
    prow = lax.broadcasted_iota(jnp.int32, (tm, 1), 0) + (pos0 + 1) + t * tm
    winl = jnp.left_shift(2, lane256).astype(jnp.float32)
    cnt = jnp.minimum(prow.astype(jnp.float32), winl)
    pooled = wsum / cnt - zp
    mix[:, 768:1024] = (_dot(pooled.astype(jnp.bfloat16), pw_ref[...]) * pscale_ref[...]).astype(jnp.bfloat16)
    pst_ref[...] = pext[POOL_PAD + tm - POOL_HIST:POOL_PAD + tm, :]
    if nt > 1:
        pext[0:POOL_PAD, :] = pext[tm:tm + POOL_PAD, :]

    xo_ref[...] = x + _dot(mix[...], wout_ref[...])


def _const_spec(shape):
    nd = len(shape)
    return pl.BlockSpec(shape, lambda b, t, _nd=nd: (0,) * _nd, pipeline_mode=pl.Buffered(1))


def _window_bias(qn):
    rows = (np.arange(4 * qn) % qn) // CHUNK
    keys = np.arange(WINDOW + qn)
    ok = (keys[None, :] >= rows[:, None] * CHUNK) & (keys[None, :] < (rows[:, None] + 3) * CHUNK)
    first = ok & (keys[None, :] >= WINDOW)
    return np.where(np.stack([ok, first]), 0.0, -np.inf).astype(np.float32)


def _mixer_call(x, tabs, past, lw, *, tm, pos0):
    bsz, t_len, _ = x.shape
    nt = t_len // tm
    has_past = past is not None
    kr = min(WINDOW, tm)
    qn = min(ATTN_BLOCK, tm)
    in_specs = [pl.BlockSpec((None, tm, D_MODEL), lambda b, t: (b, t, 0))]
    in_specs += [pl.BlockSpec((tm, LANES), lambda b, t: (t, 0))] * 3
    args = [x, *tabs]
    if has_past:
        kp, vp, cp, pp = past
        in_specs += [pl.BlockSpec((None, WINDOW, KV_WIDTH), lambda b, t: (b, 0, 0)),
                     pl.BlockSpec((None, WINDOW, KV_WIDTH), lambda b, t: (b, 0, 0)),
                     pl.BlockSpec((None, CONV_WIDTH - 1, W_CONV), lambda b, t: (b, 0, 0)),
                     pl.BlockSpec((None, POOL_HIST, W_POOL), lambda b, t: (b, 0, 0))]
        args += [kp, vp, cp, pp]
    else:
        bias = jnp.asarray(_window_bias(qn))
        in_specs.append(_const_spec(bias.shape))
        args.append(bias)
    wnames = ('g_mix', 'w_in', 'sink', 'conv_w', 'ln_g', 'ln_b', 'gmlp_w', 'gmlp_b', 'pool_w', 'pool_scale', 'w_out')
    for n in wnames:
        a = lw[n]
        if n == 'sink':
            in_specs.append(pl.BlockSpec(memory_space=pltpu.SMEM))
        else:
            in_specs.append(_const_spec(a.shape))
        args.append(a)
    out_shape = [jax.ShapeDtypeStruct((bsz, t_len, D_MODEL), jnp.float32),
                 jax.ShapeDtypeStruct((bsz, kr, KV_WIDTH), jnp.float32),
                 jax.ShapeDtypeStruct((bsz, kr, KV_WIDTH), jnp.float32),
                 jax.ShapeDtypeStruct((bsz, CONV_WIDTH - 1, W_CONV), jnp.float32),
                 jax.ShapeDtypeStruct((bsz, POOL_HIST, W_POOL), jnp.float32)]
    out_specs = [pl.BlockSpec((None, tm, D_MODEL), lambda b, t: (b, t, 0)),
                 pl.BlockSpec((None, kr, KV_WIDTH), lambda b, t: (b, 0, 0)),
                 pl.BlockSpec((None, kr, KV_WIDTH), lambda b, t: (b, 0, 0)),
                 pl.BlockSpec((None, CONV_WIDTH - 1, W_CONV), lambda b, t: (b, 0, 0)),
                 pl.BlockSpec((None, POOL_HIST, W_POOL), lambda b, t: (b, 0, 0))]
    if has_past:
        out_shape.append(jax.ShapeDtypeStruct((bsz, t_len, W_GMLP), jnp.float32))
        out_specs.append(pl.BlockSpec((None, tm, W_GMLP), lambda b, t: (b, t, 0)))
    scratch = [pltpu.VMEM((WINDOW + tm, KV_WIDTH), jnp.bfloat16),
               pltpu.VMEM((WINDOW + tm, KV_WIDTH), jnp.bfloat16),
               pltpu.VMEM((SUBLANES + tm, W_CONV), jnp.float32),
               pltpu.VMEM((POOL_PAD + tm, W_POOL), jnp.float32),
               pltpu.VMEM((POOL_PAD + tm, W_POOL), jnp.float32),
               pltpu.VMEM((POOL_PAD + tm, W_POOL), jnp.float32),
               pltpu.VMEM((tm, D_MODEL), jnp.bfloat16)]
    return pl.pallas_call(
        functools.partial(_mixer_kernel, tm=tm, nt=nt, has_past=has_past, pos0=pos0),
        grid=(bsz, nt),
        in_specs=in_specs,
        out_specs=out_specs,
        out_shape=out_shape,
        scratch_shapes=scratch,
        compiler_params=pltpu.CompilerParams(
            dimension_semantics=("arbitrary", "arbitrary"), vmem_limit_bytes=VMEM_LIMIT),
        name="mixer_past" if has_past else "mixer",
    )(*args)


def _ffn_kernel(*refs, tm, nt, chained, final):
    it = iter(refs)
    x_ref = next(it)
    if not chained:
        past_ref = next(it)
    gffn_ref, wup_ref, fcw_ref, wdown_ref = next(it), next(it), next(it), next(it)
    if final:
        gfin_ref = next(it)
    xo_ref, fst_ref = next(it), next(it)
    carry, xacc, hbuf, act = next(it), next(it), next(it), next(it)

    t = pl.program_id(1)
    fc = FF_CHUNK
    halves = (0, D_FF)
    seg = tm // SUBLANES

    if chained:
        @pl.when(t == 0)
        def _init():
            carry[...] = jnp.zeros(carry.shape, jnp.float32)

    xacc[...] = jnp.swapaxes(x_ref[...].reshape(SUBLANES, seg, D_MODEL), 0, 1).reshape(tm, D_MODEL)
    hbuf[...] = _rmsnorm(xacc[...], gffn_ref[...]).astype(jnp.bfloat16)
    sub = lax.broadcasted_iota(jnp.int32, (SUBLANES, fc), 0)
    for c in range(N_FF_CHUNKS):
        ys = []
        for hi, off in enumerate(halves):
            cols = slice(off + c * fc, off + (c + 1) * fc)
            lanes = slice(hi * fc, (hi + 1) * fc)
            up = _dot(hbuf[...], wup_ref[:, cols])
            last1, last2 = up[tm - SUBLANES:tm], up[tm - 2 * SUBLANES:tm - SUBLANES]
            if chained:
                p1 = jnp.where(sub == 0, pltpu.roll(carry[c, 1, :, lanes], 1, axis=0), pltpu.roll(last1, 1, axis=0))
                p2 = jnp.where(sub == 0, pltpu.roll(carry[c, 0, :, lanes], 1, axis=0), pltpu.roll(last2, 1, axis=0))
                fst_ref[0:1, cols] = last2[SUBLANES - 1:SUBLANES]
                fst_ref[1:2, cols] = last1[SUBLANES - 1:SUBLANES]
                if nt > 1:
                    carry[c, 0, :, lanes] = last2
                    carry[c, 1, :, lanes] = last1
            else:
                p2, p1 = past_ref[0, :, cols], past_ref[1, :, cols]
                fst_ref[0, :, cols] = last2
                fst_ref[1, :, cols] = last1
            d1 = jnp.concatenate([p1, up[0:tm - SUBLANES]], axis=0)
            d2 = jnp.concatenate([p2, p1, up[0:tm - 2 * SUBLANES]], axis=0)
            cw = fcw_ref[:, cols]
            ys.append(d2 * cw[0:1, :] + d1 * cw[1:2, :] + up * cw[2:3, :])
        gate, val = ys
        act[:, c * fc:(c + 1) * fc] = (gate * jax.nn.sigmoid(gate) * val).astype(jnp.bfloat16)
    out = xacc[...] + _dot(act[...], wdown_ref[...])
    if final:
        out = _rmsnorm(out, gfin_ref[...])
    xo_ref[...] = jnp.swapaxes(out.reshape(seg, SUBLANES, D_MODEL), 0, 1).reshape(tm, D_MODEL)


def _ffn_call(x, fpast, lw, g_final, *, tm):
    bsz, t_len, _ = x.shape
    chained = fpast is None
    final = g_final is not None
    args = []
    if chained:
        nt = t_len // tm
        grid = (bsz, nt)
        st_shape = (bsz, CONV_WIDTH - 1, 2 * D_FF)
        st_spec = pl.BlockSpec((None, CONV_WIDTH - 1, 2 * D_FF), lambda b, t: (b, 0, 0))
        in_specs = []
    else:
        assert bsz == SUBLANES and t_len % SUBLANES == 0
        x = x.reshape(1, bsz * t_len, D_MODEL)
        tm, nt, grid = bsz * t_len, 1, (1, 1)
        st_shape = (CONV_WIDTH - 1, bsz, 2 * D_FF)
        st_spec = pl.BlockSpec(st_shape, lambda b, t: (0, 0, 0))
        in_specs = [pl.BlockSpec(st_shape, lambda b, t: (0, 0, 0))]
        args.append(jnp.swapaxes(fpast, 0, 1))
    assert tm % (SUBLANES * SUBLANES) == 0
    in_specs = [pl.BlockSpec((None, tm, D_MODEL), lambda b, t: (b, t, 0))] + in_specs
    args = [x] + args
    for n in ('g_ffn', 'w_up', 'ffn_conv_w', 'w_down'):
        in_specs.append(_const_spec(lw[n].shape))
        args.append(lw[n])
    if final:
        in_specs.append(_const_spec(g_final.shape))
        args.append(g_final)
    out_shape = [jax.ShapeDtypeStruct(x.shape, jnp.float32), jax.ShapeDtypeStruct(st_shape, jnp.float32)]
    out_specs = [pl.BlockSpec((None, tm, D_MODEL), lambda b, t: (b, t, 0)), st_spec]
    scratch = [pltpu.VMEM((N_FF_CHUNKS, CONV_WIDTH - 1, SUBLANES, 2 * FF_CHUNK), jnp.float32),
               pltpu.VMEM((tm, D_MODEL), jnp.float32),
               pltpu.VMEM((tm, D_MODEL), jnp.bfloat16),
               pltpu.VMEM((tm, D_FF), jnp.bfloat16)]
    y, fst = pl.pallas_call(
        functools.partial(_ffn_kernel, tm=tm, nt=nt, chained=chained, final=final),
        grid=grid,
        in_specs=in_specs,
        out_specs=out_specs,
        out_shape=out_shape,
        scratch_shapes=scratch,
        compiler_params=pltpu.CompilerParams(
            dimension_semantics=("arbitrary", "arbitrary"), vmem_limit_bytes=VMEM_LIMIT),
        name=("ffn" if chained else "ffn_past") + ("_final" if final else ""),
    )(*args)
    if not chained:
        y, fst = y.reshape(bsz, t_len, D_MODEL), jnp.swapaxes(fst, 0, 1)
    return y, fst


def _swap_mid_heads(a, axis):
    parts = jnp.split(a, N_Q_HEADS, axis=axis)
    return jnp.concatenate([parts[0], parts[2], parts[1], parts[3]], axis=axis)


def _rope_tables(pos0, t_len):
    half = ROT_DIM // 2
    inv_freq = jnp.power(jnp.float32(ROPE_THETA), -jnp.arange(half, dtype=jnp.float32) * (2.0 / ROT_DIM))
    ang = (pos0 + jnp.arange(t_len)).astype(jnp.float32)[:, None] * inv_freq[None, :]
    cos, sin = jnp.cos(ang), jnp.sin(ang)
    ones = jnp.ones((t_len, HEAD_DIM - ROT_DIM), jnp.float32)
    zeros8 = jnp.zeros((t_len, half), jnp.float32)
    zeros = jnp.zeros((t_len, HEAD_DIM - ROT_DIM), jnp.float32)
    cos_h = jnp.concatenate([cos, cos, ones], axis=1)
    sinlo_h = jnp.concatenate([-sin, zeros8, zeros], axis=1)
    sinhi_h = jnp.concatenate([zeros8, sin, zeros], axis=1)
    return tuple(jnp.concatenate([a, a], axis=1) for a in (cos_h, sinlo_h, sinhi_h))


def _layer_weights(l, g_mix, w_in, attn_sink, conv_w, gmlp_ln_g, gmlp_ln_b, gmlp_w, gmlp_b, pool_w, pool_scale,
                   w_out, g_ffn, w_up, ffn_conv_w, w_down):
    win = w_in[l]
    win = jnp.concatenate([_swap_mid_heads(win[:, :W_ATTN], 1), win[:, W_ATTN:]], axis=1).astype(jnp.bfloat16)
    wout = w_out[l]
    wout = jnp.concatenate([_swap_mid_heads(wout[:W_ATTN], 0), wout[W_ATTN:]], axis=0).astype(jnp.bfloat16)
    pw = jnp.zeros((W_POOL, W_POOL), jnp.float32)
    for g in range(len(POOL_WINDOWS)):
        sl = slice(g * 64, (g + 1) * 64)
        pw = pw.at[sl, sl].set(pool_w[l, g])
    return {
        'g_mix': g_mix[l][None, :],
        'w_in': win,
        'sink': attn_sink[l],
        'conv_w': conv_w[l],
        'ln_g': gmlp_ln_g[l][None, :],
        'ln_b': gmlp_ln_b[l][None, :],
        'gmlp_w': gmlp_w[l],
        'gmlp_b': jnp.repeat(gmlp_b[l].T, GMLP_GROUP_DIM, axis=1),
        'pool_w': pw.astype(jnp.bfloat16),
        'pool_scale': pool_scale[l][None, :],
        'w_out': wout,
        'g_ffn': g_ffn[l][None, :],
        'w_up': w_up[l].astype(jnp.bfloat16),
        'ffn_conv_w': ffn_conv_w[l],
        'w_down': w_down[l].astype(jnp.bfloat16),
    }


def _run(x, pos0, pasts, lws, g_final, tm):
    bsz, t_len, _ = x.shape
    tabs = _rope_tables(pos0, t_len)
    states = []
    for l in range(DEPTH):
        lw = lws[l]
        if pasts is None:
            mpast, fpast = None, None
        else:
            ck, cv, sc, sp, sf = pasts
            mpast = (ck[:, l].reshape(bsz, WINDOW, KV_WIDTH), cv[:, l].reshape(bsz, WINDOW, KV_WIDTH),
                     sc[:, l], sp[:, l])
            fpast = sf[:, l]
        mo = _mixer_call(x, tabs, mpast, lw, tm=tm, pos0=pos0)
        x_mid, kst, vst, cst, pst = mo[:5]
        gfin = g_final[None, :] if l == DEPTH - 1 else None
        x, fst = _ffn_call(x_mid, fpast, lw, gfin, tm=tm)
        st = [kst.reshape(bsz, -1, N_KV_HEADS, HEAD_DIM), vst.reshape(bsz, -1, N_KV_HEADS, HEAD_DIM),
              cst, pst, fst]
        if pasts is not None:
            st.append(mo[5])
        states.append(st)
    stacked = [jnp.stack([st[i] for st in states], axis=1) for i in range(len(states[0]))]
    return x, stacked


def kernel(x_prompt, x_sample, cache_attn_k, cache_attn_v, state_conv, state_pool, state_ffn_conv, g_mix, w_in,
           attn_sink, conv_w, gmlp_ln_g, gmlp_ln_b, gmlp_w, gmlp_b, pool_w, pool_scale, w_out, g_ffn, w_up,
           ffn_conv_w, w_down, g_final):
    lws = [_layer_weights(l, g_mix, w_in, attn_sink, conv_w, gmlp_ln_g, gmlp_ln_b, gmlp_w, gmlp_b, pool_w,
                          pool_scale, w_out, g_ffn, w_up, ffn_conv_w, w_down) for l in range(DEPTH)]
    y_prompt, sp = _run(x_prompt, 0, None, lws, g_final, tm=512)
    y_sample, ss = _run(x_sample, PAST_LEN, (cache_attn_k, cache_attn_v, state_conv, state_pool, state_ffn_conv),
                        lws, g_final, tm=x_sample.shape[1])
    return (y_prompt, y_sample, sp[0], sp[1], sp[2], sp[3], sp[4], ss[0], ss[1], ss[2], ss[3], ss[4], ss[5])
```

```python
import functools

import numpy as np
import jax
import jax.numpy as jnp
from jax import lax
from jax.experimental import pallas as pl
from jax.experimental.pallas import tpu as pltpu

D_MODEL = 1024
DEPTH = 2
CHUNK = 64
HEAD_DIM = 64
N_Q_HEADS = 4
N_KV_HEADS = 2
KV_WIDTH = N_KV_HEADS * HEAD_DIM
W_ATTN = 256
W_CONV = 256
W_GMLP = 256
W_POOL = 256
WINDOW = 128
ROPE_THETA = 500000.0
ROT_DIM = 16
CONV_WIDTH = 3
GMLP_CHUNK = 128
GMLP_GROUPS = 4
GMLP_GROUP_DIM = 64
POOL_WINDOWS = (2, 4, 8, 16)
POOL_HIST = 15
D_FF = 2816
NORM_EPS = 1e-6
PROJ_WIDTH = 2048
PAST_LEN = 2048

LANES = 128
SUBLANES = 8
FF_CHUNK = 256
N_FF_CHUNKS = D_FF // FF_CHUNK
POOL_PAD = 24
ATTN_BLOCK = 2 * CHUNK
MIXER_SUB = 256
MIXER_STAGGER = 0
VMEM_LIMIT = 56 * 1024 * 1024

_Q0, _K0, _V0 = 0, 256, 384
_CB0, _CC0, _CH0 = 512, 768, 1024
_GU0, _GV0 = 1280, 1536
_P0 = 1792


def _rmsnorm(x, g):
    ms = jnp.mean(x * x, axis=-1, keepdims=True)
    return x * lax.rsqrt(ms + NORM_EPS) * g


def _dot(a, b):
    return jnp.dot(a, b, preferred_element_type=jnp.float32)


def _dot_nt(a, b):
    return lax.dot_general(a, b, (((1,), (1,)), ((), ())), preferred_element_type=jnp.float32)


def _rope(x, cos_t, sin_lo, sin_hi):
    up = pltpu.roll(x, LANES - ROT_DIM // 2, axis=1)
    dn = pltpu.roll(x, ROT_DIM // 2, axis=1)
    return x * cos_t + up * sin_lo + dn * sin_hi


def _mixer_kernel(*refs, tm, nt, has_past, pos0):
    it = iter(refs)
    x_ref = next(it)
    cos_ref, sinlo_ref, sinhi_ref = next(it), next(it), next(it)
    if has_past:
        kpast_ref, vpast_ref, cpast_ref, ppast_ref = next(it), next(it), next(it), next(it)
    else:
        bias_ref = next(it)
    gmix_ref, win_ref, sink_ref, convw_ref = next(it), next(it), next(it), next(it)
    lng_ref, lnb_ref, gw_ref, gb_ref = next(it), next(it), next(it), next(it)
    pw_ref, pscale_ref, wout_ref = next(it), next(it), next(it)
    xo_ref, kst_ref, vst_ref, cst_ref, pst_ref = next(it), next(it), next(it), next(it), next(it)
    if has_past:
        grows_ref = next(it)
    kext, vext, mext, pext, pa, pb, hb, mix = (next(it), next(it), next(it), next(it), next(it), next(it),
                                               next(it), next(it))

    t = pl.program_id(1)
    qn = min(ATTN_BLOCK, tm)
    nkeys = WINDOW + qn
    kr = min(WINDOW, tm)
    gl = min(GMLP_CHUNK, tm)
    sub = min(MIXER_SUB, tm)

    @pl.when(t == 0)
    def _init():
        mext[0:SUBLANES, :] = jnp.zeros((SUBLANES, W_CONV), jnp.float32)
        pext[0:POOL_PAD, :] = jnp.zeros((POOL_PAD, W_POOL), jnp.float32)
        pa[0:SUBLANES, :] = jnp.zeros((SUBLANES, W_POOL), jnp.float32)
        pb[0:SUBLANES, :] = jnp.zeros((SUBLANES, W_POOL), jnp.float32)
        if has_past:
            kext[0:WINDOW, :] = kpast_ref[...].astype(jnp.bfloat16)
            vext[0:WINDOW, :] = vpast_ref[...].astype(jnp.bfloat16)
            mext[SUBLANES - (CONV_WIDTH - 1):SUBLANES, :] = cpast_ref[...]
            pext[POOL_PAD - POOL_HIST:POOL_PAD, :] = ppast_ref[...]
        else:
            kext[0:WINDOW, :] = jnp.zeros((WINDOW, KV_WIDTH), jnp.bfloat16)
            vext[0:WINDOW, :] = jnp.zeros((WINDOW, KV_WIDTH), jnp.bfloat16)

    lane = lax.broadcasted_iota(jnp.int32, (1, LANES), 1)
    lo = lane < HEAD_DIM
    lane256 = lax.broadcasted_iota(jnp.int32, (1, W_GMLP), 1) // GMLP_GROUP_DIM
    row4 = lax.broadcasted_iota(jnp.int32, (4 * qn, 1), 0)
    sk = jnp.where(row4 < qn, sink_ref[0],
                   jnp.where(row4 < 2 * qn, sink_ref[1], jnp.where(row4 < 3 * qn, sink_ref[2], sink_ref[3])))
    gi = lax.broadcasted_iota(jnp.int32, (gl, gl), 0)
    gj = lax.broadcasted_iota(jnp.int32, (gl, gl), 1)
    tri = (gj // CHUNK) <= (gi // CHUNK)
    wstack = jnp.concatenate([jnp.where(tri, gw_ref[g, 0:gl, 0:gl], 0.0).astype(jnp.bfloat16)
                              for g in range(GMLP_GROUPS)], axis=0)
    bias_g = gb_ref[0:gl, :]
    scale = HEAD_DIM ** -0.5

    def subtile(r0, last):
        rows = slice(r0, r0 + sub)
        x = x_ref[rows, :]
        hb[rows, :] = _rmsnorm(x, gmix_ref[...]).astype(jnp.bfloat16)
        zq = _dot(hb[rows, :], win_ref[:, _Q0:_CB0])
        cos_t, sin_lo, sin_hi = cos_ref[rows, :], sinlo_ref[rows, :], sinhi_ref[rows, :]
        qa = _rope(zq[:, 0:128], cos_t, sin_lo, sin_hi) * scale
        qb = _rope(zq[:, 128:256], cos_t, sin_lo, sin_hi) * scale
        k = _rope(zq[:, 256:384], cos_t, sin_lo, sin_hi)
        v = zq[:, 384:512]
        kext[WINDOW + r0:WINDOW + r0 + sub, :] = k.astype(jnp.bfloat16)
        vext[WINDOW + r0:WINDOW + r0 + sub, :] = v.astype(jnp.bfloat16)
        if last:
            kst_ref[...] = k[sub - kr:sub, :]
            vst_ref[...] = v[sub - kr:sub, :]
        yield
        scores = []
        for b in range(sub // qn):
            q0 = b * qn
            qa_b, qb_b = qa[q0:q0 + qn, :], qb[q0:q0 + qn, :]
            qs = jnp.concatenate([jnp.where(lo, qa_b, 0.0), jnp.where(lo, qb_b, 0.0),
                                  jnp.where(lo, 0.0, qa_b), jnp.where(lo, 0.0, qb_b)], axis=0)
            s = _dot_nt(qs.astype(jnp.bfloat16), kext[r0 + q0:r0 + q0 + nkeys, :])
            if not has_past:
                s = s + (bias_ref[0] if r0 + q0 > 0 else bias_ref[jnp.where(t == 0, 1, 0)])
            scores.append(s)
        yield
        zc = _dot(hb[rows, :], win_ref[:, _CB0:_GU0])
        zg = _dot(hb[rows, :], win_ref[:, _GU0:_P0])
        zp = _dot(hb[rows, :], win_ref[:, _P0:PROJ_WIDTH])
        yield
        probs, dens = [], []
        for s in scores:
            m = jnp.maximum(jnp.max(s, axis=-1, keepdims=True), sk)
            p = jnp.exp(s - m)
            dens.append(jnp.sum(p, axis=-1, keepdims=True) + jnp.exp(sk - m))
            probs.append(p.astype(jnp.bfloat16))
        yield
        for b in range(sub // qn):
            q0 = r0 + b * qn
            o = _dot(probs[b], vext[q0:q0 + nkeys, :]) / dens[b]
            oa = jnp.where(lo, o[0:qn], o[2 * qn:3 * qn])
            ob = jnp.where(lo, o[qn:2 * qn], o[3 * qn:4 * qn])
            mix[q0:q0 + qn, 0:128] = oa.astype(jnp.bfloat16)
            mix[q0:q0 + qn, 128:256] = ob.astype(jnp.bfloat16)
        yield
        mprod = zc[:, 256:512] * zc[:, 512:768]
        mext[SUBLANES + r0:SUBLANES + r0 + sub, :] = mprod
        cw = convw_ref[...]
        yc = (mext[SUBLANES - 2 + r0:SUBLANES - 2 + r0 + sub, :] * cw[0:1, :]
              + mext[SUBLANES - 1 + r0:SUBLANES - 1 + r0 + sub, :] * cw[1:2, :]
              + mprod * cw[2:3, :])
        mix[rows, 256:512] = (zc[:, 0:256] * yc).astype(jnp.bfloat16)
        if last:
            cst_ref[...] = mext[SUBLANES + tm - 2:SUBLANES + tm, :]

        u = jax.nn.gelu(zg[:, 0:256])
        gv = jax.nn.gelu(zg[:, 256:512])
        mu = jnp.mean(gv, axis=-1, keepdims=True)
        var = jnp.mean(jnp.square(gv - mu), axis=-1, keepdims=True)
        vn = (gv - mu) * lax.rsqrt(var + NORM_EPS) * lng_ref[...] + lnb_ref[...]
        if has_past:
            grows_ref[rows, :] = vn
        vnb = vn.astype(jnp.bfloat16)
        for c in range(sub // gl):
            c0 = c * gl
            full = _dot(wstack, vnb[c0:c0 + gl, :])
            s = full[0:gl]
            for g in range(1, GMLP_GROUPS):
                s = jnp.where(lane256 >= g, full[g * gl:(g + 1) * gl], s)
            mix[r0 + c0:r0 + c0 + gl, 512:768] = (u[c0:c0 + gl, :] * (s + bias_g)).astype(jnp.bfloat16)

        pext[POOL_PAD + r0:POOL_PAD + r0 + sub, :] = zp
        n = POOL_PAD - SUBLANES + sub
        lo_r = SUBLANES + r0
        pa[lo_r:lo_r + n, :] = pext[lo_r:lo_r + n, :] + pext[lo_r - 1:lo_r - 1 + n, :]
        pb[lo_r:lo_r + n, :] = pa[lo_r:lo_r + n, :] + pa[lo_r - 2:lo_r - 2 + n, :]
        tr = slice(POOL_PAD + r0, POOL_PAD + r0 + sub)
        s2 = pa[tr, 0:LANES]
        s4 = pb[tr, 0:LANES]
        pa[lo_r:lo_r + n, LANES:] = pb[lo_r:lo_r + n, LANES:] + pb[lo_r - 4:lo_r - 4 + n, LANES:]
        s8 = pa[tr, LANES:]
        s16 = s8 + pa[POOL_PAD - 8 + r0:POOL_PAD - 8 + r0 + sub, LANES:]
        wsum = jnp.concatenate([jnp.where(lo, s2, s4), jnp.where(lo, s8, s16)], axis=1)
        prow = lax.broadcasted_iota(jnp.int32, (sub, 1), 0) + (pos0 + 1 + r0) + t * tm
        winl = jnp.left_shift(2, lane256).astype(jnp.float32)
        cnt = jnp.minimum(prow.astype(jnp.float32), winl)
        pooled = wsum / cnt - zp
        mix[rows, 768:1024] = (_dot(pooled.astype(jnp.bfloat16), pw_ref[...]) * pscale_ref[...]).astype(jnp.bfloat16)
        if last:
            pst_ref[...] = pext[POOL_PAD + tm - POOL_HIST:POOL_PAD + tm, :]
        yield
        xo_ref[rows, :] = x_ref[rows, :] + _dot(mix[rows, :], wout_ref[...])
        yield

    nsub = tm // sub
    gens = [subtile(i * sub, i == nsub - 1) for i in range(nsub)]
    n_stages = 7
    for step in range(n_stages + MIXER_STAGGER * (nsub - 1)):
        for i, g in enumerate(gens):
            if 0 <= step - MIXER_STAGGER * i < n_stages:
                next(g)

    if nt > 1:
        kext[0:WINDOW, :] = kext[tm:tm + WINDOW, :]
        vext[0:WINDOW, :] = vext[tm:tm + WINDOW, :]
        mext[0:SUBLANES, :] = mext[tm:tm + SUBLANES, :]
        pext[0:POOL_PAD, :] = pext[tm:tm + POOL_PAD, :]


def _const_spec(shape):
    nd = len(shape)
    return pl.BlockSpec(shape, lambda b, t, _nd=nd: (0,) * _nd, pipeline_mode=pl.Buffered(1))


def _window_bias(qn):
    rows = (np.arange(4 * qn) % qn) // CHUNK
    keys = np.arange(WINDOW + qn)
    ok = (keys[None, :] >= rows[:, None] * CHUNK) & (keys[None, :] < (rows[:, None] + 3) * CHUNK)
    first = ok & (keys[None, :] >= WINDOW)
    return np.where(np.stack([ok, first]), 0.0, -np.inf).astype(np.float32)


def _mixer_call(x, tabs, past, lw, *, tm, pos0):
    bsz, t_len, _ = x.shape
    nt = t_len // tm
    has_past = past is not None
    kr = min(WINDOW, tm)
    qn = min(ATTN_BLOCK, tm)
    in_specs = [pl.BlockSpec((None, tm, D_MODEL), lambda b, t: (b, t, 0))]
    in_specs += [pl.BlockSpec((tm, LANES), lambda b, t: (t, 0))] * 3
    args = [x, *tabs]
    if has_past:
        kp, vp, cp, pp = past
        in_specs += [pl.BlockSpec((None, WINDOW, KV_WIDTH), lambda b, t: (b, 0, 0)),
                     pl.BlockSpec((None, WINDOW, KV_WIDTH), lambda b, t: (b, 0, 0)),
                     pl.BlockSpec((None, CONV_WIDTH - 1, W_CONV), lambda b, t: (b, 0, 0)),
                     pl.BlockSpec((None, POOL_HIST, W_POOL), lambda b, t: (b, 0, 0))]
        args += [kp, vp, cp, pp]
    else:
        bias = jnp.asarray(_window_bias(qn))
        in_specs.append(_const_spec(bias.shape))
        args.append(bias)
    wnames = ('g_mix', 'w_in', 'sink', 'conv_w', 'ln_g', 'ln_b', 'gmlp_w', 'gmlp_b', 'pool_w', 'pool_scale', 'w_out')
    for n in wnames:
        a = lw[n]
        if n == 'sink':
            in_specs.append(pl.BlockSpec(memory_space=pltpu.SMEM))
        else:
            in_specs.append(_const_spec(a.shape))
        args.append(a)
    out_shape = [jax.ShapeDtypeStruct((bsz, t_len, D_MODEL), jnp.float32),
                 jax.ShapeDtypeStruct((bsz, kr, KV_WIDTH), jnp.float32),
                 jax.ShapeDtypeStruct((bsz, kr, KV_WIDTH), jnp.float32),
                 jax.ShapeDtypeStruct((bsz, CONV_WIDTH - 1, W_CONV), jnp.float32),
                 jax.ShapeDtypeStruct((bsz, POOL_HIST, W_POOL), jnp.float32)]
    out_specs = [pl.BlockSpec((None, tm, D_MODEL), lambda b, t: (b, t, 0)),
                 pl.BlockSpec((None, kr, KV_WIDTH), lambda b, t: (b, 0, 0)),
                 pl.BlockSpec((None, kr, KV_WIDTH), lambda b, t: (b, 0, 0)),
                 pl.BlockSpec((None, CONV_WIDTH - 1, W_CONV), lambda b, t: (b, 0, 0)),
                 pl.BlockSpec((None, POOL_HIST, W_POOL), lambda b, t: (b, 0, 0))]
    if has_past:
        out_shape.append(jax.ShapeDtypeStruct((bsz, t_len, W_GMLP), jnp.float32))
        out_specs.append(pl.BlockSpec((None, tm, W_GMLP), lambda b, t: (b, t, 0)))
    scratch = [pltpu.VMEM((WINDOW + tm, KV_WIDTH), jnp.bfloat16),
               pltpu.VMEM((WINDOW + tm, KV_WIDTH), jnp.bfloat16),
               pltpu.VMEM((SUBLANES + tm, W_CONV), jnp.float32),
               pltpu.VMEM((POOL_PAD + tm, W_POOL), jnp.float32),
               pltpu.VMEM((POOL_PAD + tm, W_POOL), jnp.float32),
               pltpu.VMEM((POOL_PAD + tm, W_POOL), jnp.float32),
               pltpu.VMEM((tm, D_MODEL), jnp.bfloat16),
               pltpu.VMEM((tm, D_MODEL), jnp.bfloat16)]
    return pl.pallas_call(
        functools.partial(_mixer_kernel, tm=tm, nt=nt, has_past=has_past, pos0=pos0),
        grid=(bsz, nt),
        in_specs=in_specs,
        out_specs=out_specs,
        out_shape=out_shape,
        scratch_shapes=scratch,
        compiler_params=pltpu.CompilerParams(
            dimension_semantics=("arbitrary", "arbitrary"), vmem_limit_bytes=VMEM_LIMIT),
        name="mixer_past" if has_past else "mixer",
    )(*args)


def _ffn_kernel(*refs, tm, nt, chained, final):
    it = iter(refs)
    x_ref = next(it)
    if not chained:
        past_ref = next(it)
    gffn_ref, wup_ref, fcw_ref, wdown_ref = next(it), next(it), next(it), next(it)
    if final:
        gfin_ref = next(it)
    xo_ref, fst_ref = next(it), next(it)
    carry, xacc, hbuf, act = next(it), next(it), next(it), next(it)

    t = pl.program_id(1)
    fc = FF_CHUNK
    halves = (0, D_FF)
    seg = tm // SUBLANES

    if chained:
        @pl.when(t == 0)
        def _init():
            carry[...] = jnp.zeros(carry.shape, jnp.float32)

    xacc[...] = jnp.swapaxes(x_ref[...].reshape(SUBLANES, seg, D_MODEL), 0, 1).reshape(tm, D_MODEL)
    hbuf[...] = _rmsnorm(xacc[...], gffn_ref[...]).astype(jnp.bfloat16)
    sub = lax.broadcasted_iota(jnp.int32, (SUBLANES, fc), 0)
    for c in range(N_FF_CHUNKS):
        ys = []
        for hi, off in enumerate(halves):
            cols = slice(off + c * fc, off + (c + 1) * fc)
            lanes = slice(hi * fc, (hi + 1) * fc)
            up = _dot(hbuf[...], wup_ref[:, cols])
            last1, last2 = up[tm - SUBLANES:tm], up[tm - 2 * SUBLANES:tm - SUBLANES]
            if chained:
                p1 = jnp.where(sub == 0, pltpu.roll(carry[c, 1, :, lanes], 1, axis=0), pltpu.roll(last1, 1, axis=0))
                p2 = jnp.where(sub == 0, pltpu.roll(carry[c, 0, :, lanes], 1, axis=0), pltpu.roll(last2, 1, axis=0))
                fst_ref[0:1, cols] = last2[SUBLANES - 1:SUBLANES]
                fst_ref[1:2, cols] = last1[SUBLANES - 1:SUBLANES]
                if nt > 1:
                    carry[c, 0, :, lanes] = last2
                    carry[c, 1, :, lanes] = last1
            else:
                p2, p1 = past_ref[0, :, cols], past_ref[1, :, cols]
                fst_ref[0, :, cols] = last2
                fst_ref[1, :, cols] = last1
            d1 = jnp.concatenate([p1, up[0:tm - SUBLANES]], axis=0)
            d2 = jnp.concatenate([p2, p1, up[0:tm - 2 * SUBLANES]], axis=0)
            cw = fcw_ref[:, cols]
            ys.append(d2 * cw[0:1, :] + d1 * cw[1:2, :] + up * cw[2:3, :])
        gate, val = ys
        act[:, c * fc:(c + 1) * fc] = (gate * jax.nn.sigmoid(gate) * val).astype(jnp.bfloat16)
    out = xacc[...] + _dot(act[...], wdown_ref[...])
    if final:
        out = _rmsnorm(out, gfin_ref[...])
    xo_ref[...] = jnp.swapaxes(out.reshape(seg, SUBLANES, D_MODEL), 0, 1).reshape(tm, D_MODEL)


def _ffn_call(x, fpast, lw, g_final, *, tm):
    bsz, t_len, _ = x.shape
    chained = fpast is None
    final = g_final is not None
    args = []
    if chained:
        nt = t_len // tm
        grid = (bsz, nt)
        st_shape = (bsz, CONV_WIDTH - 1, 2 * D_FF)
        st_spec = pl.BlockSpec((None, CONV_WIDTH - 1, 2 * D_FF), lambda b, t: (b, 0, 0))
        in_specs = []
    else:
        assert bsz == SUBLANES and t_len % SUBLANES == 0
        x = x.reshape(1, bsz * t_len, D_MODEL)
        tm, nt, grid = bsz * t_len, 1, (1, 1)
        st_shape = (CONV_WIDTH - 1, bsz, 2 * D_FF)
        st_spec = pl.BlockSpec(st_shape, lambda b, t: (0, 0, 0))
        in_specs = [pl.BlockSpec(st_shape, lambda b, t: (0, 0, 0))]
        args.append(jnp.swapaxes(fpast, 0, 1))
    assert tm % (SUBLANES * SUBLANES) == 0
    in_specs = [pl.BlockSpec((None, tm, D_MODEL), lambda b, t: (b, t, 0))] + in_specs
    args = [x] + args
    for n in ('g_ffn', 'w_up', 'ffn_conv_w', 'w_down'):
        in_specs.append(_const_spec(lw[n].shape))
        args.append(lw[n])
    if final:
        in_specs.append(_const_spec(g_final.shape))
        args.append(g_final)
    out_shape = [jax.ShapeDtypeStruct(x.shape, jnp.float32), jax.ShapeDtypeStruct(st_shape, jnp.float32)]
    out_specs = [pl.BlockSpec((None, tm, D_MODEL), lambda b, t: (b, t, 0)), st_spec]
    scratch = [pltpu.VMEM((N_FF_CHUNKS, CONV_WIDTH - 1, SUBLANES, 2 * FF_CHUNK), jnp.float32),
               pltpu.VMEM((tm, D_MODEL), jnp.float32),
               pltpu.VMEM((tm, D_MODEL), jnp.bfloat16),
               pltpu.VMEM((tm, D_FF), jnp.bfloat16)]
    y, fst = pl.pallas_call(
        functools.partial(_ffn_kernel, tm=tm, nt=nt, chained=chained, final=final),
        grid=grid,
        in_specs=in_specs,
        out_specs=out_specs,
        out_shape=out_shape,
        scratch_shapes=scratch,
        compiler_params=pltpu.CompilerParams(
            dimension_semantics=("arbitrary", "arbitrary"), vmem_limit_bytes=VMEM_LIMIT),
        name=("ffn" if chained else "ffn_past") + ("_final" if final else ""),
    )(*args)
    if not chained:
        y, fst = y.reshape(bsz, t_len, D_MODEL), jnp.swapaxes(fst, 0, 1)
    return y, fst


def _swap_mid_heads(a, axis):
    parts = jnp.split(a, N_Q_HEADS, axis=axis)
    return jnp.concatenate([parts[0], parts[2], parts[1], parts[3]], axis=axis)


def _rope_tables(pos0, t_len):
    half = ROT_DIM // 2
    inv_freq = np.power(np.float64(ROPE_THETA), -np.arange(half, dtype=np.float64) * (2.0 / ROT_DIM))
    ang = (pos0 + np.arange(t_len, dtype=np.float64))[:, None] * inv_freq[None, :]
    cos, sin = np.cos(ang), np.sin(ang)
    rest = HEAD_DIM - ROT_DIM
    cos_h = np.concatenate([cos, cos, np.ones((t_len, rest))], axis=1)
    sinlo_h = np.concatenate([-sin, np.zeros((t_len, half + rest))], axis=1)
    sinhi_h = np.concatenate([np.zeros((t_len, half)), sin, np.zeros((t_len, rest))], axis=1)
    return tuple(jnp.asarray(np.concatenate([a, a], axis=1), dtype=jnp.float32) for a in (cos_h, sinlo_h, sinhi_h))


def _layer_weights(l, g_mix, w_in, attn_sink, conv_w, gmlp_ln_g, gmlp_ln_b, gmlp_w, gmlp_b, pool_w, pool_scale,
                   w_out, g_ffn, w_up, ffn_conv_w, w_down):
    win = w_in[l]
    win = jnp.concatenate([_swap_mid_heads(win[:, :W_ATTN], 1), win[:, W_ATTN:]], axis=1).astype(jnp.bfloat16)
    wout = w_out[l]
    wout = jnp.concatenate([_swap_mid_heads(wout[:W_ATTN], 0), wout[W_ATTN:]], axis=0).astype(jnp.bfloat16)
    pw = jnp.zeros((W_POOL, W_POOL), jnp.float32)
    for g in range(len(POOL_WINDOWS)):
        sl = slice(g * 64, (g + 1) * 64)
        pw = pw.at[sl, sl].set(pool_w[l, g])
    return {
        'g_mix': g_mix[l][None, :],
        'w_in': win,
        'sink': attn_sink[l],
        'conv_w': conv_w[l],
        'ln_g': gmlp_ln_g[l][None, :],
        'ln_b': gmlp_ln_b[l][None, :],
        'gmlp_w': gmlp_w[l],
        'gmlp_b': jnp.repeat(gmlp_b[l].T, GMLP_GROUP_DIM, axis=1),
        'pool_w': pw.astype(jnp.bfloat16),
        'pool_scale': pool_scale[l][None, :],
        'w_out': wout,
        'g_ffn': g_ffn[l][None, :],
        'w_up': w_up[l].astype(jnp.bfloat16),
        'ffn_conv_w': ffn_conv_w[l],
        'w_down': w_down[l].astype(jnp.bfloat16),
    }


def _run(x, pos0, pasts, lws, g_final, tm):
    bsz, t_len, _ = x.shape
    tabs = _rope_tables(pos0, t_len)
    states = []
    for l in range(DEPTH):
        lw = lws[l]
        if pasts is None:
            mpast, fpast = None, None
        else:
            ck, cv, sc, sp, sf = pasts
            mpast = (ck[:, l].reshape(bsz, WINDOW, KV_WIDTH), cv[:, l].reshape(bsz, WINDOW, KV_WIDTH),
                     sc[:, l], sp[:, l])
            fpast = sf[:, l]
        mo = _mixer_call(x, tabs, mpast, lw, tm=tm, pos0=pos0)
        x_mid, kst, vst, cst, pst = mo[:5]
        gfin = g_final[None, :] if l == DEPTH - 1 else None
        x, fst = _ffn_call(x_mid, fpast, lw, gfin, tm=tm)
        st = [kst.reshape(bsz, -1, N_KV_HEADS, HEAD_DIM), vst.reshape(bsz, -1, N_KV_HEADS, HEAD_DIM),
              cst, pst, fst]
        if pasts is not None:
            st.append(mo[5])
        states.append(st)
    stacked = [jnp.stack([st[i] for st in states], axis=1) for i in range(len(states[0]))]
    return x, stacked


def kernel(x_prompt, x_sample, cache_attn_k, cache_attn_v, state_conv, state_pool, state_ffn_conv, g_mix, w_in,
           attn_sink, conv_w, gmlp_ln_g, gmlp_ln_b, gmlp_w, gmlp_b, pool_w, pool_scale, w_out, g_ffn, w_up,
           ffn_conv_w, w_down, g_final):
    lws = [_layer_weights(l, g_mix, w_in, attn_sink, conv_w, gmlp_ln_g, gmlp_ln_b, gmlp_w, gmlp_b, pool_w,
                          pool_scale, w_out, g_ffn, w_up, ffn_conv_w, w_down) for l in range(DEPTH)]
    y_prompt, sp = _run(x_prompt, 0, None, lws, g_final, tm=1024)
    y_sample, ss = _run(x_sample, PAST_LEN, (cache_attn_k, cache_attn_v, state_conv, state_pool, state_ffn_conv),
                        lws, g_final, tm=x_sample.shape[1])
    return (y_prompt, y_sample, sp[0], sp[1], sp[2], sp[3], sp[4], ss[0], ss[1], ss[2], ss[3], ss[4], ss[5])
```

```python
import functools

import numpy as np
import jax
import jax.numpy as jnp
from jax import lax
from jax.experimental import pallas as pl
from jax.experimental.pallas import tpu as pltpu

D_MODEL = 1024
DEPTH = 2
CHUNK = 64
HEAD_DIM = 64
N_Q_HEADS = 4
N_KV_HEADS = 2
KV_WIDTH = N_KV_HEADS * HEAD_DIM
W_ATTN = 256
W_CONV = 256
W_GMLP = 256
W_POOL = 256
WINDOW = 128
ROPE_THETA = 500000.0
ROT_DIM = 16
CONV_WIDTH = 3
GMLP_CHUNK = 128
GMLP_GROUPS = 4
GMLP_GROUP_DIM = 64
POOL_WINDOWS = (2, 4, 8, 16)
POOL_HIST = 15
D_FF = 2816
NORM_EPS = 1e-6
PROJ_WIDTH = 2048
PAST_LEN = 2048

LANES = 128
SUBLANES = 8
FF_CHUNK = 256
N_FF_CHUNKS = D_FF // FF_CHUNK
POOL_PAD = 24
ATTN_BLOCK = 2 * CHUNK
MIXER_SUB = 256
MIXER_STAGGER = 0
VMEM_LIMIT = 56 * 1024 * 1024

_Q0, _K0, _V0 = 0, 256, 384
_CB0, _CC0, _CH0 = 512, 768, 1024
_GU0, _GV0 = 1280, 1536
_P0 = 1792


def _rmsnorm(x, g):
    ms = jnp.mean(x * x, axis=-1, keepdims=True)
    return x * lax.rsqrt(ms + NORM_EPS) * g


def _dot(a, b):
    return jnp.dot(a, b, preferred_element_type=jnp.float32)


def _dot_nt(a, b):
    return lax.dot_general(a, b, (((1,), (1,)), ((), ())), preferred_element_type=jnp.float32)


def _rope(x, cos_t, sin_lo, sin_hi):
    up = pltpu.roll(x, LANES - ROT_DIM // 2, axis=1)
    dn = pltpu.roll(x, ROT_DIM // 2, axis=1)
    return x * cos_t + up * sin_lo + dn * sin_hi


def _mixer_kernel(*refs, tm, nt, has_past, pos0, layer):
    it = iter(refs)
    x_ref = next(it)
    cos_ref, sinlo_ref, sinhi_ref = next(it), next(it), next(it)
    if has_past:
        kpast_ref, vpast_ref, cpast_ref, ppast_ref = next(it), next(it), next(it), next(it)
    else:
        bias_ref = next(it)
    gmix_ref, win_ref, sink_ref, convw_ref = next(it), next(it), next(it), next(it)
    lng_ref, lnb_ref, gw_ref, gb_ref = next(it), next(it), next(it), next(it)
    pw_ref, pscale_ref, wout_ref = next(it), next(it), next(it)
    xo_ref, kst_ref, vst_ref, cst_ref, pst_ref = next(it), next(it), next(it), next(it), next(it)
    if has_past:
        grows_ref = next(it)
    kext, vext, mext, pext, pa, pb, hb, mix = (next(it), next(it), next(it), next(it), next(it), next(it),
                                               next(it), next(it))

    t = pl.program_id(1)
    qn = min(ATTN_BLOCK, tm)
    nkeys = WINDOW + qn
    kr = min(WINDOW, tm)
    gl = min(GMLP_CHUNK, tm)
    sub = min(MIXER_SUB, tm)

    @pl.when(t == 0)
    def _init():
        mext[0:SUBLANES, :] = jnp.zeros((SUBLANES, W_CONV), jnp.float32)
        pext[0:POOL_PAD, :] = jnp.zeros((POOL_PAD, W_POOL), jnp.float32)
        pa[0:SUBLANES, :] = jnp.zeros((SUBLANES, W_POOL), jnp.float32)
        pb[0:SUBLANES, :] = jnp.zeros((SUBLANES, W_POOL), jnp.float32)
        if has_past:
            kext[0:WINDOW, :] = kpast_ref[...].astype(jnp.bfloat16)
            vext[0:WINDOW, :] = vpast_ref[...].astype(jnp.bfloat16)
            mext[SUBLANES - (CONV_WIDTH - 1):SUBLANES, :] = cpast_ref[...]
            pext[POOL_PAD - POOL_HIST:POOL_PAD, :] = ppast_ref[...]
        else:
            kext[0:WINDOW, :] = jnp.zeros((WINDOW, KV_WIDTH), jnp.bfloat16)
            vext[0:WINDOW, :] = jnp.zeros((WINDOW, KV_WIDTH), jnp.bfloat16)

    lane = lax.broadcasted_iota(jnp.int32, (1, LANES), 1)
    lo = lane < HEAD_DIM
    lane256 = lax.broadcasted_iota(jnp.int32, (1, W_GMLP), 1) // GMLP_GROUP_DIM
    row4 = lax.broadcasted_iota(jnp.int32, (4 * qn, 1), 0)
    sk = jnp.where(row4 < qn, sink_ref[layer, 0],
                   jnp.where(row4 < 2 * qn, sink_ref[layer, 1],
                             jnp.where(row4 < 3 * qn, sink_ref[layer, 2], sink_ref[layer, 3])))
    gi = lax.broadcasted_iota(jnp.int32, (gl, gl), 0)
    gj = lax.broadcasted_iota(jnp.int32, (gl, gl), 1)
    tri = (gj // CHUNK) <= (gi // CHUNK)
    wstack = jnp.concatenate([jnp.where(tri, gw_ref[g, 0:gl, 0:gl], 0.0).astype(jnp.bfloat16)
                              for g in range(GMLP_GROUPS)], axis=0)
    bias_g = gb_ref[0:gl, :]
    scale = HEAD_DIM ** -0.5

    def subtile(r0, last):
        rows = slice(r0, r0 + sub)
        x = x_ref[rows, :]
        hb[rows, :] = _rmsnorm(x, gmix_ref[...]).astype(jnp.bfloat16)
        zq = _dot(hb[rows, :], win_ref[:, _Q0:_CB0])
        cos_t, sin_lo, sin_hi = cos_ref[rows, :], sinlo_ref[rows, :], sinhi_ref[rows, :]
        qa = _rope(zq[:, 0:128], cos_t, sin_lo, sin_hi) * scale
        qb = _rope(zq[:, 128:256], cos_t, sin_lo, sin_hi) * scale
        k = _rope(zq[:, 256:384], cos_t, sin_lo, sin_hi)
        v = zq[:, 384:512]
        kext[WINDOW + r0:WINDOW + r0 + sub, :] = k.astype(jnp.bfloat16)
        vext[WINDOW + r0:WINDOW + r0 + sub, :] = v.astype(jnp.bfloat16)
        if last:
            kst_ref[...] = k[sub - kr:sub, :]
            vst_ref[...] = v[sub - kr:sub, :]
        yield
        scores = []
        for b in range(sub // qn):
            q0 = b * qn
            qa_b, qb_b = qa[q0:q0 + qn, :], qb[q0:q0 + qn, :]
            qs = jnp.concatenate([jnp.where(lo, qa_b, 0.0), jnp.where(lo, qb_b, 0.0),
                                  jnp.where(lo, 0.0, qa_b), jnp.where(lo, 0.0, qb_b)], axis=0)
            s = _dot_nt(qs.astype(jnp.bfloat16), kext[r0 + q0:r0 + q0 + nkeys, :])
            if not has_past:
                s = s + (bias_ref[0] if r0 + q0 > 0 else bias_ref[jnp.where(t == 0, 1, 0)])
            scores.append(s)
        yield
        zc = _dot(hb[rows, :], win_ref[:, _CB0:_GU0])
        zg = _dot(hb[rows, :], win_ref[:, _GU0:_P0])
        zp = _dot(hb[rows, :], win_ref[:, _P0:PROJ_WIDTH])
        yield
        probs, dens = [], []
        for s in scores:
            m = jnp.maximum(jnp.max(s, axis=-1, keepdims=True), sk)
            p = jnp.exp(s - m)
            dens.append(jnp.sum(p, axis=-1, keepdims=True) + jnp.exp(sk - m))
            probs.append(p.astype(jnp.bfloat16))
        yield
        for b in range(sub // qn):
            q0 = r0 + b * qn
            o = _dot(probs[b], vext[q0:q0 + nkeys, :]) / dens[b]
            oa = jnp.where(lo, o[0:qn], o[2 * qn:3 * qn])
            ob = jnp.where(lo, o[qn:2 * qn], o[3 * qn:4 * qn])
            mix[q0:q0 + qn, 0:128] = oa.astype(jnp.bfloat16)
            mix[q0:q0 + qn, 128:256] = ob.astype(jnp.bfloat16)
        yield
        mprod = zc[:, 256:512] * zc[:, 512:768]
        mext[SUBLANES + r0:SUBLANES + r0 + sub, :] = mprod
        cw = convw_ref[...]
        yc = (mext[SUBLANES - 2 + r0:SUBLANES - 2 + r0 + sub, :] * cw[0:1, :]
              + mext[SUBLANES - 1 + r0:SUBLANES - 1 + r0 + sub, :] * cw[1:2, :]
              + mprod * cw[2:3, :])
        mix[rows, 256:512] = (zc[:, 0:256] * yc).astype(jnp.bfloat16)
        if last:
            cst_ref[...] = mext[SUBLANES + tm - 2:SUBLANES + tm, :]

        u = jax.nn.gelu(zg[:, 0:256])
        gv = jax.nn.gelu(zg[:, 256:512])
        mu = jnp.mean(gv, axis=-1, keepdims=True)
        var = jnp.mean(jnp.square(gv - mu), axis=-1, keepdims=True)
        vn = (gv - mu) * lax.rsqrt(var + NORM_EPS) * lng_ref[...] + lnb_ref[...]
        if has_past:
            grows_ref[rows, :] = vn
        vnb = vn.astype(jnp.bfloat16)
        for c in range(sub // gl):
            c0 = c * gl
            full = _dot(wstack, vnb[c0:c0 + gl, :])
            s = full[0:gl]
            for g in range(1, GMLP_GROUPS):
                s = jnp.where(lane256 >= g, full[g * gl:(g + 1) * gl], s)
            mix[r0 + c0:r0 + c0 + gl, 512:768] = (u[c0:c0 + gl, :] * (s + bias_g)).astype(jnp.bfloat16)

        pext[POOL_PAD + r0:POOL_PAD + r0 + sub, :] = zp
        n = POOL_PAD - SUBLANES + sub
        lo_r = SUBLANES + r0
        pa[lo_r:lo_r + n, :] = pext[lo_r:lo_r + n, :] + pext[lo_r - 1:lo_r - 1 + n, :]
        pb[lo_r:lo_r + n, :] = pa[lo_r:lo_r + n, :] + pa[lo_r - 2:lo_r - 2 + n, :]
        tr = slice(POOL_PAD + r0, POOL_PAD + r0 + sub)
        s2 = pa[tr, 0:LANES]
        s4 = pb[tr, 0:LANES]
        pa[lo_r:lo_r + n, LANES:] = pb[lo_r:lo_r + n, LANES:] + pb[lo_r - 4:lo_r - 4 + n, LANES:]
        s8 = pa[tr, LANES:]
        s16 = s8 + pa[POOL_PAD - 8 + r0:POOL_PAD - 8 + r0 + sub, LANES:]
        wsum = jnp.concatenate([jnp.where(lo, s2, s4), jnp.where(lo, s8, s16)], axis=1)
        prow = lax.broadcasted_iota(jnp.int32, (sub, 1), 0) + (pos0 + 1 + r0) + t * tm
        winl = jnp.left_shift(2, lane256).astype(jnp.float32)
        cnt = jnp.minimum(prow.astype(jnp.float32), winl)
        pooled = wsum / cnt - zp
        mix[rows, 768:1024] = (_dot(pooled.astype(jnp.bfloat16), pw_ref[...]) * pscale_ref[...]).astype(jnp.bfloat16)
        if last:
            pst_ref[...] = pext[POOL_PAD + tm - POOL_HIST:POOL_PAD + tm, :]
        yield
        xo_ref[rows, :] = x_ref[rows, :] + _dot(mix[rows, :], wout_ref[...])
        yield

    nsub = tm // sub
    gens = [subtile(i * sub, i == nsub - 1) for i in range(nsub)]
    n_stages = 7
    for step in range(n_stages + MIXER_STAGGER * (nsub - 1)):
        for i, g in enumerate(gens):
            if 0 <= step - MIXER_STAGGER * i < n_stages:
                next(g)

    if nt > 1:
        kext[0:WINDOW, :] = kext[tm:tm + WINDOW, :]
        vext[0:WINDOW, :] = vext[tm:tm + WINDOW, :]
        mext[0:SUBLANES, :] = mext[tm:tm + SUBLANES, :]
        pext[0:POOL_PAD, :] = pext[tm:tm + POOL_PAD, :]


def _const_spec(shape):
    nd = len(shape)
    return pl.BlockSpec(shape, lambda b, t, _nd=nd: (0,) * _nd, pipeline_mode=pl.Buffered(1))


def _layer_spec(stacked_shape, l):
    nd = len(stacked_shape) - 1
    return pl.BlockSpec((None,) + tuple(stacked_shape[1:]), lambda b, t, _nd=nd, _l=l: (_l,) + (0,) * _nd,
                        pipeline_mode=pl.Buffered(1))


def _window_bias(qn):
    rows = (np.arange(4 * qn) % qn) // CHUNK
    keys = np.arange(WINDOW + qn)
    ok = (keys[None, :] >= rows[:, None] * CHUNK) & (keys[None, :] < (rows[:, None] + 3) * CHUNK)
    first = ok & (keys[None, :] >= WINDOW)
    return np.where(np.stack([ok, first]), 0.0, -np.inf).astype(np.float32)


def _mixer_call(x, tabs, past, lw, l, *, tm, pos0):
    bsz, t_len, _ = x.shape
    nt = t_len // tm
    has_past = past is not None
    kr = min(WINDOW, tm)
    qn = min(ATTN_BLOCK, tm)
    in_specs = [pl.BlockSpec((None, tm, D_MODEL), lambda b, t: (b, t, 0))]
    in_specs += [pl.BlockSpec((tm, LANES), lambda b, t: (t, 0))] * 3
    args = [x, *tabs]
    if has_past:
        for a in past:
            in_specs.append(pl.BlockSpec((None, None) + a.shape[2:], lambda b, t, _l=l: (b, _l, 0, 0)))
            args.append(a)
    else:
        bias = jnp.asarray(_window_bias(qn))
        in_specs.append(_const_spec(bias.shape))
        args.append(bias)
    wnames = ('g_mix', 'w_in', 'sink', 'conv_w', 'ln_g', 'ln_b', 'gmlp_w', 'gmlp_b', 'pool_w', 'pool_scale', 'w_out')
    for n in wnames:
        a = lw[n]
        if n == 'sink':
            in_specs.append(pl.BlockSpec(memory_space=pltpu.SMEM))
        else:
            in_specs.append(_layer_spec(a.shape, l))
        args.append(a)
    out_shape = [jax.ShapeDtypeStruct((bsz, t_len, D_MODEL), jnp.float32),
                 jax.ShapeDtypeStruct((bsz, kr, KV_WIDTH), jnp.float32),
                 jax.ShapeDtypeStruct((bsz, kr, KV_WIDTH), jnp.float32),
                 jax.ShapeDtypeStruct((bsz, CONV_WIDTH - 1, W_CONV), jnp.float32),
                 jax.ShapeDtypeStruct((bsz, POOL_HIST, W_POOL), jnp.float32)]
    out_specs = [pl.BlockSpec((None, tm, D_MODEL), lambda b, t: (b, t, 0)),
                 pl.BlockSpec((None, kr, KV_WIDTH), lambda b, t: (b, 0, 0)),
                 pl.BlockSpec((None, kr, KV_WIDTH), lambda b, t: (b, 0, 0)),
                 pl.BlockSpec((None, CONV_WIDTH - 1, W_CONV), lambda b, t: (b, 0, 0)),
                 pl.BlockSpec((None, POOL_HIST, W_POOL), lambda b, t: (b, 0, 0))]
    if has_past:
        out_shape.append(jax.ShapeDtypeStruct((bsz, t_len, W_GMLP), jnp.float32))
        out_specs.append(pl.BlockSpec((None, tm, W_GMLP), lambda b, t: (b, t, 0)))
    scratch = [pltpu.VMEM((WINDOW + tm, KV_WIDTH), jnp.bfloat16),
               pltpu.VMEM((WINDOW + tm, KV_WIDTH), jnp.bfloat16),
               pltpu.VMEM((SUBLANES + tm, W_CONV), jnp.float32),
               pltpu.VMEM((POOL_PAD + tm, W_POOL), jnp.float32),
               pltpu.VMEM((POOL_PAD + tm, W_POOL), jnp.float32),
               pltpu.VMEM((POOL_PAD + tm, W_POOL), jnp.float32),
               pltpu.VMEM((tm, D_MODEL), jnp.bfloat16),
               pltpu.VMEM((tm, D_MODEL), jnp.bfloat16)]
    return pl.pallas_call(
        functools.partial(_mixer_kernel, tm=tm, nt=nt, has_past=has_past, pos0=pos0, layer=l),
        grid=(bsz, nt),
        in_specs=in_specs,
        out_specs=out_specs,
        out_shape=out_shape,
        scratch_shapes=scratch,
        compiler_params=pltpu.CompilerParams(
            dimension_semantics=("arbitrary", "arbitrary"), vmem_limit_bytes=VMEM_LIMIT),
        name="mixer_past" if has_past else "mixer",
    )(*args)


def _ffn_kernel(*refs, tm, nt, chained, final):
    it = iter(refs)
    x_ref = next(it)
    if not chained:
        past_ref = next(it)
    gffn_ref, wup_ref, fcw_ref, wdown_ref = next(it), next(it), next(it), next(it)
    if final:
        gfin_ref = next(it)
    xo_ref, fst_ref = next(it), next(it)
    carry, xacc, hbuf, act = next(it), next(it), next(it), next(it)

    t = pl.program_id(1)
    fc = FF_CHUNK
    halves = (0, D_FF)
    seg = tm // SUBLANES

    if chained:
        @pl.when(t == 0)
        def _init():
            carry[...] = jnp.zeros(carry.shape, jnp.float32)

    xacc[...] = jnp.swapaxes(x_ref[...].reshape(SUBLANES, seg, D_MODEL), 0, 1).reshape(tm, D_MODEL)
    hbuf[...] = _rmsnorm(xacc[...], gffn_ref[...]).astype(jnp.bfloat16)
    sub = lax.broadcasted_iota(jnp.int32, (SUBLANES, fc), 0)
    for c in range(N_FF_CHUNKS):
        ys = []
        for hi, off in enumerate(halves):
            cols = slice(off + c * fc, off + (c + 1) * fc)
            lanes = slice(hi * fc, (hi + 1) * fc)
            up = _dot(hbuf[...], wup_ref[:, cols])
            last1, last2 = up[tm - SUBLANES:tm], up[tm - 2 * SUBLANES:tm - SUBLANES]
            if chained:
                p1 = jnp.where(sub == 0, pltpu.roll(carry[c, 1, :, lanes], 1, axis=0), pltpu.roll(last1, 1, axis=0))
                p2 = jnp.where(sub == 0, pltpu.roll(carry[c, 0, :, lanes], 1, axis=0), pltpu.roll(last2, 1, axis=0))
                fst_ref[0:1, cols] = last2[SUBLANES - 1:SUBLANES]
                fst_ref[1:2, cols] = last1[SUBLANES - 1:SUBLANES]
                if nt > 1:
                    carry[c, 0, :, lanes] = last2
                    carry[c, 1, :, lanes] = last1
            else:
                p2, p1 = past_ref[0, :, cols], past_ref[1, :, cols]
                fst_ref[0, :, cols] = last2
                fst_ref[1, :, cols] = last1
            d1 = jnp.concatenate([p1, up[0:tm - SUBLANES]], axis=0)
            d2 = jnp.concatenate([p2, p1, up[0:tm - 2 * SUBLANES]], axis=0)
            cw = fcw_ref[:, cols]
            ys.append(d2 * cw[0:1, :] + d1 * cw[1:2, :] + up * cw[2:3, :])
        gate, val = ys
        act[:, c * fc:(c + 1) * fc] = (gate * jax.nn.sigmoid(gate) * val).astype(jnp.bfloat16)
    out = xacc[...] + _dot(act[...], wdown_ref[...])
    if final:
        out = _rmsnorm(out, gfin_ref[...])
    xo_ref[...] = jnp.swapaxes(out.reshape(seg, SUBLANES, D_MODEL), 0, 1).reshape(tm, D_MODEL)


def _ffn_call(x, fpast, lw, l, g_final, *, tm):
    bsz, t_len, _ = x.shape
    chained = fpast is None
    final = g_final is not None
    args = []
    if chained:
        nt = t_len // tm
        grid = (bsz, nt)
        st_shape = (bsz, CONV_WIDTH - 1, 2 * D_FF)
        st_spec = pl.BlockSpec((None, CONV_WIDTH - 1, 2 * D_FF), lambda b, t: (b, 0, 0))
        in_specs = []
    else:
        assert bsz == SUBLANES and t_len % SUBLANES == 0
        x = x.reshape(1, bsz * t_len, D_MODEL)
        tm, nt, grid = bsz * t_len, 1, (1, 1)
        st_shape = (CONV_WIDTH - 1, bsz, 2 * D_FF)
        st_spec = pl.BlockSpec(st_shape, lambda b, t: (0, 0, 0))
        in_specs = [pl.BlockSpec((None,) + st_shape, lambda b, t, _l=l: (_l, 0, 0, 0))]
        args.append(fpast)
    assert tm % (SUBLANES * SUBLANES) == 0
    in_specs = [pl.BlockSpec((None, tm, D_MODEL), lambda b, t: (b, t, 0))] + in_specs
    args = [x] + args
    for n in ('g_ffn', 'w_up', 'ffn_conv_w', 'w_down'):
        in_specs.append(_layer_spec(lw[n].shape, l))
        args.append(lw[n])
    if final:
        in_specs.append(_const_spec(g_final.shape))
        args.append(g_final)
    out_shape = [jax.ShapeDtypeStruct(x.shape, jnp.float32), jax.ShapeDtypeStruct(st_shape, jnp.float32)]
    out_specs = [pl.BlockSpec((None, tm, D_MODEL), lambda b, t: (b, t, 0)), st_spec]
    scratch = [pltpu.VMEM((N_FF_CHUNKS, CONV_WIDTH - 1, SUBLANES, 2 * FF_CHUNK), jnp.float32),
               pltpu.VMEM((tm, D_MODEL), jnp.float32),
               pltpu.VMEM((tm, D_MODEL), jnp.bfloat16),
               pltpu.VMEM((tm, D_FF), jnp.bfloat16)]
    y, fst = pl.pallas_call(
        functools.partial(_ffn_kernel, tm=tm, nt=nt, chained=chained, final=final),
        grid=grid,
        in_specs=in_specs,
        out_specs=out_specs,
        out_shape=out_shape,
        scratch_shapes=scratch,
        compiler_params=pltpu.CompilerParams(
            dimension_semantics=("arbitrary", "arbitrary"), vmem_limit_bytes=VMEM_LIMIT),
        name=("ffn" if chained else "ffn_past") + ("_final" if final else ""),
    )(*args)
    if not chained:
        y, fst = y.reshape(bsz, t_len, D_MODEL), jnp.swapaxes(fst, 0, 1)
    return y, fst


def _swap_mid_heads(a, axis):
    parts = jnp.split(a, N_Q_HEADS, axis=axis)
    return jnp.concatenate([parts[0], parts[2], parts[1], parts[3]], axis=axis)


def _rope_tables(pos0, t_len):
    half = ROT_DIM // 2
    inv_freq = np.power(np.float64(ROPE_THETA), -np.arange(half, dtype=np.float64) * (2.0 / ROT_DIM))
    ang = (pos0 + np.arange(t_len, dtype=np.float64))[:, None] * inv_freq[None, :]
    cos, sin = np.cos(ang), np.sin(ang)
    rest = HEAD_DIM - ROT_DIM
    cos_h = np.concatenate([cos, cos, np.ones((t_len, rest))], axis=1)
    sinlo_h = np.concatenate([-sin, np.zeros((t_len, half + rest))], axis=1)
    sinhi_h = np.concatenate([np.zeros((t_len, half)), sin, np.zeros((t_len, rest))], axis=1)
    return tuple(jnp.asarray(np.concatenate([a, a], axis=1), dtype=jnp.float32) for a in (cos_h, sinlo_h, sinhi_h))


def _stacked_weights(g_mix, w_in, attn_sink, conv_w, gmlp_ln_g, gmlp_ln_b, gmlp_w, gmlp_b, pool_w, pool_scale,
                     w_out, g_ffn, w_up, ffn_conv_w, w_down):
    win = jnp.concatenate([_swap_mid_heads(w_in[:, :, :W_ATTN], 2), w_in[:, :, W_ATTN:]], axis=2)
    wout = jnp.concatenate([_swap_mid_heads(w_out[:, :W_ATTN], 1), w_out[:, W_ATTN:]], axis=1)
    eye = jnp.eye(len(POOL_WINDOWS), dtype=pool_w.dtype)
    pw = jnp.einsum('lgij,gh->lgihj', pool_w, eye).reshape(DEPTH, W_POOL, W_POOL)
    return {
        'g_mix': g_mix[:, None, :],
        'w_in': win.astype(jnp.bfloat16),
        'sink': attn_sink,
        'conv_w': conv_w,
        'ln_g': gmlp_ln_g[:, None, :],
        'ln_b': gmlp_ln_b[:, None, :],
        'gmlp_w': gmlp_w,
        'gmlp_b': jnp.repeat(jnp.swapaxes(gmlp_b, 1, 2), GMLP_GROUP_DIM, axis=2),
        'pool_w': pw.astype(jnp.bfloat16),
        'pool_scale': pool_scale[:, None, :],
        'w_out': wout.astype(jnp.bfloat16),
        'g_ffn': g_ffn[:, None, :],
        'w_up': w_up.astype(jnp.bfloat16),
        'ffn_conv_w': ffn_conv_w,
        'w_down': w_down.astype(jnp.bfloat16),
    }


def _run(x, pos0, pasts, lw, g_final, tm):
    bsz, t_len, _ = x.shape
    tabs = _rope_tables(pos0, t_len)
    if pasts is not None:
        ck, cv, sc, sp, sf = pasts
        mpast = (ck.reshape(bsz, DEPTH, WINDOW, KV_WIDTH), cv.reshape(bsz, DEPTH, WINDOW, KV_WIDTH), sc, sp)
        fpast = jnp.transpose(sf, (1, 2, 0, 3))
    else:
        mpast, fpast = None, None
    states = []
    for l in range(DEPTH):
        mo = _mixer_call(x, tabs, mpast, lw, l, tm=tm, pos0=pos0)
        x_mid, kst, vst, cst, pst = mo[:5]
        gfin = g_final[None, :] if l == DEPTH - 1 else None
        x, fst = _ffn_call(x_mid, fpast, lw, l, gfin, tm=tm)
        st = [kst.reshape(bsz, -1, N_KV_HEADS, HEAD_DIM), vst.reshape(bsz, -1, N_KV_HEADS, HEAD_DIM),
              cst, pst, fst]
        if pasts is not None:
            st.append(mo[5])
        states.append(st)
    stacked = [jnp.stack([st[i] for st in states], axis=1) for i in range(len(states[0]))]
    return x, stacked


def kernel(x_prompt, x_sample, cache_attn_k, cache_attn_v, state_conv, state_pool, state_ffn_conv, g_mix, w_in,
           attn_sink, conv_w, gmlp_ln_g, gmlp_ln_b, gmlp_w, gmlp_b, pool_w, pool_scale, w_out, g_ffn, w_up,
           ffn_conv_w, w_down, g_final):
    lw = _stacked_weights(g_mix, w_in, attn_sink, conv_w, gmlp_ln_g, gmlp_ln_b, gmlp_w, gmlp_b, pool_w,
                          pool_scale, w_out, g_ffn, w_up, ffn_conv_w, w_down)
    y_prompt, sp = _run(x_prompt, 0, None, lw, g_final, tm=1024)
    y_sample, ss = _run(x_sample, PAST_LEN, (cache_attn_k, cache_attn_v, state_conv, state_pool, state_ffn_conv),
                        lw, g_final, tm=x_sample.shape[1])
    return (y_prompt, y_sample, sp[0], sp[1], sp[2], sp[3], sp[4], ss[0], ss[1], ss[2], ss[3], ss[4], ss[5])
```

```python
import functools

import numpy as np
import jax
import jax.numpy as jnp
from jax import lax
from jax.experimental import pallas as pl
from jax.experimental.pallas import tpu as pltpu

D_MODEL = 1024
DEPTH = 2
CHUNK = 64
HEAD_DIM = 64
N_Q_HEADS = 4
N_KV_HEADS = 2
KV_WIDTH = N_KV_HEADS * HEAD_DIM
W_ATTN = 256
W_CONV = 256
W_GMLP = 256
W_POOL = 256
WINDOW = 128
ROPE_THETA = 500000.0
ROT_DIM = 16
CONV_WIDTH = 3
GMLP_CHUNK = 128
GMLP_GROUPS = 4
GMLP_GROUP_DIM = 64
POOL_WINDOWS = (2, 4, 8, 16)
POOL_HIST = 15
D_FF = 2816
NORM_EPS = 1e-6
PROJ_WIDTH = 2048
PAST_LEN = 2048

LANES = 128
SUBLANES = 8
FF_CHUNK = 256
N_FF_CHUNKS = D_FF // FF_CHUNK
POOL_PAD = 24
ATTN_BLOCK = 2 * CHUNK
MIXER_SUB = 256
MIXER_STAGGER = 0
VMEM_LIMIT = 56 * 1024 * 1024

_Q0, _K0, _V0 = 0, 256, 384
_CB0, _CC0, _CH0 = 512, 768, 1024
_GU0, _GV0 = 1280, 1536
_P0 = 1792


def _rmsnorm(x, g):
    ms = jnp.mean(x * x, axis=-1, keepdims=True)
    return x * lax.rsqrt(ms + NORM_EPS) * g


def _dot(a, b):
    return jnp.dot(a, b, preferred_element_type=jnp.float32)


def _dot_nt(a, b):
    return lax.dot_general(a, b, (((1,), (1,)), ((), ())), preferred_element_type=jnp.float32)


def _rope(x, cos_t, sin_lo, sin_hi):
    up = pltpu.roll(x, LANES - ROT_DIM // 2, axis=1)
    dn = pltpu.roll(x, ROT_DIM // 2, axis=1)
    return x * cos_t + up * sin_lo + dn * sin_hi


def _swap_halves(x):
    return pltpu.roll(x, HEAD_DIM, axis=1)


def _mixer_kernel(*refs, tm, nt, has_past, pos0, layer):
    it = iter(refs)
    x_ref = next(it)
    cos_ref, sinlo_ref, sinhi_ref = next(it), next(it), next(it)
    if has_past:
        kpast_ref, vpast_ref, cpast_ref, ppast_ref = next(it), next(it), next(it), next(it)
    else:
        bias_ref = next(it)
    gmix_ref, win_ref, sink_ref, convw_ref = next(it), next(it), next(it), next(it)
    lng_ref, lnb_ref, gw_ref, gb_ref = next(it), next(it), next(it), next(it)
    pw_ref, pscale_ref, wout_ref = next(it), next(it), next(it)
    xo_ref, kst_ref, vst_ref, cst_ref, pst_ref = next(it), next(it), next(it), next(it), next(it)
    if has_past:
        grows_ref = next(it)
    kext, vext, kext2, vext2 = next(it), next(it), next(it), next(it)
    mext, pext, pa, pb, hb, mix = next(it), next(it), next(it), next(it), next(it), next(it)

    t = pl.program_id(1)
    qn = min(ATTN_BLOCK, tm)
    nkeys = WINDOW + qn
    kr = min(WINDOW, tm)
    gl = min(GMLP_CHUNK, tm)
    sub = min(MIXER_SUB, tm)

    @pl.when(t == 0)
    def _init():
        mext[0:SUBLANES, :] = jnp.zeros((SUBLANES, W_CONV), jnp.float32)
        pext[0:POOL_PAD, :] = jnp.zeros((POOL_PAD, W_POOL), jnp.float32)
        pa[0:SUBLANES, :] = jnp.zeros((SUBLANES, W_POOL), jnp.float32)
        pb[0:SUBLANES, :] = jnp.zeros((SUBLANES, W_POOL), jnp.float32)
        if has_past:
            kp, vp = kpast_ref[...], vpast_ref[...]
            kext[0:WINDOW, :] = kp.astype(jnp.bfloat16)
            vext[0:WINDOW, :] = vp.astype(jnp.bfloat16)
            kext2[0:WINDOW, :] = _swap_halves(kp).astype(jnp.bfloat16)
            vext2[0:WINDOW, :] = _swap_halves(vp).astype(jnp.bfloat16)
            mext[SUBLANES - (CONV_WIDTH - 1):SUBLANES, :] = cpast_ref[...]
            pext[POOL_PAD - POOL_HIST:POOL_PAD, :] = ppast_ref[...]
        else:
            for buf in (kext, vext, kext2, vext2):
                buf[0:WINDOW, :] = jnp.zeros((WINDOW, KV_WIDTH), jnp.bfloat16)

    lane = lax.broadcasted_iota(jnp.int32, (1, LANES), 1)
    lo = lane < HEAD_DIM
    lane256 = lax.broadcasted_iota(jnp.int32, (1, W_GMLP), 1) // GMLP_GROUP_DIM
    row4 = lax.broadcasted_iota(jnp.int32, (4 * qn, 1), 0)
    sk = jnp.where(row4 < qn, sink_ref[layer, 0],
                   jnp.where(row4 < 2 * qn, sink_ref[layer, 3],
                             jnp.where(row4 < 3 * qn, sink_ref[layer, 1], sink_ref[layer, 2])))
    gi = lax.broadcasted_iota(jnp.int32, (gl, gl), 0)
    gj = lax.broadcasted_iota(jnp.int32, (gl, gl), 1)
    tri = (gj // CHUNK) <= (gi // CHUNK)
    wstack = jnp.concatenate([jnp.where(tri, gw_ref[g, 0:gl, 0:gl], 0.0).astype(jnp.bfloat16)
                              for g in range(GMLP_GROUPS)], axis=0)
    bias_g = gb_ref[0:gl, :]
    scale = HEAD_DIM ** -0.5

    def subtile(r0, last):
        rows = slice(r0, r0 + sub)
        krows = slice(WINDOW + r0, WINDOW + r0 + sub)
        x = x_ref[rows, :]
        hb[rows, :] = _rmsnorm(x, gmix_ref[...]).astype(jnp.bfloat16)
        zq = _dot(hb[rows, :], win_ref[:, _Q0:_CB0])
        cos_t, sin_lo, sin_hi = cos_ref[rows, :], sinlo_ref[rows, :], sinhi_ref[rows, :]
        qa = _rope(zq[:, 0:128], cos_t, sin_lo, sin_hi) * scale
        qb = _rope(zq[:, 128:256], cos_t, sin_lo, sin_hi) * scale
        k = _rope(zq[:, 256:384], cos_t, sin_lo, sin_hi)
        v = zq[:, 384:512]
        kext[krows, :] = k.astype(jnp.bfloat16)
        vext[krows, :] = v.astype(jnp.bfloat16)
        kext2[krows, :] = _swap_halves(k).astype(jnp.bfloat16)
        vext2[krows, :] = _swap_halves(v).astype(jnp.bfloat16)
        if last:
            kst_ref[...] = k[sub - kr:sub, :]
            vst_ref[...] = v[sub - kr:sub, :]
        yield
        scores = []
        for b in range(sub // qn):
            q0 = b * qn
            wrows = slice(r0 + q0, r0 + q0 + nkeys)
            qa_b, qb_b = qa[q0:q0 + qn, :], qb[q0:q0 + qn, :]
            q03 = jnp.concatenate([jnp.where(lo, qa_b, 0.0), jnp.where(lo, 0.0, qb_b)], axis=0)
            q12 = jnp.concatenate([jnp.where(lo, 0.0, qa_b), jnp.where(lo, qb_b, 0.0)], axis=0)
            s = jnp.concatenate([_dot_nt(q03.astype(jnp.bfloat16), kext[wrows, :]),
                                 _dot_nt(q12.astype(jnp.bfloat16), kext2[wrows, :])], axis=0)
            if not has_past:
                s = s + (bias_ref[0] if r0 + q0 > 0 else bias_ref[jnp.where(t == 0, 1, 0)])
            scores.append(s)
        yield
        zc = _dot(hb[rows, :], win_ref[:, _CB0:_GU0])
        zg = _dot(hb[rows, :], win_ref[:, _GU0:_P0])
        zp = _dot(hb[rows, :], win_ref[:, _P0:PROJ_WIDTH])
        yield
        probs, dens = [], []
        for s in scores:
            m = jnp.maximum(jnp.max(s, axis=-1, keepdims=True), sk)
            p = jnp.exp(s - m)
            dens.append(jnp.sum(p, axis=-1, keepdims=True) + jnp.exp(sk - m))
            probs.append(p.astype(jnp.bfloat16))
        yield
        for b in range(sub // qn):
            q0 = r0 + b * qn
            wrows = slice(q0, q0 + nkeys)
            o03 = _dot(probs[b][0:2 * qn], vext[wrows, :]) / dens[b][0:2 * qn]
            o12 = _dot(probs[b][2 * qn:4 * qn], vext2[wrows, :]) / dens[b][2 * qn:4 * qn]
            oa = jnp.where(lo, o03[0:qn], o12[0:qn])
            ob = jnp.where(lo, o12[qn:2 * qn], o03[qn:2 * qn])
            mix[q0:q0 + qn, 0:128] = oa.astype(jnp.bfloat16)
            mix[q0:q0 + qn, 128:256] = ob.astype(jnp.bfloat16)
        yield
        mprod = zc[:, 256:512] * zc[:, 512:768]
        mext[SUBLANES + r0:SUBLANES + r0 + sub, :] = mprod
        cw = convw_ref[...]
        yc = (mext[SUBLANES - 2 + r0:SUBLANES - 2 + r0 + sub, :] * cw[0:1, :]
              + mext[SUBLANES - 1 + r0:SUBLANES - 1 + r0 + sub, :] * cw[1:2, :]
              + mprod * cw[2:3, :])
        mix[rows, 256:512] = (zc[:, 0:256] * yc).astype(jnp.bfloat16)
        if last:
            cst_ref[...] = mext[SUBLANES + tm - 2:SUBLANES + tm, :]

        u = jax.nn.gelu(zg[:, 0:256])
        gv = jax.nn.gelu(zg[:, 256:512])
        mu = jnp.mean(gv, axis=-1, keepdims=True)
        var = jnp.mean(jnp.square(gv - mu), axis=-1, keepdims=True)
        vn = (gv - mu) * lax.rsqrt(var + NORM_EPS) * lng_ref[...] + lnb_ref[...]
        if has_past:
            grows_ref[rows, :] = vn
        vnb = vn.astype(jnp.bfloat16)
        for c in range(sub // gl):
            c0 = c * gl
            full = _dot(wstack, vnb[c0:c0 + gl, :])
            s = full[0:gl]
            for g in range(1, GMLP_GROUPS):
                s = jnp.where(lane256 >= g, full[g * gl:(g + 1) * gl], s)
            mix[r0 + c0:r0 + c0 + gl, 512:768] = (u[c0:c0 + gl, :] * (s + bias_g)).astype(jnp.bfloat16)

        pext[POOL_PAD + r0:POOL_PAD + r0 + sub, :] = zp
        n = POOL_PAD - SUBLANES + sub
        lo_r = SUBLANES + r0
        pa[lo_r:lo_r + n, :] = pext[lo_r:lo_r + n, :] + pext[lo_r - 1:lo_r - 1 + n, :]
        pb[lo_r:lo_r + n, :] = pa[lo_r:lo_r + n, :] + pa[lo_r - 2:lo_r - 2 + n, :]
        tr = slice(POOL_PAD + r0, POOL_PAD + r0 + sub)
        s2 = pa[tr, 0:LANES]
        s4 = pb[tr, 0:LANES]
        pa[lo_r:lo_r + n, LANES:] = pb[lo_r:lo_r + n, LANES:] + pb[lo_r - 4:lo_r - 4 + n, LANES:]
        s8 = pa[tr, LANES:]
        s16 = s8 + pa[POOL_PAD - 8 + r0:POOL_PAD - 8 + r0 + sub, LANES:]
        wsum = jnp.concatenate([jnp.where(lo, s2, s4), jnp.where(lo, s8, s16)], axis=1)
        prow = lax.broadcasted_iota(jnp.int32, (sub, 1), 0) + (pos0 + 1 + r0) + t * tm
        winl = jnp.left_shift(2, lane256).astype(jnp.float32)
        cnt = jnp.minimum(prow.astype(jnp.float32), winl)
        pooled = wsum / cnt - zp
        mix[rows, 768:1024] = (_dot(pooled.astype(jnp.bfloat16), pw_ref[...]) * pscale_ref[...]).astype(jnp.bfloat16)
        if last:
            pst_ref[...] = pext[POOL_PAD + tm - POOL_HIST:POOL_PAD + tm, :]
        yield
        xo_ref[rows, :] = x_ref[rows, :] + _dot(mix[rows, :], wout_ref[...])
        yield

    nsub = tm // sub
    gens = [subtile(i * sub, i == nsub - 1) for i in range(nsub)]
    n_stages = 7
    for step in range(n_stages + MIXER_STAGGER * (nsub - 1)):
        for i, g in enumerate(gens):
            if 0 <= step - MIXER_STAGGER * i < n_stages:
                next(g)

    if nt > 1:
        for buf in (kext, vext, kext2, vext2):
            buf[0:WINDOW, :] = buf[tm:tm + WINDOW, :]
        mext[0:SUBLANES, :] = mext[tm:tm + SUBLANES, :]
        pext[0:POOL_PAD, :] = pext[tm:tm + POOL_PAD, :]


def _const_spec(shape):
    nd = len(shape)
    return pl.BlockSpec(shape, lambda b, t, _nd=nd: (0,) * _nd, pipeline_mode=pl.Buffered(1))


def _layer_spec(stacked_shape, l):
    nd = len(stacked_shape) - 1
    return pl.BlockSpec((None,) + tuple(stacked_shape[1:]), lambda b, t, _nd=nd, _l=l: (_l,) + (0,) * _nd,
                        pipeline_mode=pl.Buffered(1))


def _window_bias(qn):
    rows = (np.arange(4 * qn) % qn) // CHUNK
    keys = np.arange(WINDOW + qn)
    ok = (keys[None, :] >= rows[:, None] * CHUNK) & (keys[None, :] < (rows[:, None] + 3) * CHUNK)
    first = ok & (keys[None, :] >= WINDOW)
    return np.where(np.stack([ok, first]), 0.0, -np.inf).astype(np.float32)


def _mixer_call(x, tabs, past, lw, l, *, tm, pos0):
    bsz, t_len, _ = x.shape
    nt = t_len // tm
    has_past = past is not None
    kr = min(WINDOW, tm)
    qn = min(ATTN_BLOCK, tm)
    in_specs = [pl.BlockSpec((None, tm, D_MODEL), lambda b, t: (b, t, 0))]
    in_specs += [pl.BlockSpec((tm, LANES), lambda b, t: (t, 0))] * 3
    args = [x, *tabs]
    if has_past:
        for a in past:
            in_specs.append(pl.BlockSpec((None, None) + a.shape[2:], lambda b, t, _l=l: (b, _l, 0, 0)))
            args.append(a)
    else:
        bias = jnp.asarray(_window_bias(qn))
        in_specs.append(_const_spec(bias.shape))
        args.append(bias)
    wnames = ('g_mix', 'w_in', 'sink', 'conv_w', 'ln_g', 'ln_b', 'gmlp_w', 'gmlp_b', 'pool_w', 'pool_scale', 'w_out')
    for n in wnames:
        a = lw[n]
        if n == 'sink':
            in_specs.append(pl.BlockSpec(memory_space=pltpu.SMEM))
        else:
            in_specs.append(_layer_spec(a.shape, l))
        args.append(a)
    out_shape = [jax.ShapeDtypeStruct((bsz, t_len, D_MODEL), jnp.float32),
                 jax.ShapeDtypeStruct((bsz, kr, KV_WIDTH), jnp.float32),
                 jax.ShapeDtypeStruct((bsz, kr, KV_WIDTH), jnp.float32),
                 jax.ShapeDtypeStruct((bsz, CONV_WIDTH - 1, W_CONV), jnp.float32),
                 jax.ShapeDtypeStruct((bsz, POOL_HIST, W_POOL), jnp.float32)]
    out_specs = [pl.BlockSpec((None, tm, D_MODEL), lambda b, t: (b, t, 0)),
                 pl.BlockSpec((None, kr, KV_WIDTH), lambda b, t: (b, 0, 0)),
                 pl.BlockSpec((None, kr, KV_WIDTH), lambda b, t: (b, 0, 0)),
                 pl.BlockSpec((None, CONV_WIDTH - 1, W_CONV), lambda b, t: (b, 0, 0)),
                 pl.BlockSpec((None, POOL_HIST, W_POOL), lambda b, t: (b, 0, 0))]
    if has_past:
        out_shape.append(jax.ShapeDtypeStruct((bsz, t_len, W_GMLP), jnp.float32))
        out_specs.append(pl.BlockSpec((None, tm, W_GMLP), lambda b, t: (b, t, 0)))
    scratch = [pltpu.VMEM((WINDOW + tm, KV_WIDTH), jnp.bfloat16)] * 4
    scratch += [pltpu.VMEM((SUBLANES + tm, W_CONV), jnp.float32),
                pltpu.VMEM((POOL_PAD + tm, W_POOL), jnp.float32),
                pltpu.VMEM((POOL_PAD + tm, W_POOL), jnp.float32),
                pltpu.VMEM((POOL_PAD + tm, W_POOL), jnp.float32),
                pltpu.VMEM((tm, D_MODEL), jnp.bfloat16),
                pltpu.VMEM((tm, D_MODEL), jnp.bfloat16)]
    return pl.pallas_call(
        functools.partial(_mixer_kernel, tm=tm, nt=nt, has_past=has_past, pos0=pos0, layer=l),
        grid=(bsz, nt),
        in_specs=in_specs,
        out_specs=out_specs,
        out_shape=out_shape,
        scratch_shapes=scratch,
        compiler_params=pltpu.CompilerParams(
            dimension_semantics=("arbitrary", "arbitrary"), vmem_limit_bytes=VMEM_LIMIT),
        name="mixer_past" if has_past else "mixer",
    )(*args)


def _ffn_kernel(*refs, tm, nt, nsteps, chained, final):
    it = iter(refs)
    x_ref = next(it)
    if not chained:
        past_ref = next(it)
    gffn_ref, wup_ref, fcw_ref, wdown_ref = next(it), next(it), next(it), next(it)
    if final:
        gfin_ref = next(it)
    xo_ref, fst_ref = next(it), next(it)
    carry, hbuf, act = next(it), next(it), next(it)
    if chained:
        xbuf, obuf, sem_in, sem_out = next(it), next(it), next(it), next(it)

    t = pl.program_id(1)
    fc = FF_CHUNK
    halves = (0, D_FF)
    seg = tm // SUBLANES

    if chained:
        step = pl.program_id(0) * nt + t
        slot = step % 2

        def tile_copies(tile, sl, outward):
            tb, row0 = tile // nt, (tile % nt) * tm
            cps = []
            for s in range(SUBLANES):
                hbm = (xo_ref if outward else x_ref).at[tb, pl.ds(row0 + s * seg, seg), :]
                if outward:
                    cps.append(pltpu.make_async_copy(obuf.at[sl, :, s, :], hbm, sem_out.at[sl, s]))
                else:
                    cps.append(pltpu.make_async_copy(hbm, xbuf.at[sl, :, s, :], sem_in.at[sl, s]))
            return cps

        @pl.when(step == 0)
        def _first_in():
            for cp in tile_copies(0, 0, False):
                cp.start()

        @pl.when(step + 1 < nsteps)
        def _next_in():
            for cp in tile_copies(step + 1, 1 - slot, False):
                cp.start()

        @pl.when(step >= 2)
        def _free_out():
            for cp in tile_copies(step - 2, slot, True):
                cp.wait()

        for cp in tile_copies(step, slot, False):
            cp.wait()

        @pl.when(t == 0)
        def _init():
            carry[...] = jnp.zeros(carry.shape, jnp.float32)

        xp = xbuf[slot].reshape(tm, D_MODEL)
    else:
        xp = jnp.swapaxes(x_ref[...].reshape(SUBLANES, seg, D_MODEL), 0, 1).reshape(tm, D_MODEL)
    hbuf[...] = _rmsnorm(xp, gffn_ref[...]).astype(jnp.bfloat16)
    sub = lax.broadcasted_iota(jnp.int32, (SUBLANES, fc), 0)
    for c in range(N_FF_CHUNKS):
        ys = []
        for hi, off in enumerate(halves):
            cols = slice(off + c * fc, off + (c + 1) * fc)
            lanes = slice(hi * fc, (hi + 1) * fc)
            up = _dot(hbuf[...], wup_ref[:, cols])
            last1, last2 = up[tm - SUBLANES:tm], up[tm - 2 * SUBLANES:tm - SUBLANES]
            if chained:
                p1 = jnp.where(sub == 0, pltpu.roll(carry[c, 1, :, lanes], 1, axis=0), pltpu.roll(last1, 1, axis=0))
                p2 = jnp.where(sub == 0, pltpu.roll(carry[c, 0, :, lanes], 1, axis=0), pltpu.roll(last2, 1, axis=0))
                fst_ref[0:1, cols] = last2[SUBLANES - 1:SUBLANES]
                fst_ref[1:2, cols] = last1[SUBLANES - 1:SUBLANES]
                if nt > 1:
                    carry[c, 0, :, lanes] = last2
                    carry[c, 1, :, lanes] = last1
            else:
                p2, p1 = past_ref[0, :, cols], past_ref[1, :, cols]
                fst_ref[0, :, cols] = last2
                fst_ref[1, :, cols] = last1
            d1 = jnp.concatenate([p1, up[0:tm - SUBLANES]], axis=0)
            d2 = jnp.concatenate([p2, p1, up[0:tm - 2 * SUBLANES]], axis=0)
            cw = fcw_ref[:, cols]
            ys.append(d2 * cw[0:1, :] + d1 * cw[1:2, :] + up * cw[2:3, :])
        gate, val = ys
        act[:, c * fc:(c + 1) * fc] = (gate * jax.nn.sigmoid(gate) * val).astype(jnp.bfloat16)
    down = _dot(act[...], wdown_ref[...])
    if chained:
        out = xbuf[slot].reshape(tm, D_MODEL) + down
    else:
        out = xp + down
    if final:
        out = _rmsnorm(out, gfin_ref[...])
    if chained:
        obuf[slot] = out.reshape(seg, SUBLANES, D_MODEL)
        for cp in tile_copies(step, slot, True):
            cp.start()

        @pl.when(step == nsteps - 1)
        def _drain():
            if nsteps > 1:
                for cp in tile_copies(step - 1, 1 - slot, True):
                    cp.wait()
            for cp in tile_copies(step, slot, True):
                cp.wait()
    else:
        xo_ref[...] = jnp.swapaxes(out.reshape(seg, SUBLANES, D_MODEL), 0, 1).reshape(tm, D_MODEL)


def _ffn_call(x, fpast, lw, l, g_final, *, tm):
    bsz, t_len, _ = x.shape
    chained = fpast is None
    final = g_final is not None
    args = []
    if chained:
        nt = t_len // tm
        grid = (bsz, nt)
        st_shape = (bsz, CONV_WIDTH - 1, 2 * D_FF)
        st_spec = pl.BlockSpec((None, CONV_WIDTH - 1, 2 * D_FF), lambda b, t: (b, 0, 0))
        in_specs = [pl.BlockSpec(memory_space=pl.ANY)]
        x_out_spec = pl.BlockSpec(memory_space=pl.ANY)
    else:
        assert bsz == SUBLANES and t_len % SUBLANES == 0
        x = x.reshape(1, bsz * t_len, D_MODEL)
        tm, nt, grid = bsz * t_len, 1, (1, 1)
        st_shape = (CONV_WIDTH - 1, bsz, 2 * D_FF)
        st_spec = pl.BlockSpec(st_shape, lambda b, t: (0, 0, 0))
        in_specs = [pl.BlockSpec((None, tm, D_MODEL), lambda b, t: (b, t, 0)),
                    pl.BlockSpec((None,) + st_shape, lambda b, t, _l=l: (_l, 0, 0, 0))]
        x_out_spec = pl.BlockSpec((None, tm, D_MODEL), lambda b, t: (b, t, 0))
        args.append(fpast)
    assert tm % (SUBLANES * SUBLANES) == 0
    seg = tm // SUBLANES
    args = [x] + args
    for n in ('g_ffn', 'w_up', 'ffn_conv_w', 'w_down'):
        in_specs.append(_layer_spec(lw[n].shape, l))
        args.append(lw[n])
    if final:
        in_specs.append(_const_spec(g_final.shape))
        args.append(g_final)
    out_shape = [jax.ShapeDtypeStruct(x.shape, jnp.float32), jax.ShapeDtypeStruct(st_shape, jnp.float32)]
    out_specs = [x_out_spec, st_spec]
    scratch = [pltpu.VMEM((N_FF_CHUNKS, CONV_WIDTH - 1, SUBLANES, 2 * FF_CHUNK), jnp.float32),
               pltpu.VMEM((tm, D_MODEL), jnp.bfloat16),
               pltpu.VMEM((tm, D_FF), jnp.bfloat16)]
    if chained:
        scratch += [pltpu.VMEM((2, seg, SUBLANES, D_MODEL), jnp.float32),
                    pltpu.VMEM((2, seg, SUBLANES, D_MODEL), jnp.float32),
                    pltpu.SemaphoreType.DMA((2, SUBLANES)),
                    pltpu.SemaphoreType.DMA((2, SUBLANES))]
    y, fst = pl.pallas_call(
        functools.partial(_ffn_kernel, tm=tm, nt=nt, nsteps=grid[0] * grid[1], chained=chained, final=final),
        grid=grid,
        in_specs=in_specs,
        out_specs=out_specs,
        out_shape=out_shape,
        scratch_shapes=scratch,
        compiler_params=pltpu.CompilerParams(
            dimension_semantics=("arbitrary", "arbitrary"), vmem_limit_bytes=VMEM_LIMIT),
        name=("ffn" if chained else "ffn_past") + ("_final" if final else ""),
    )(*args)
    if not chained:
        y, fst = y.reshape(bsz, t_len, D_MODEL), jnp.swapaxes(fst, 0, 1)
    return y, fst


def _rope_tables(pos0, t_len):
    half = ROT_DIM // 2
    inv_freq = np.power(np.float64(ROPE_THETA), -np.arange(half, dtype=np.float64) * (2.0 / ROT_DIM))
    ang = (pos0 + np.arange(t_len, dtype=np.float64))[:, None] * inv_freq[None, :]
    cos, sin = np.cos(ang), np.sin(ang)
    rest = HEAD_DIM - ROT_DIM
    cos_h = np.concatenate([cos, cos, np.ones((t_len, rest))], axis=1)
    sinlo_h = np.concatenate([-sin, np.zeros((t_len, half + rest))], axis=1)
    sinhi_h = np.concatenate([np.zeros((t_len, half)), sin, np.zeros((t_len, rest))], axis=1)
    return tuple(jnp.asarray(np.concatenate([a, a], axis=1), dtype=jnp.float32) for a in (cos_h, sinlo_h, sinhi_h))


def _stacked_weights(g_mix, w_in, attn_sink, conv_w, gmlp_ln_g, gmlp_ln_b, gmlp_w, gmlp_b, pool_w, pool_scale,
                     w_out, g_ffn, w_up, ffn_conv_w, w_down):
    eye = jnp.eye(len(POOL_WINDOWS), dtype=pool_w.dtype)
    pw = jnp.einsum('lgij,gh->lgihj', pool_w, eye).reshape(DEPTH, W_POOL, W_POOL)
    return {
        'g_mix': g_mix[:, None, :],
        'w_in': w_in.astype(jnp.bfloat16),
        'sink': attn_sink,
        'conv_w': conv_w,
        'ln_g': gmlp_ln_g[:, None, :],
        'ln_b': gmlp_ln_b[:, None, :],
        'gmlp_w': gmlp_w,
        'gmlp_b': jnp.repeat(jnp.swapaxes(gmlp_b, 1, 2), GMLP_GROUP_DIM, axis=2),
        'pool_w': pw.astype(jnp.bfloat16),
        'pool_scale': pool_scale[:, None, :],
        'w_out': w_out.astype(jnp.bfloat16),
        'g_ffn': g_ffn[:, None, :],
        'w_up': w_up.astype(jnp.bfloat16),
        'ffn_conv_w': ffn_conv_w,
        'w_down': w_down.astype(jnp.bfloat16),
    }


def _run(x, pos0, pasts, lw, g_final, tm):
    bsz, t_len, _ = x.shape
    tabs = _rope_tables(pos0, t_len)
    if pasts is not None:
        ck, cv, sc, sp, sf = pasts
        mpast = (ck.reshape(bsz, DEPTH, WINDOW, KV_WIDTH), cv.reshape(bsz, DEPTH, WINDOW, KV_WIDTH), sc, sp)
        fpast = jnp.transpose(sf, (1, 2, 0, 3))
    else:
        mpast, fpast = None, None
    states = []
    for l in range(DEPTH):
        mo = _mixer_call(x, tabs, mpast, lw, l, tm=tm, pos0=pos0)
        x_mid, kst, vst, cst, pst = mo[:5]
        gfin = g_final[None, :] if l == DEPTH - 1 else None
        x, fst = _ffn_call(x_mid, fpast, lw, l, gfin, tm=tm)
        st = [kst.reshape(bsz, -1, N_KV_HEADS, HEAD_DIM), vst.reshape(bsz, -1, N_KV_HEADS, HEAD_DIM),
              cst, pst, fst]
        if pasts is not None:
            st.append(mo[5])
        states.append(st)
    stacked = [jnp.stack([st[i] for st in states], axis=1) for i in range(len(states[0]))]
    return x, stacked


def kernel(x_prompt, x_sample, cache_attn_k, cache_attn_v, state_conv, state_pool, state_ffn_conv, g_mix, w_in,
           attn_sink, conv_w, gmlp_ln_g, gmlp_ln_b, gmlp_w, gmlp_b, pool_w, pool_scale, w_out, g_ffn, w_up,
           ffn_conv_w, w_down, g_final):
    lw = _stacked_weights(g_mix, w_in, attn_sink, conv_w, gmlp_ln_g, gmlp_ln_b, gmlp_w, gmlp_b, pool_w,
                          pool_scale, w_out, g_ffn, w_up, ffn_conv_w, w_down)
    y_prompt, sp = _run(x_prompt, 0, None, lw, g_final, tm=1024)
    y_sample, ss = _run(x_sample, PAST_LEN, (cache_attn_k, cache_attn_v, state_conv, state_pool, state_ffn_conv),
                        lw, g_final, tm=x_sample.shape[1])
    return (y_prompt, y_sample, sp[0], sp[1], sp[2], sp[3], sp[4], ss[0], ss[1], ss[2], ss[3], ss[4], ss[5])
```

```python
import functools

import numpy as np
import jax
import jax.numpy as jnp
from jax import lax
from jax.experimental import pallas as pl
from jax.experimental.pallas import tpu as pltpu

D_MODEL = 1024
DEPTH = 2
CHUNK = 64
HEAD_DIM = 64
N_Q_HEADS = 4
N_KV_HEADS = 2
KV_WIDTH = N_KV_HEADS * HEAD_DIM
W_ATTN = 256
W_CONV = 256
W_GMLP = 256
W_POOL = 256
WINDOW = 128
ROPE_THETA = 500000.0
ROT_DIM = 16
CONV_WIDTH = 3
GMLP_CHUNK = 128
GMLP_GROUPS = 4
GMLP_GROUP_DIM = 64
POOL_WINDOWS = (2, 4, 8, 16)
POOL_HIST = 15
D_FF = 2816
NORM_EPS = 1e-6
PROJ_WIDTH = 2048
PAST_LEN = 2048

LANES = 128
SUBLANES = 8
FF_CHUNK = 256
N_FF_CHUNKS = D_FF // FF_CHUNK
POOL_PAD = 24
ATTN_BLOCK = 2 * CHUNK
MIXER_SUB = 256
MIXER_STAGGER = 0
VMEM_LIMIT = 56 * 1024 * 1024

_Q0, _K0, _V0 = 0, 256, 384
_CB0, _CC0, _CH0 = 512, 768, 1024
_GU0, _GV0 = 1280, 1536
_P0 = 1792


def _rmsnorm(x, g):
    ms = jnp.mean(x * x, axis=-1, keepdims=True)
    return x * lax.rsqrt(ms + NORM_EPS) * g


def _dot(a, b):
    return jnp.dot(a, b, preferred_element_type=jnp.float32)


def _dot_nt(a, b):
    return lax.dot_general(a, b, (((1,), (1,)), ((), ())), preferred_element_type=jnp.float32)


def _rope(x, cos_t, sin_lo, sin_hi):
    up = pltpu.roll(x, LANES - ROT_DIM // 2, axis=1)
    dn = pltpu.roll(x, ROT_DIM // 2, axis=1)
    return x * cos_t + up * sin_lo + dn * sin_hi


def _swap_halves(x):
    return pltpu.roll(x, HEAD_DIM, axis=1)


def _mixer_kernel(*refs, tm, nt, has_past, pos0, layer):
    it = iter(refs)
    x_ref = next(it)
    cos_ref, sinlo_ref, sinhi_ref = next(it), next(it), next(it)
    if has_past:
        kpast_ref, vpast_ref, cpast_ref, ppast_ref = next(it), next(it), next(it), next(it)
    else:
        bias_ref = next(it)
    gmix_ref, win_ref, sink_ref, convw_ref = next(it), next(it), next(it), next(it)
    lng_ref, lnb_ref, gw_ref, gb_ref = next(it), next(it), next(it), next(it)
    pw_ref, pscale_ref, wout_ref = next(it), next(it), next(it)
    xo_ref, kst_ref, vst_ref, cst_ref, pst_ref = next(it), next(it), next(it), next(it), next(it)
    if has_past:
        grows_ref = next(it)
    kext, vext, kext2, vext2 = next(it), next(it), next(it), next(it)
    mext, pext, pa, pb, hb, mix = next(it), next(it), next(it), next(it), next(it), next(it)

    t = pl.program_id(1)
    qn = min(ATTN_BLOCK, tm)
    nkeys = WINDOW + qn
    kr = min(WINDOW, tm)
    gl = min(GMLP_CHUNK, tm)
    sub = min(MIXER_SUB, tm)

    @pl.when(t == 0)
    def _init():
        mext[0:SUBLANES, :] = jnp.zeros((SUBLANES, W_CONV), jnp.float32)
        pext[0:POOL_PAD, :] = jnp.zeros((POOL_PAD, W_POOL), jnp.float32)
        pa[0:SUBLANES, :] = jnp.zeros((SUBLANES, W_POOL), jnp.float32)
        pb[0:SUBLANES, :] = jnp.zeros((SUBLANES, W_POOL), jnp.float32)
        if has_past:
            kp, vp = kpast_ref[...], vpast_ref[...]
            kext[0:WINDOW, :] = kp.astype(jnp.bfloat16)
            vext[0:WINDOW, :] = vp.astype(jnp.bfloat16)
            kext2[0:WINDOW, :] = _swap_halves(kp).astype(jnp.bfloat16)
            vext2[0:WINDOW, :] = _swap_halves(vp).astype(jnp.bfloat16)
            mext[SUBLANES - (CONV_WIDTH - 1):SUBLANES, :] = cpast_ref[...]
            pext[POOL_PAD - POOL_HIST:POOL_PAD, :] = ppast_ref[...]
        else:
            for buf in (kext, vext, kext2, vext2):
                buf[0:WINDOW, :] = jnp.zeros((WINDOW, KV_WIDTH), jnp.bfloat16)

    lane = lax.broadcasted_iota(jnp.int32, (1, LANES), 1)
    lo = lane < HEAD_DIM
    lane256 = lax.broadcasted_iota(jnp.int32, (1, W_GMLP), 1) // GMLP_GROUP_DIM
    row4 = lax.broadcasted_iota(jnp.int32, (4 * qn, 1), 0)
    sk = jnp.where(row4 < qn, sink_ref[layer, 0],
                   jnp.where(row4 < 2 * qn, sink_ref[layer, 3],
                             jnp.where(row4 < 3 * qn, sink_ref[layer, 1], sink_ref[layer, 2])))
    gi = lax.broadcasted_iota(jnp.int32, (gl, gl), 0)
    gj = lax.broadcasted_iota(jnp.int32, (gl, gl), 1)
    tri = (gj // CHUNK) <= (gi // CHUNK)
    wcat = jnp.concatenate([jnp.where(tri, gw_ref[g, 0:gl, 0:gl], 0.0).astype(jnp.bfloat16)
                            for g in range(GMLP_GROUPS)], axis=1)
    gmask = [(lane256 == g).astype(jnp.bfloat16) for g in range(GMLP_GROUPS)]
    bias_g = gb_ref[0:gl, :]
    scale = HEAD_DIM ** -0.5

    def subtile(r0, last):
        rows = slice(r0, r0 + sub)
        krows = slice(WINDOW + r0, WINDOW + r0 + sub)
        x = x_ref[rows, :]
        hb[rows, :] = _rmsnorm(x, gmix_ref[...]).astype(jnp.bfloat16)
        zq = _dot(hb[rows, :], win_ref[:, _Q0:_CB0])
        trows = pl.ds(pl.multiple_of(t * tm + r0, SUBLANES), sub)
        cos_t, sin_lo, sin_hi = cos_ref[trows, :], sinlo_ref[trows, :], sinhi_ref[trows, :]
        qa = _rope(zq[:, 0:128], cos_t, sin_lo, sin_hi) * scale
        qb = _rope(zq[:, 128:256], cos_t, sin_lo, sin_hi) * scale
        k = _rope(zq[:, 256:384], cos_t, sin_lo, sin_hi)
        v = zq[:, 384:512]
        kext[krows, :] = k.astype(jnp.bfloat16)
        vext[krows, :] = v.astype(jnp.bfloat16)
        kext2[krows, :] = _swap_halves(k).astype(jnp.bfloat16)
        vext2[krows, :] = _swap_halves(v).astype(jnp.bfloat16)
        if last:
            kst_ref[...] = k[sub - kr:sub, :]
            vst_ref[...] = v[sub - kr:sub, :]
        yield
        scores = []
        for b in range(sub // qn):
            q0 = b * qn
            wrows = slice(r0 + q0, r0 + q0 + nkeys)
            qa_b, qb_b = qa[q0:q0 + qn, :], qb[q0:q0 + qn, :]
            q03 = jnp.concatenate([jnp.where(lo, qa_b, 0.0), jnp.where(lo, 0.0, qb_b)], axis=0)
            q12 = jnp.concatenate([jnp.where(lo, 0.0, qa_b), jnp.where(lo, qb_b, 0.0)], axis=0)
            s = jnp.concatenate([_dot_nt(q03.astype(jnp.bfloat16), kext[wrows, :]),
                                 _dot_nt(q12.astype(jnp.bfloat16), kext2[wrows, :])], axis=0)
            if not has_past:
                s = s + (bias_ref[0] if r0 + q0 > 0 else bias_ref[jnp.where(t == 0, 1, 0)])
            scores.append(s)
        yield
        zc = _dot(hb[rows, :], win_ref[:, _CB0:_GU0])
        zg = _dot(hb[rows, :], win_ref[:, _GU0:_P0])
        zp = _dot(hb[rows, :], win_ref[:, _P0:PROJ_WIDTH])
        yield
        probs, dens = [], []
        for s in scores:
            m = jnp.maximum(jnp.max(s, axis=-1, keepdims=True), sk)
            p = jnp.exp(s - m)
            dens.append(jnp.sum(p, axis=-1, keepdims=True) + jnp.exp(sk - m))
            probs.append(p.astype(jnp.bfloat16))
        yield
        for b in range(sub // qn):
            q0 = r0 + b * qn
            wrows = slice(q0, q0 + nkeys)
            o03 = _dot(probs[b][0:2 * qn], vext[wrows, :]) / dens[b][0:2 * qn]
            o12 = _dot(probs[b][2 * qn:4 * qn], vext2[wrows, :]) / dens[b][2 * qn:4 * qn]
            oa = jnp.where(lo, o03[0:qn], o12[0:qn])
            ob = jnp.where(lo, o12[qn:2 * qn], o03[qn:2 * qn])
            mix[q0:q0 + qn, 0:128] = oa.astype(jnp.bfloat16)
            mix[q0:q0 + qn, 128:256] = ob.astype(jnp.bfloat16)
        yield
        mprod = zc[:, 256:512] * zc[:, 512:768]
        mext[SUBLANES + r0:SUBLANES + r0 + sub, :] = mprod
        cw = convw_ref[...]
        yc = (mext[SUBLANES - 2 + r0:SUBLANES - 2 + r0 + sub, :] * cw[0:1, :]
              + mext[SUBLANES - 1 + r0:SUBLANES - 1 + r0 + sub, :] * cw[1:2, :]
              + mprod * cw[2:3, :])
        mix[rows, 256:512] = (zc[:, 0:256] * yc).astype(jnp.bfloat16)
        if last:
            cst_ref[...] = mext[SUBLANES + tm - 2:SUBLANES + tm, :]

        u = jax.nn.gelu(zg[:, 0:256])
        gv = jax.nn.gelu(zg[:, 256:512])
        mu = jnp.mean(gv, axis=-1, keepdims=True)
        var = jnp.mean(jnp.square(gv - mu), axis=-1, keepdims=True)
        vn = (gv - mu) * lax.rsqrt(var + NORM_EPS) * lng_ref[...] + lnb_ref[...]
        if has_past:
            grows_ref[rows, :] = vn
        vnb = vn.astype(jnp.bfloat16)
        for c in range(sub // gl):
            c0 = c * gl
            vc = vnb[c0:c0 + gl, :]
            s = _dot(wcat, jnp.concatenate([vc * gmask[g] for g in range(GMLP_GROUPS)], axis=0))
            mix[r0 + c0:r0 + c0 + gl, 512:768] = (u[c0:c0 + gl, :] * (s + bias_g)).astype(jnp.bfloat16)

        pext[POOL_PAD + r0:POOL_PAD + r0 + sub, :] = zp
        n = POOL_PAD - SUBLANES + sub
        lo_r = SUBLANES + r0
        pa[lo_r:lo_r + n, :] = pext[lo_r:lo_r + n, :] + pext[lo_r - 1:lo_r - 1 + n, :]
        pb[lo_r:lo_r + n, :] = pa[lo_r:lo_r + n, :] + pa[lo_r - 2:lo_r - 2 + n, :]
        tr = slice(POOL_PAD + r0, POOL_PAD + r0 + sub)
        s2 = pa[tr, 0:LANES]
        s4 = pb[tr, 0:LANES]
        pa[lo_r:lo_r + n, LANES:] = pb[lo_r:lo_r + n, LANES:] + pb[lo_r - 4:lo_r - 4 + n, LANES:]
        s8 = pa[tr, LANES:]
        s16 = s8 + pa[POOL_PAD - 8 + r0:POOL_PAD - 8 + r0 + sub, LANES:]
        wsum = jnp.concatenate([jnp.where(lo, s2, s4), jnp.where(lo, s8, s16)], axis=1)
        prow = lax.broadcasted_iota(jnp.int32, (sub, 1), 0) + (pos0 + 1 + r0) + t * tm
        winl = jnp.left_shift(2, lane256).astype(jnp.float32)
        cnt = jnp.minimum(prow.astype(jnp.float32), winl)
        pooled = wsum / cnt - zp
        mix[rows, 768:1024] = (_dot(pooled.astype(jnp.bfloat16), pw_ref[...]) * pscale_ref[...]).astype(jnp.bfloat16)
        if last:
            pst_ref[...] = pext[POOL_PAD + tm - POOL_HIST:POOL_PAD + tm, :]
        yield
        xo_ref[rows, :] = x_ref[rows, :] + _dot(mix[rows, :], wout_ref[...])
        yield

    nsub = tm // sub
    gens = [subtile(i * sub, i == nsub - 1) for i in range(nsub)]
    n_stages = 7
    for step in range(n_stages + MIXER_STAGGER * (nsub - 1)):
        for i, g in enumerate(gens):
            if 0 <= step - MIXER_STAGGER * i < n_stages:
                next(g)

    if nt > 1:
        for buf in (kext, vext, kext2, vext2):
            buf[0:WINDOW, :] = buf[tm:tm + WINDOW, :]
        mext[0:SUBLANES, :] = mext[tm:tm + SUBLANES, :]
        pext[0:POOL_PAD, :] = pext[tm:tm + POOL_PAD, :]


def _const_spec(shape):
    nd = len(shape)
    return pl.BlockSpec(shape, lambda b, t, _nd=nd: (0,) * _nd, pipeline_mode=pl.Buffered(1))


def _layer_spec(stacked_shape, l):
    nd = len(stacked_shape) - 1
    return pl.BlockSpec((None,) + tuple(stacked_shape[1:]), lambda b, t, _nd=nd, _l=l: (_l,) + (0,) * _nd,
                        pipeline_mode=pl.Buffered(1))


def _window_bias(qn):
    rows = (np.arange(4 * qn) % qn) // CHUNK
    keys = np.arange(WINDOW + qn)
    ok = (keys[None, :] >= rows[:, None] * CHUNK) & (keys[None, :] < (rows[:, None] + 3) * CHUNK)
    first = ok & (keys[None, :] >= WINDOW)
    return np.where(np.stack([ok, first]), 0.0, -np.inf).astype(np.float32)


def _mixer_call(x, tabs, past, lw, l, *, tm, pos0):
    bsz, t_len, _ = x.shape
    nt = t_len // tm
    has_past = past is not None
    kr = min(WINDOW, tm)
    qn = min(ATTN_BLOCK, tm)
    in_specs = [pl.BlockSpec((None, tm, D_MODEL), lambda b, t: (b, t, 0))]
    in_specs += [_const_spec(tabs[0].shape)] * 3
    args = [x, *tabs]
    if has_past:
        for a in past:
            in_specs.append(pl.BlockSpec((None, None) + a.shape[2:], lambda b, t, _l=l: (b, _l, 0, 0)))
            args.append(a)
    else:
        bias = jnp.asarray(_window_bias(qn))
        in_specs.append(_const_spec(bias.shape))
        args.append(bias)
    wnames = ('g_mix', 'w_in', 'sink', 'conv_w', 'ln_g', 'ln_b', 'gmlp_w', 'gmlp_b', 'pool_w', 'pool_scale', 'w_out')
    for n in wnames:
        a = lw[n]
        if n == 'sink':
            in_specs.append(pl.BlockSpec(memory_space=pltpu.SMEM))
        else:
            in_specs.append(_layer_spec(a.shape, l))
        args.append(a)
    out_shape = [jax.ShapeDtypeStruct((bsz, t_len, D_MODEL), jnp.float32),
                 jax.ShapeDtypeStruct((bsz, kr, KV_WIDTH), jnp.float32),
                 jax.ShapeDtypeStruct((bsz, kr, KV_WIDTH), jnp.float32),
                 jax.ShapeDtypeStruct((bsz, CONV_WIDTH - 1, W_CONV), jnp.float32),
                 jax.ShapeDtypeStruct((bsz, POOL_HIST, W_POOL), jnp.float32)]
    out_specs = [pl.BlockSpec((None, tm, D_MODEL), lambda b, t: (b, t, 0)),
                 pl.BlockSpec((None, kr, KV_WIDTH), lambda b, t: (b, 0, 0)),
                 pl.BlockSpec((None, kr, KV_WIDTH), lambda b, t: (b, 0, 0)),
                 pl.BlockSpec((None, CONV_WIDTH - 1, W_CONV), lambda b, t: (b, 0, 0)),
                 pl.BlockSpec((None, POOL_HIST, W_POOL), lambda b, t: (b, 0, 0))]
    if has_past:
        out_shape.append(jax.ShapeDtypeStruct((bsz, t_len, W_GMLP), jnp.float32))
        out_specs.append(pl.BlockSpec((None, tm, W_GMLP), lambda b, t: (b, t, 0)))
    scratch = [pltpu.VMEM((WINDOW + tm, KV_WIDTH), jnp.bfloat16)] * 4
    scratch += [pltpu.VMEM((SUBLANES + tm, W_CONV), jnp.float32),
                pltpu.VMEM((POOL_PAD + tm, W_POOL), jnp.float32),
                pltpu.VMEM((POOL_PAD + tm, W_POOL), jnp.float32),
                pltpu.VMEM((POOL_PAD + tm, W_POOL), jnp.float32),
                pltpu.VMEM((tm, D_MODEL), jnp.bfloat16),
                pltpu.VMEM((tm, D_MODEL), jnp.bfloat16)]
    return pl.pallas_call(
        functools.partial(_mixer_kernel, tm=tm, nt=nt, has_past=has_past, pos0=pos0, layer=l),
        grid=(bsz, nt),
        in_specs=in_specs,
        out_specs=out_specs,
        out_shape=out_shape,
        scratch_shapes=scratch,
        compiler_params=pltpu.CompilerParams(
            dimension_semantics=("arbitrary", "arbitrary"), vmem_limit_bytes=VMEM_LIMIT),
        name="mixer_past" if has_past else "mixer",
    )(*args)


def _ffn_kernel(*refs, tm, nt, nsteps, chained, final):
    it = iter(refs)
    x_ref = next(it)
    if not chained:
        past_ref = next(it)
    gffn_ref, wup_ref, fcw_ref, wdown_ref = next(it), next(it), next(it), next(it)
    if final:
        gfin_ref = next(it)
    xo_ref, fst_ref = next(it), next(it)
    carry, hbuf, act = next(it), next(it), next(it)
    if chained:
        xbuf, obuf, sem_in, sem_out = next(it), next(it), next(it), next(it)

    t = pl.program_id(1)
    fc = FF_CHUNK
    halves = (0, D_FF)
    seg = tm // SUBLANES

    if chained:
        step = pl.program_id(0) * nt + t
        slot = step % 2

        def tile_copies(tile, sl, outward):
            tb, row0 = tile // nt, (tile % nt) * tm
            cps = []
            for s in range(SUBLANES):
                hbm = (xo_ref if outward else x_ref).at[tb, pl.ds(row0 + s * seg, seg), :]
                if outward:
                    cps.append(pltpu.make_async_copy(obuf.at[sl, :, s, :], hbm, sem_out.at[sl, s]))
                else:
                    cps.append(pltpu.make_async_copy(hbm, xbuf.at[sl, :, s, :], sem_in.at[sl, s]))
            return cps

        @pl.when(step == 0)
        def _first_in():
            for cp in tile_copies(0, 0, False):
                cp.start()

        @pl.when(step + 1 < nsteps)
        def _next_in():
            for cp in tile_copies(step + 1, 1 - slot, False):
                cp.start()

        @pl.when(step >= 2)
        def _free_out():
            for cp in tile_copies(step - 2, slot, True):
                cp.wait()

        for cp in tile_copies(step, slot, False):
            cp.wait()

        @pl.when(t == 0)
        def _init():
            carry[...] = jnp.zeros(carry.shape, jnp.float32)

        xp = xbuf[slot].reshape(tm, D_MODEL)
    else:
        xp = jnp.swapaxes(x_ref[...].reshape(SUBLANES, seg, D_MODEL), 0, 1).reshape(tm, D_MODEL)
    hbuf[...] = _rmsnorm(xp, gffn_ref[...]).astype(jnp.bfloat16)
    sub = lax.broadcasted_iota(jnp.int32, (SUBLANES, fc), 0)
    for c in range(N_FF_CHUNKS):
        ys = []
        for hi, off in enumerate(halves):
            cols = slice(off + c * fc, off + (c + 1) * fc)
            lanes = slice(hi * fc, (hi + 1) * fc)
            up = _dot(hbuf[...], wup_ref[:, cols])
            last1, last2 = up[tm - SUBLANES:tm], up[tm - 2 * SUBLANES:tm - SUBLANES]
            if chained:
                p1 = jnp.where(sub == 0, pltpu.roll(carry[c, 1, :, lanes], 1, axis=0), pltpu.roll(last1, 1, axis=0))
                p2 = jnp.where(sub == 0, pltpu.roll(carry[c, 0, :, lanes], 1, axis=0), pltpu.roll(last2, 1, axis=0))
                fst_ref[0:1, cols] = last2[SUBLANES - 1:SUBLANES]
                fst_ref[1:2, cols] = last1[SUBLANES - 1:SUBLANES]
                if nt > 1:
                    carry[c, 0, :, lanes] = last2
                    carry[c, 1, :, lanes] = last1
            else:
                p2, p1 = past_ref[0, :, cols], past_ref[1, :, cols]
                fst_ref[0, :, cols] = last2
                fst_ref[1, :, cols] = last1
            d1 = jnp.concatenate([p1, up[0:tm - SUBLANES]], axis=0)
            d2 = jnp.concatenate([p2, p1, up[0:tm - 2 * SUBLANES]], axis=0)
            cw = fcw_ref[:, cols]
            ys.append(d2 * cw[0:1, :] + d1 * cw[1:2, :] + up * cw[2:3, :])
        gate, val = ys
        act[:, c * fc:(c + 1) * fc] = (gate * jax.nn.sigmoid(gate) * val).astype(jnp.bfloat16)
    down = _dot(act[...], wdown_ref[...])
    if chained:
        out = xbuf[slot].reshape(tm, D_MODEL) + down
    else:
        out = xp + down
    if final:
        out = _rmsnorm(out, gfin_ref[...])
    if chained:
        obuf[slot] = out.reshape(seg, SUBLANES, D_MODEL)
        for cp in tile_copies(step, slot, True):
            cp.start()

        @pl.when(step == nsteps - 1)
        def _drain():
            if nsteps > 1:
                for cp in tile_copies(step - 1, 1 - slot, True):
                    cp.wait()
            for cp in tile_copies(step, slot, True):
                cp.wait()
    else:
        xo_ref[...] = jnp.swapaxes(out.reshape(seg, SUBLANES, D_MODEL), 0, 1).reshape(tm, D_MODEL)


def _ffn_call(x, fpast, lw, l, g_final, *, tm):
    bsz, t_len, _ = x.shape
    chained = fpast is None
    final = g_final is not None
    args = []
    if chained:
        nt = t_len // tm
        grid = (bsz, nt)
        st_shape = (bsz, CONV_WIDTH - 1, 2 * D_FF)
        st_spec = pl.BlockSpec((None, CONV_WIDTH - 1, 2 * D_FF), lambda b, t: (b, 0, 0))
        in_specs = [pl.BlockSpec(memory_space=pl.ANY)]
        x_out_spec = pl.BlockSpec(memory_space=pl.ANY)
    else:
        assert bsz == SUBLANES and t_len % SUBLANES == 0
        x = x.reshape(1, bsz * t_len, D_MODEL)
        tm, nt, grid = bsz * t_len, 1, (1, 1)
        st_shape = (CONV_WIDTH - 1, bsz, 2 * D_FF)
        st_spec = pl.BlockSpec(st_shape, lambda b, t: (0, 0, 0))
        in_specs = [pl.BlockSpec((None, tm, D_MODEL), lambda b, t: (b, t, 0)),
                    pl.BlockSpec((None,) + st_shape, lambda b, t, _l=l: (_l, 0, 0, 0))]
        x_out_spec = pl.BlockSpec((None, tm, D_MODEL), lambda b, t: (b, t, 0))
        args.append(fpast)
    assert tm % (SUBLANES * SUBLANES) == 0
    seg = tm // SUBLANES
    args = [x] + args
    for n in ('g_ffn', 'w_up', 'ffn_conv_w', 'w_down'):
        in_specs.append(_layer_spec(lw[n].shape, l))
        args.append(lw[n])
    if final:
        in_specs.append(_const_spec(g_final.shape))
        args.append(g_final)
    out_shape = [jax.ShapeDtypeStruct(x.shape, jnp.float32), jax.ShapeDtypeStruct(st_shape, jnp.float32)]
    out_specs = [x_out_spec, st_spec]
    scratch = [pltpu.VMEM((N_FF_CHUNKS, CONV_WIDTH - 1, SUBLANES, 2 * FF_CHUNK), jnp.float32),
               pltpu.VMEM((tm, D_MODEL), jnp.bfloat16),
               pltpu.VMEM((tm, D_FF), jnp.bfloat16)]
    if chained:
        scratch += [pltpu.VMEM((2, seg, SUBLANES, D_MODEL), jnp.float32),
                    pltpu.VMEM((2, seg, SUBLANES, D_MODEL), jnp.float32),
                    pltpu.SemaphoreType.DMA((2, SUBLANES)),
                    pltpu.SemaphoreType.DMA((2, SUBLANES))]
    y, fst = pl.pallas_call(
        functools.partial(_ffn_kernel, tm=tm, nt=nt, nsteps=grid[0] * grid[1], chained=chained, final=final),
        grid=grid,
        in_specs=in_specs,
        out_specs=out_specs,
        out_shape=out_shape,
        scratch_shapes=scratch,
        compiler_params=pltpu.CompilerParams(
            dimension_semantics=("arbitrary", "arbitrary"), vmem_limit_bytes=VMEM_LIMIT),
        name=("ffn" if chained else "ffn_past") + ("_final" if final else ""),
    )(*args)
    if not chained:
        y, fst = y.reshape(bsz, t_len, D_MODEL), jnp.swapaxes(fst, 0, 1)
    return y, fst


def _rope_tables(pos0, t_len):
    half = ROT_DIM // 2
    inv_freq = np.power(np.float64(ROPE_THETA), -np.arange(half, dtype=np.float64) * (2.0 / ROT_DIM))
    ang = (pos0 + np.arange(t_len, dtype=np.float64))[:, None] * inv_freq[None, :]
    cos, sin = np.cos(ang), np.sin(ang)
    rest = HEAD_DIM - ROT_DIM
    cos_h = np.concatenate([cos, cos, np.ones((t_len, rest))], axis=1)
    sinlo_h = np.concatenate([-sin, np.zeros((t_len, half + rest))], axis=1)
    sinhi_h = np.concatenate([np.zeros((t_len, half)), sin, np.zeros((t_len, rest))], axis=1)
    return tuple(jnp.asarray(np.concatenate([a, a], axis=1), dtype=jnp.float32) for a in (cos_h, sinlo_h, sinhi_h))


def _stacked_weights(g_mix, w_in, attn_sink, conv_w, gmlp_ln_g, gmlp_ln_b, gmlp_w, gmlp_b, pool_w, pool_scale,
                     w_out, g_ffn, w_up, ffn_conv_w, w_down):
    eye = jnp.eye(len(POOL_WINDOWS), dtype=pool_w.dtype)
    pw = jnp.einsum('lgij,gh->lgihj', pool_w, eye).reshape(DEPTH, W_POOL, W_POOL)
    return {
        'g_mix': g_mix[:, None, :],
        'w_in': w_in.astype(jnp.bfloat16),
        'sink': attn_sink,
        'conv_w': conv_w,
        'ln_g': gmlp_ln_g[:, None, :],
        'ln_b': gmlp_ln_b[:, None, :],
        'gmlp_w': gmlp_w,
        'gmlp_b': jnp.repeat(jnp.swapaxes(gmlp_b, 1, 2), GMLP_GROUP_DIM, axis=2),
        'pool_w': pw.astype(jnp.bfloat16),
        'pool_scale': pool_scale[:, None, :],
        'w_out': w_out.astype(jnp.bfloat16),
        'g_ffn': g_ffn[:, None, :],
        'w_up': w_up.astype(jnp.bfloat16),
        'ffn_conv_w': ffn_conv_w,
        'w_down': w_down.astype(jnp.bfloat16),
    }


def _run(x, pos0, pasts, lw, g_final, tm):
    bsz, t_len, _ = x.shape
    tabs = _rope_tables(pos0, t_len)
    if pasts is not None:
        ck, cv, sc, sp, sf = pasts
        mpast = (ck.reshape(bsz, DEPTH, WINDOW, KV_WIDTH), cv.reshape(bsz, DEPTH, WINDOW, KV_WIDTH), sc, sp)
        fpast = jnp.transpose(sf, (1, 2, 0, 3))
    else:
        mpast, fpast = None, None
    states = []
    for l in range(DEPTH):
        mo = _mixer_call(x, tabs, mpast, lw, l, tm=tm, pos0=pos0)
        x_mid, kst, vst, cst, pst = mo[:5]
        gfin = g_final[None, :] if l == DEPTH - 1 else None
        x, fst = _ffn_call(x_mid, fpast, lw, l, gfin, tm=tm)
        st = [kst.reshape(bsz, -1, N_KV_HEADS, HEAD_DIM), vst.reshape(bsz, -1, N_KV_HEADS, HEAD_DIM),
              cst, pst, fst]
        if pasts is not None:
            st.append(mo[5])
        states.append(st)
    stacked = [jnp.stack([st[i] for st in states], axis=1) for i in range(len(states[0]))]
    return x, stacked


def kernel(x_prompt, x_sample, cache_attn_k, cache_attn_v, state_conv, state_pool, state_ffn_conv, g_mix, w_in,
           attn_sink, conv_w, gmlp_ln_g, gmlp_ln_b, gmlp_w, gmlp_b, pool_w, pool_scale, w_out, g_ffn, w_up,
           ffn_conv_w, w_down, g_final):
    lw = _stacked_weights(g_mix, w_in, attn_sink, conv_w, gmlp_ln_g, gmlp_ln_b, gmlp_w, gmlp_b, pool_w,
                          pool_scale, w_out, g_ffn, w_up, ffn_conv_w, w_down)
    y_prompt, sp = _run(x_prompt, 0, None, lw, g_final, tm=1024)
    y_sample, ss = _run(x_sample, PAST_LEN, (cache_attn_k, cache_attn_v, state_conv, state_pool, state_ffn_conv),
                        lw, g_final, tm=x_sample.shape[1])
    return (y_prompt, y_sample, sp[0], sp[1], sp[2], sp[3], sp[4], ss[0], ss[1], ss[2], ss[3], ss[4], ss[5])
```

```python
import functools

import numpy as np
import jax
import jax.numpy as jnp
from jax import lax
from jax.experimental import pallas as pl
from jax.experimental.pallas import tpu as pltpu

D_MODEL = 1024
DEPTH = 2
CHUNK = 64
HEAD_DIM = 64
N_Q_HEADS = 4
N_KV_HEADS = 2
KV_WIDTH = N_KV_HEADS * HEAD_DIM
W_ATTN = 256
W_CONV = 256
W_GMLP = 256
W_POOL = 256
WINDOW = 128
ROPE_THETA = 500000.0
ROT_DIM = 16
CONV_WIDTH = 3
GMLP_CHUNK = 128
GMLP_GROUPS = 4
GMLP_GROUP_DIM = 64
POOL_WINDOWS = (2, 4, 8, 16)
POOL_HIST = 15
D_FF = 2816
NORM_EPS = 1e-6
PROJ_WIDTH = 2048
PAST_LEN = 2048

LANES = 128
SUBLANES = 8
FF_CHUNK = 256
N_FF_CHUNKS = D_FF // FF_CHUNK
POOL_PAD = 24
ATTN_BLOCK = 2 * CHUNK
MIXER_SUB = 256
MIXER_STAGGER = 0
VMEM_LIMIT = 56 * 1024 * 1024

_Q0, _K0, _V0 = 0, 256, 384
_CB0, _CC0, _CH0 = 512, 768, 1024
_GU0, _GV0 = 1280, 1536
_P0 = 1792


def _rmsnorm(x, g):
    ms = jnp.mean(x * x, axis=-1, keepdims=True)
    return x * lax.rsqrt(ms + NORM_EPS) * g


def _dot(a, b):
    return jnp.dot(a, b, preferred_element_type=jnp.float32)


def _dot_nt(a, b):
    return lax.dot_general(a, b, (((1,), (1,)), ((), ())), preferred_element_type=jnp.float32)


def _rope(x, cos_t, sin_lo, sin_hi):
    up = pltpu.roll(x, LANES - ROT_DIM // 2, axis=1)
    dn = pltpu.roll(x, ROT_DIM // 2, axis=1)
    return x * cos_t + up * sin_lo + dn * sin_hi


def _swap_halves(x):
    return pltpu.roll(x, HEAD_DIM, axis=1)


def _swap_row_halves(x):
    return jnp.concatenate([x[HEAD_DIM:], x[:HEAD_DIM]], axis=0)


def _mixer_kernel(*refs, tm, nt, has_past, pos0, layer):
    it = iter(refs)
    x_ref = next(it)
    cos_ref, sinlo_ref, sinhi_ref = next(it), next(it), next(it)
    if has_past:
        kpast_ref, vpast_ref, cpast_ref, ppast_ref = next(it), next(it), next(it), next(it)
    else:
        bias_ref = next(it)
    gmix_ref, win_ref, sink_ref, convw_ref = next(it), next(it), next(it), next(it)
    lng_ref, lnb_ref, gw_ref, gb_ref = next(it), next(it), next(it), next(it)
    pw_ref, pscale_ref, wout_ref = next(it), next(it), next(it)
    xo_ref, kst_ref, vst_ref, cst_ref, pst_ref = next(it), next(it), next(it), next(it), next(it)
    if has_past:
        grows_ref = next(it)
    kt, vext, kt2, vext2 = next(it), next(it), next(it), next(it)
    mext, pext, pa, pb, hb, mix = next(it), next(it), next(it), next(it), next(it), next(it)

    t = pl.program_id(1)
    qn = min(ATTN_BLOCK, tm)
    nkeys = WINDOW + qn
    kr = min(WINDOW, tm)
    gl = min(GMLP_CHUNK, tm)
    sub = min(MIXER_SUB, tm)

    @pl.when(t == 0)
    def _init():
        mext[0:SUBLANES, :] = jnp.zeros((SUBLANES, W_CONV), jnp.float32)
        pext[0:POOL_PAD, :] = jnp.zeros((POOL_PAD, W_POOL), jnp.float32)
        pa[0:SUBLANES, :] = jnp.zeros((SUBLANES, W_POOL), jnp.float32)
        pb[0:SUBLANES, :] = jnp.zeros((SUBLANES, W_POOL), jnp.float32)
        if has_past:
            kp, vp = kpast_ref[...], vpast_ref[...]
            kpt = kp.T.astype(jnp.bfloat16)
            kt[:, 0:WINDOW] = kpt
            kt2[:, 0:WINDOW] = _swap_row_halves(kpt)
            vext[0:WINDOW, :] = vp.astype(jnp.bfloat16)
            vext2[0:WINDOW, :] = _swap_halves(vp).astype(jnp.bfloat16)
            mext[SUBLANES - (CONV_WIDTH - 1):SUBLANES, :] = cpast_ref[...]
            pext[POOL_PAD - POOL_HIST:POOL_PAD, :] = ppast_ref[...]
        else:
            for buf in (vext, vext2):
                buf[0:WINDOW, :] = jnp.zeros((WINDOW, KV_WIDTH), jnp.bfloat16)
            for buf in (kt, kt2):
                buf[:, 0:WINDOW] = jnp.zeros((KV_WIDTH, WINDOW), jnp.bfloat16)

    lane = lax.broadcasted_iota(jnp.int32, (1, LANES), 1)
    lo = lane < HEAD_DIM
    lane256 = lax.broadcasted_iota(jnp.int32, (1, W_GMLP), 1) // GMLP_GROUP_DIM
    row4 = lax.broadcasted_iota(jnp.int32, (4 * qn, 1), 0)
    sk = jnp.where(row4 < qn, sink_ref[layer, 0],
                   jnp.where(row4 < 2 * qn, sink_ref[layer, 3],
                             jnp.where(row4 < 3 * qn, sink_ref[layer, 1], sink_ref[layer, 2])))
    gi = lax.broadcasted_iota(jnp.int32, (gl, gl), 0)
    gj = lax.broadcasted_iota(jnp.int32, (gl, gl), 1)
    tri = (gj // CHUNK) <= (gi // CHUNK)
    wcat = jnp.concatenate([jnp.where(tri, gw_ref[g, 0:gl, 0:gl], 0.0).astype(jnp.bfloat16)
                            for g in range(GMLP_GROUPS)], axis=1)
    gmask = [(lane256 == g).astype(jnp.bfloat16) for g in range(GMLP_GROUPS)]
    bias_g = gb_ref[0:gl, :]
    scale = HEAD_DIM ** -0.5

    def subtile(r0, last):
        rows = slice(r0, r0 + sub)
        krows = slice(WINDOW + r0, WINDOW + r0 + sub)
        x = x_ref[rows, :]
        hb[rows, :] = _rmsnorm(x, gmix_ref[...]).astype(jnp.bfloat16)
        zq = _dot(hb[rows, :], win_ref[:, _Q0:_CB0])
        trows = pl.ds(pl.multiple_of(t * tm + r0, SUBLANES), sub)
        cos_t, sin_lo, sin_hi = cos_ref[trows, :], sinlo_ref[trows, :], sinhi_ref[trows, :]
        qa = _rope(zq[:, 0:128], cos_t, sin_lo, sin_hi) * scale
        qb = _rope(zq[:, 128:256], cos_t, sin_lo, sin_hi) * scale
        k = _rope(zq[:, 256:384], cos_t, sin_lo, sin_hi)
        v = zq[:, 384:512]
        ktb = k.T.astype(jnp.bfloat16)
        kt[:, krows] = ktb
        kt2[:, krows] = _swap_row_halves(ktb)
        vext[krows, :] = v.astype(jnp.bfloat16)
        vext2[krows, :] = _swap_halves(v).astype(jnp.bfloat16)
        if last:
            kst_ref[...] = k[sub - kr:sub, :]
            vst_ref[...] = v[sub - kr:sub, :]
        yield
        scores = []
        for b in range(sub // qn):
            q0 = b * qn
            wrows = slice(r0 + q0, r0 + q0 + nkeys)
            qa_b, qb_b = qa[q0:q0 + qn, :], qb[q0:q0 + qn, :]
            q03 = jnp.concatenate([jnp.where(lo, qa_b, 0.0), jnp.where(lo, 0.0, qb_b)], axis=0)
            q12 = jnp.concatenate([jnp.where(lo, 0.0, qa_b), jnp.where(lo, qb_b, 0.0)], axis=0)
            s = jnp.concatenate([_dot(q03.astype(jnp.bfloat16), kt[:, wrows]),
                                 _dot(q12.astype(jnp.bfloat16), kt2[:, wrows])], axis=0)
            if not has_past:
                s = s + (bias_ref[0] if r0 + q0 > 0 else bias_ref[jnp.where(t == 0, 1, 0)])
            scores.append(s)
        yield
        zc = _dot(hb[rows, :], win_ref[:, _CB0:_GU0])
        zg = _dot(hb[rows, :], win_ref[:, _GU0:_P0])
        zp = _dot(hb[rows, :], win_ref[:, _P0:PROJ_WIDTH])
        yield
        probs, dens = [], []
        for s in scores:
            m = jnp.maximum(jnp.max(s, axis=-1, keepdims=True), sk)
            p = jnp.exp(s - m)
            dens.append(jnp.sum(p, axis=-1, keepdims=True) + jnp.exp(sk - m))
            probs.append(p.astype(jnp.bfloat16))
        yield
        for b in range(sub // qn):
            q0 = r0 + b * qn
            wrows = slice(q0, q0 + nkeys)
            o03 = _dot(probs[b][0:2 * qn], vext[wrows, :]) / dens[b][0:2 * qn]
            o12 = _dot(probs[b][2 * qn:4 * qn], vext2[wrows, :]) / dens[b][2 * qn:4 * qn]
            oa = jnp.where(lo, o03[0:qn], o12[0:qn])
            ob = jnp.where(lo, o12[qn:2 * qn], o03[qn:2 * qn])
            mix[q0:q0 + qn, 0:128] = oa.astype(jnp.bfloat16)
            mix[q0:q0 + qn, 128:256] = ob.astype(jnp.bfloat16)
        yield
        mprod = zc[:, 256:512] * zc[:, 512:768]
        mext[SUBLANES + r0:SUBLANES + r0 + sub, :] = mprod
        cw = convw_ref[...]
        yc = (mext[SUBLANES - 2 + r0:SUBLANES - 2 + r0 + sub, :] * cw[0:1, :]
              + mext[SUBLANES - 1 + r0:SUBLANES - 1 + r0 + sub, :] * cw[1:2, :]
              + mprod * cw[2:3, :])
        mix[rows, 256:512] = (zc[:, 0:256] * yc).astype(jnp.bfloat16)
        if last:
            cst_ref[...] = mext[SUBLANES + tm - 2:SUBLANES + tm, :]

        u = jax.nn.gelu(zg[:, 0:256])
        gv = jax.nn.gelu(zg[:, 256:512])
        mu = jnp.mean(gv, axis=-1, keepdims=True)
        var = jnp.mean(jnp.square(gv - mu), axis=-1, keepdims=True)
        vn = (gv - mu) * lax.rsqrt(var + NORM_EPS) * lng_ref[...] + lnb_ref[...]
        if has_past:
            grows_ref[rows, :] = vn
        vnb = vn.astype(jnp.bfloat16)
        for c in range(sub // gl):
            c0 = c * gl
            vc = vnb[c0:c0 + gl, :]
            s = _dot(wcat, jnp.concatenate([vc * gmask[g] for g in range(GMLP_GROUPS)], axis=0))
            mix[r0 + c0:r0 + c0 + gl, 512:768] = (u[c0:c0 + gl, :] * (s + bias_g)).astype(jnp.bfloat16)

        pext[POOL_PAD + r0:POOL_PAD + r0 + sub, :] = zp
        n = POOL_PAD - SUBLANES + sub
        lo_r = SUBLANES + r0
        pa[lo_r:lo_r + n, :] = pext[lo_r:lo_r + n, :] + pext[lo_r - 1:lo_r - 1 + n, :]
        pb[lo_r:lo_r + n, :] = pa[lo_r:lo_r + n, :] + pa[lo_r - 2:lo_r - 2 + n, :]
        tr = slice(POOL_PAD + r0, POOL_PAD + r0 + sub)
        s2 = pa[tr, 0:LANES]
        s4 = pb[tr, 0:LANES]
        pa[lo_r:lo_r + n, LANES:] = pb[lo_r:lo_r + n, LANES:] + pb[lo_r - 4:lo_r - 4 + n, LANES:]
        s8 = pa[tr, LANES:]
        s16 = s8 + pa[POOL_PAD - 8 + r0:POOL_PAD - 8 + r0 + sub, LANES:]
        wsum = jnp.concatenate([jnp.where(lo, s2, s4), jnp.where(lo, s8, s16)], axis=1)
        prow = lax.broadcasted_iota(jnp.int32, (sub, 1), 0) + (pos0 + 1 + r0) + t * tm
        winl = jnp.left_shift(2, lane256).astype(jnp.float32)
        cnt = jnp.minimum(prow.astype(jnp.float32), winl)
        pooled = wsum / cnt - zp
        mix[rows, 768:1024] = (_dot(pooled.astype(jnp.bfloat16), pw_ref[...]) * pscale_ref[...]).astype(jnp.bfloat16)
        if last:
            pst_ref[...] = pext[POOL_PAD + tm - POOL_HIST:POOL_PAD + tm, :]
        yield
        xo_ref[rows, :] = x_ref[rows, :] + _dot(mix[rows, :], wout_ref[...])
        yield

    nsub = tm // sub
    gens = [subtile(i * sub, i == nsub - 1) for i in range(nsub)]
    n_stages = 7
    for step in range(n_stages + MIXER_STAGGER * (nsub - 1)):
        for i, g in enumerate(gens):
            if 0 <= step - MIXER_STAGGER * i < n_stages:
                next(g)

    if nt > 1:
        for buf in (vext, vext2):
            buf[0:WINDOW, :] = buf[tm:tm + WINDOW, :]
        for buf in (kt, kt2):
            buf[:, 0:WINDOW] = buf[:, tm:tm + WINDOW]
        mext[0:SUBLANES, :] = mext[tm:tm + SUBLANES, :]
        pext[0:POOL_PAD, :] = pext[tm:tm + POOL_PAD, :]


def _const_spec(shape):
    nd = len(shape)
    return pl.BlockSpec(shape, lambda b, t, _nd=nd: (0,) * _nd, pipeline_mode=pl.Buffered(1))


def _layer_spec(stacked_shape, l):
    nd = len(stacked_shape) - 1
    return pl.BlockSpec((None,) + tuple(stacked_shape[1:]), lambda b, t, _nd=nd, _l=l: (_l,) + (0,) * _nd,
                        pipeline_mode=pl.Buffered(1))


def _window_bias(qn):
    rows = (np.arange(4 * qn) % qn) // CHUNK
    keys = np.arange(WINDOW + qn)
    ok = (keys[None, :] >= rows[:, None] * CHUNK) & (keys[None, :] < (rows[:, None] + 3) * CHUNK)
    first = ok & (keys[None, :] >= WINDOW)
    return np.where(np.stack([ok, first]), 0.0, -np.inf).astype(np.float32)


def _mixer_call(x, tabs, past, lw, l, *, tm, pos0):
    bsz, t_len, _ = x.shape
    nt = t_len // tm
    has_past = past is not None
    kr = min(WINDOW, tm)
    qn = min(ATTN_BLOCK, tm)
    in_specs = [pl.BlockSpec((None, tm, D_MODEL), lambda b, t: (b, t, 0))]
    in_specs += [_const_spec(tabs[0].shape)] * 3
    args = [x, *tabs]
    if has_past:
        for a in past:
            in_specs.append(pl.BlockSpec((None, None) + a.shape[2:], lambda b, t, _l=l: (b, _l, 0, 0)))
            args.append(a)
    else:
        bias = jnp.asarray(_window_bias(qn))
        in_specs.append(_const_spec(bias.shape))
        args.append(bias)
    wnames = ('g_mix', 'w_in', 'sink', 'conv_w', 'ln_g', 'ln_b', 'gmlp_w', 'gmlp_b', 'pool_w', 'pool_scale', 'w_out')
    for n in wnames:
        a = lw[n]
        if n == 'sink':
            in_specs.append(pl.BlockSpec(memory_space=pltpu.SMEM))
        else:
            in_specs.append(_layer_spec(a.shape, l))
        args.append(a)
    out_shape = [jax.ShapeDtypeStruct((bsz, t_len, D_MODEL), jnp.float32),
                 jax.ShapeDtypeStruct((bsz, kr, KV_WIDTH), jnp.float32),
                 jax.ShapeDtypeStruct((bsz, kr, KV_WIDTH), jnp.float32),
                 jax.ShapeDtypeStruct((bsz, CONV_WIDTH - 1, W_CONV), jnp.float32),
                 jax.ShapeDtypeStruct((bsz, POOL_HIST, W_POOL), jnp.float32)]
    out_specs = [pl.BlockSpec((None, tm, D_MODEL), lambda b, t: (b, t, 0)),
                 pl.BlockSpec((None, kr, KV_WIDTH), lambda b, t: (b, 0, 0)),
                 pl.BlockSpec((None, kr, KV_WIDTH), lambda b, t: (b, 0, 0)),
                 pl.BlockSpec((None, CONV_WIDTH - 1, W_CONV), lambda b, t: (b, 0, 0)),
                 pl.BlockSpec((None, POOL_HIST, W_POOL), lambda b, t: (b, 0, 0))]
    if has_past:
        out_shape.append(jax.ShapeDtypeStruct((bsz, t_len, W_GMLP), jnp.float32))
        out_specs.append(pl.BlockSpec((None, tm, W_GMLP), lambda b, t: (b, t, 0)))
    scratch = [pltpu.VMEM((KV_WIDTH, WINDOW + tm), jnp.bfloat16), pltpu.VMEM((WINDOW + tm, KV_WIDTH), jnp.bfloat16)] * 2
    scratch += [pltpu.VMEM((SUBLANES + tm, W_CONV), jnp.float32),
                pltpu.VMEM((POOL_PAD + tm, W_POOL), jnp.float32),
                pltpu.VMEM((POOL_PAD + tm, W_POOL), jnp.float32),
                pltpu.VMEM((POOL_PAD + tm, W_POOL), jnp.float32),
                pltpu.VMEM((tm, D_MODEL), jnp.bfloat16),
                pltpu.VMEM((tm, D_MODEL), jnp.bfloat16)]
    return pl.pallas_call(
        functools.partial(_mixer_kernel, tm=tm, nt=nt, has_past=has_past, pos0=pos0, layer=l),
        grid=(bsz, nt),
        in_specs=in_specs,
        out_specs=out_specs,
        out_shape=out_shape,
        scratch_shapes=scratch,
        compiler_params=pltpu.CompilerParams(
            dimension_semantics=("arbitrary", "arbitrary"), vmem_limit_bytes=VMEM_LIMIT),
        name="mixer_past" if has_past else "mixer",
    )(*args)


def _ffn_kernel(*refs, tm, nt, nsteps, chained, final):
    it = iter(refs)
    x_ref = next(it)
    if not chained:
        past_ref = next(it)
    gffn_ref, wup_ref, fcw_ref, wdown_ref = next(it), next(it), next(it), next(it)
    if final:
        gfin_ref = next(it)
    xo_ref, fst_ref = next(it), next(it)
    carry, hbuf, act = next(it), next(it), next(it)
    if chained:
        xbuf, obuf, sem_in, sem_out = next(it), next(it), next(it), next(it)

    t = pl.program_id(1)
    fc = FF_CHUNK
    halves = (0, D_FF)
    seg = tm // SUBLANES

    if chained:
        step = pl.program_id(0) * nt + t
        slot = step % 2

        def tile_copies(tile, sl, outward):
            tb, row0 = tile // nt, (tile % nt) * tm
            cps = []
            for s in range(SUBLANES):
                hbm = (xo_ref if outward else x_ref).at[tb, pl.ds(row0 + s * seg, seg), :]
                if outward:
                    cps.append(pltpu.make_async_copy(obuf.at[sl, :, s, :], hbm, sem_out.at[sl, s]))
                else:
                    cps.append(pltpu.make_async_copy(hbm, xbuf.at[sl, :, s, :], sem_in.at[sl, s]))
            return cps

        @pl.when(step == 0)
        def _first_in():
            for cp in tile_copies(0, 0, False):
                cp.start()

        @pl.when(step + 1 < nsteps)
        def _next_in():
            for cp in tile_copies(step + 1, 1 - slot, False):
                cp.start()

        @pl.when(step >= 2)
        def _free_out():
            for cp in tile_copies(step - 2, slot, True):
                cp.wait()

        for cp in tile_copies(step, slot, False):
            cp.wait()

        @pl.when(t == 0)
        def _init():
            carry[...] = jnp.zeros(carry.shape, jnp.float32)

        xp = xbuf[slot].reshape(tm, D_MODEL)
    else:
        xp = jnp.swapaxes(x_ref[...].reshape(SUBLANES, seg, D_MODEL), 0, 1).reshape(tm, D_MODEL)
    hbuf[...] = _rmsnorm(xp, gffn_ref[...]).astype(jnp.bfloat16)
    sub = lax.broadcasted_iota(jnp.int32, (SUBLANES, fc), 0)
    for c in range(N_FF_CHUNKS):
        ys = []
        for hi, off in enumerate(halves):
            cols = slice(off + c * fc, off + (c + 1) * fc)
            lanes = slice(hi * fc, (hi + 1) * fc)
            up = _dot(hbuf[...], wup_ref[:, cols])
            last1, last2 = up[tm - SUBLANES:tm], up[tm - 2 * SUBLANES:tm - SUBLANES]
            if chained:
                p1 = jnp.where(sub == 0, pltpu.roll(carry[c, 1, :, lanes], 1, axis=0), pltpu.roll(last1, 1, axis=0))
                p2 = jnp.where(sub == 0, pltpu.roll(carry[c, 0, :, lanes], 1, axis=0), pltpu.roll(last2, 1, axis=0))
                fst_ref[0:1, cols] = last2[SUBLANES - 1:SUBLANES]
                fst_ref[1:2, cols] = last1[SUBLANES - 1:SUBLANES]
                if nt > 1:
                    carry[c, 0, :, lanes] = last2
                    carry[c, 1, :, lanes] = last1
            else:
                p2, p1 = past_ref[0, :, cols], past_ref[1, :, cols]
                fst_ref[0, :, cols] = last2
                fst_ref[1, :, cols] = last1
            d1 = jnp.concatenate([p1, up[0:tm - SUBLANES]], axis=0)
            d2 = jnp.concatenate([p2, p1, up[0:tm - 2 * SUBLANES]], axis=0)
            cw = fcw_ref[:, cols]
            ys.append(d2 * cw[0:1, :] + d1 * cw[1:2, :] + up * cw[2:3, :])
        gate, val = ys
        act[:, c * fc:(c + 1) * fc] = (gate * jax.nn.sigmoid(gate) * val).astype(jnp.bfloat16)
    down = _dot(act[...], wdown_ref[...])
    if chained:
        out = xbuf[slot].reshape(tm, D_MODEL) + down
    else:
        out = xp + down
    if final:
        out = _rmsnorm(out, gfin_ref[...])
    if chained:
        obuf[slot] = out.reshape(seg, SUBLANES, D_MODEL)
        for cp in tile_copies(step, slot, True):
            cp.start()

        @pl.when(step == nsteps - 1)
        def _drain():
            if nsteps > 1:
                for cp in tile_copies(step - 1, 1 - slot, True):
                    cp.wait()
            for cp in tile_copies(step, slot, True):
                cp.wait()
    else:
        xo_ref[...] = jnp.swapaxes(out.reshape(seg, SUBLANES, D_MODEL), 0, 1).reshape(tm, D_MODEL)


def _ffn_call(x, fpast, lw, l, g_final, *, tm):
    bsz, t_len, _ = x.shape
    chained = fpast is None
    final = g_final is not None
    args = []
    if chained:
        nt = t_len // tm
        grid = (bsz, nt)
        st_shape = (bsz, CONV_WIDTH - 1, 2 * D_FF)
        st_spec = pl.BlockSpec((None, CONV_WIDTH - 1, 2 * D_FF), lambda b, t: (b, 0, 0))
        in_specs = [pl.BlockSpec(memory_space=pl.ANY)]
        x_out_spec = pl.BlockSpec(memory_space=pl.ANY)
    else:
        assert bsz == SUBLANES and t_len % SUBLANES == 0
        x = x.reshape(1, bsz * t_len, D_MODEL)
        tm, nt, grid = bsz * t_len, 1, (1, 1)
        st_shape = (CONV_WIDTH - 1, bsz, 2 * D_FF)
        st_spec = pl.BlockSpec(st_shape, lambda b, t: (0, 0, 0))
        in_specs = [pl.BlockSpec((None, tm, D_MODEL), lambda b, t: (b, t, 0)),
                    pl.BlockSpec((None,) + st_shape, lambda b, t, _l=l: (_l, 0, 0, 0))]
        x_out_spec = pl.BlockSpec((None, tm, D_MODEL), lambda b, t: (b, t, 0))
        args.append(fpast)
    assert tm % (SUBLANES * SUBLANES) == 0
    seg = tm // SUBLANES
    args = [x] + args
    for n in ('g_ffn', 'w_up', 'ffn_conv_w', 'w_down'):
        in_specs.append(_layer_spec(lw[n].shape, l))
        args.append(lw[n])
    if final:
        in_specs.append(_const_spec(g_final.shape))
        args.append(g_final)
    out_shape = [jax.ShapeDtypeStruct(x.shape, jnp.float32), jax.ShapeDtypeStruct(st_shape, jnp.float32)]
    out_specs = [x_out_spec, st_spec]
    scratch = [pltpu.VMEM((N_FF_CHUNKS, CONV_WIDTH - 1, SUBLANES, 2 * FF_CHUNK), jnp.float32),
               pltpu.VMEM((tm, D_MODEL), jnp.bfloat16),
               pltpu.VMEM((tm, D_FF), jnp.bfloat16)]
    if chained:
        scratch += [pltpu.VMEM((2, seg, SUBLANES, D_MODEL), jnp.float32),
                    pltpu.VMEM((2, seg, SUBLANES, D_MODEL), jnp.float32),
                    pltpu.SemaphoreType.DMA((2, SUBLANES)),
                    pltpu.SemaphoreType.DMA((2, SUBLANES))]
    y, fst = pl.pallas_call(
        functools.partial(_ffn_kernel, tm=tm, nt=nt, nsteps=grid[0] * grid[1], chained=chained, final=final),
        grid=grid,
        in_specs=in_specs,
        out_specs=out_specs,
        out_shape=out_shape,
        scratch_shapes=scratch,
        compiler_params=pltpu.CompilerParams(
            dimension_semantics=("arbitrary", "arbitrary"), vmem_limit_bytes=VMEM_LIMIT),
        name=("ffn" if chained else "ffn_past") + ("_final" if final else ""),
    )(*args)
    if not chained:
        y, fst = y.reshape(bsz, t_len, D_MODEL), jnp.swapaxes(fst, 0, 1)
    return y, fst


def _rope_tables(pos0, t_len):
    half = ROT_DIM // 2
    inv_freq = np.power(np.float64(ROPE_THETA), -np.arange(half, dtype=np.float64) * (2.0 / ROT_DIM))
    ang = (pos0 + np.arange(t_len, dtype=np.float64))[:, None] * inv_freq[None, :]
    cos, sin = np.cos(ang), np.sin(ang)
    rest = HEAD_DIM - ROT_DIM
    cos_h = np.concatenate([cos, cos, np.ones((t_len, rest))], axis=1)
    sinlo_h = np.concatenate([-sin, np.zeros((t_len, half + rest))], axis=1)
    sinhi_h = np.concatenate([np.zeros((t_len, half)), sin, np.zeros((t_len, rest))], axis=1)
    return tuple(jnp.asarray(np.concatenate([a, a], axis=1), dtype=jnp.float32) for a in (cos_h, sinlo_h, sinhi_h))


def _stacked_weights(g_mix, w_in, attn_sink, conv_w, gmlp_ln_g, gmlp_ln_b, gmlp_w, gmlp_b, pool_w, pool_scale,
                     w_out, g_ffn, w_up, ffn_conv_w, w_down):
    eye = jnp.eye(len(POOL_WINDOWS), dtype=pool_w.dtype)
    pw = jnp.einsum('lgij,gh->lgihj', pool_w, eye).reshape(DEPTH, W_POOL, W_POOL)
    return {
        'g_mix': g_mix[:, None, :],
        'w_in': w_in.astype(jnp.bfloat16),
        'sink': attn_sink,
        'conv_w': conv_w,
        'ln_g': gmlp_ln_g[:, None, :],
        'ln_b': gmlp_ln_b[:, None, :],
        'gmlp_w': gmlp_w,
        'gmlp_b': jnp.repeat(jnp.swapaxes(gmlp_b, 1, 2), GMLP_GROUP_DIM, axis=2),
        'pool_w': pw.astype(jnp.bfloat16),
        'pool_scale': pool_scale[:, None, :],
        'w_out': w_out.astype(jnp.bfloat16),
        'g_ffn': g_ffn[:, None, :],
        'w_up': w_up.astype(jnp.bfloat16),
        'ffn_conv_w': ffn_conv_w,
        'w_down': w_down.astype(jnp.bfloat16),
    }


def _run(x, pos0, pasts, lw, g_final, tm):
    bsz, t_len, _ = x.shape
    tabs = _rope_tables(pos0, t_len)
    if pasts is not None:
        ck, cv, sc, sp, sf = pasts
        mpast = (ck.reshape(bsz, DEPTH, WINDOW, KV_WIDTH), cv.reshape(bsz, DEPTH, WINDOW, KV_WIDTH), sc, sp)
        fpast = jnp.transpose(sf, (1, 2, 0, 3))
    else:
        mpast, fpast = None, None
    states = []
    for l in range(DEPTH):
        mo = _mixer_call(x, tabs, mpast, lw, l, tm=tm, pos0=pos0)
        x_mid, kst, vst, cst, pst = mo[:5]
        gfin = g_final[None, :] if l == DEPTH - 1 else None
        x, fst = _ffn_call(x_mid, fpast, lw, l, gfin, tm=tm)
        st = [kst.reshape(bsz, -1, N_KV_HEADS, HEAD_DIM), vst.reshape(bsz, -1, N_KV_HEADS, HEAD_DIM),
              cst, pst, fst]
        if pasts is not None:
            st.append(mo[5])
        states.append(st)
    stacked = [jnp.stack([st[i] for st in states], axis=1) for i in range(len(states[0]))]
    return x, stacked


def kernel(x_prompt, x_sample, cache_attn_k, cache_attn_v, state_conv, state_pool, state_ffn_conv, g_mix, w_in,
           attn_sink, conv_w, gmlp_ln_g, gmlp_ln_b, gmlp_w, gmlp_b, pool_w, pool_scale, w_out, g_ffn, w_up,
           ffn_conv_w, w_down, g_final):
    lw = _stacked_weights(g_mix, w_in, attn_sink, conv_w, gmlp_ln_g, gmlp_ln_b, gmlp_w, gmlp_b, pool_w,
                          pool_scale, w_out, g_ffn, w_up, ffn_conv_w, w_down)
    y_prompt, sp = _run(x_prompt, 0, None, lw, g_final, tm=1024)
    y_sample, ss = _run(x_sample, PAST_LEN, (cache_attn_k, cache_attn_v, state_conv, state_pool, state_ffn_conv),
                        lw, g_final, tm=x_sample.shape[1])
    return (y_prompt, y_sample, sp[0], sp[1], sp[2], sp[3], sp[4], ss[0], ss[1], ss[2], ss[3], ss[4], ss[5])
```

```python
import functools

import numpy as np
import jax
import jax.numpy as jnp
from jax import lax
from jax.experimental import pallas as pl
from jax.experimental.pallas import tpu as pltpu

D_MODEL = 1024
DEPTH = 2
CHUNK = 64
HEAD_DIM = 64
N_Q_HEADS = 4
N_KV_HEADS = 2
KV_WIDTH = N_KV_HEADS * HEAD_DIM
W_ATTN = 256
W_CONV = 256
W_GMLP = 256
W_POOL = 256
WINDOW = 128
ROPE_THETA = 500000.0
ROT_DIM = 16
CONV_WIDTH = 3
GMLP_CHUNK = 128
GMLP_GROUPS = 4
GMLP_GROUP_DIM = 64
POOL_WINDOWS = (2, 4, 8, 16)
POOL_HIST = 15
D_FF = 2816
NORM_EPS = 1e-6
PROJ_WIDTH = 2048
PAST_LEN = 2048

LANES = 128
SUBLANES = 8
FF_CHUNK = 256
N_FF_CHUNKS = D_FF // FF_CHUNK
POOL_PAD = 24
ATTN_BLOCK = 2 * CHUNK
MIXER_SUB = 256
MIXER_STAGGER = 0
VMEM_LIMIT = 56 * 1024 * 1024

_Q0, _K0, _V0 = 0, 256, 384
_CB0, _CC0, _CH0 = 512, 768, 1024
_GU0, _GV0 = 1280, 1536
_P0 = 1792


def _rmsnorm(x, g):
    ms = jnp.mean(x * x, axis=-1, keepdims=True)
    return x * lax.rsqrt(ms + NORM_EPS) * g


def _dot(a, b):
    return jnp.dot(a, b, preferred_element_type=jnp.float32)


def _rope(x, cos_t, sin_lo, sin_hi):
    up = pltpu.roll(x, LANES - ROT_DIM // 2, axis=1)
    dn = pltpu.roll(x, ROT_DIM // 2, axis=1)
    return x * cos_t + up * sin_lo + dn * sin_hi


def _swap_halves(x):
    return pltpu.roll(x, HEAD_DIM, axis=1)


def _swap_row_halves(x):
    return jnp.concatenate([x[HEAD_DIM:], x[:HEAD_DIM]], axis=0)


def _mixer_kernel(*refs, tm, nt, pos0, layer):
    it = iter(refs)
    x_ref = next(it)
    cos_ref, sinlo_ref, sinhi_ref = next(it), next(it), next(it)
    bias_ref = next(it)
    gmix_ref, win_ref, sink_ref, convw_ref = next(it), next(it), next(it), next(it)
    lng_ref, lnb_ref, gw_ref, gb_ref = next(it), next(it), next(it), next(it)
    pw_ref, pscale_ref, wout_ref = next(it), next(it), next(it)
    xo_ref, kst_ref, vst_ref, cst_ref, pst_ref = next(it), next(it), next(it), next(it), next(it)
    kt, vext, kt2, vext2 = next(it), next(it), next(it), next(it)
    mext, pext, pa, pb, hb, mix = next(it), next(it), next(it), next(it), next(it), next(it)

    t = pl.program_id(1)
    qn = min(ATTN_BLOCK, tm)
    nkeys = WINDOW + qn
    kr = min(WINDOW, tm)
    gl = min(GMLP_CHUNK, tm)
    sub = min(MIXER_SUB, tm)

    @pl.when(t == 0)
    def _init():
        mext[0:SUBLANES, :] = jnp.zeros((SUBLANES, W_CONV), jnp.float32)
        pext[0:POOL_PAD, :] = jnp.zeros((POOL_PAD, W_POOL), jnp.float32)
        pa[0:SUBLANES, :] = jnp.zeros((SUBLANES, W_POOL), jnp.float32)
        pb[0:SUBLANES, :] = jnp.zeros((SUBLANES, W_POOL), jnp.float32)
        for buf in (vext, vext2):
            buf[0:WINDOW, :] = jnp.zeros((WINDOW, KV_WIDTH), jnp.bfloat16)
        for buf in (kt, kt2):
            buf[:, 0:WINDOW] = jnp.zeros((KV_WIDTH, WINDOW), jnp.bfloat16)

    lane = lax.broadcasted_iota(jnp.int32, (1, LANES), 1)
    lo = lane < HEAD_DIM
    lane256 = lax.broadcasted_iota(jnp.int32, (1, W_GMLP), 1) // GMLP_GROUP_DIM
    row4 = lax.broadcasted_iota(jnp.int32, (4 * qn, 1), 0)
    sk = jnp.where(row4 < qn, sink_ref[layer, 0],
                   jnp.where(row4 < 2 * qn, sink_ref[layer, 3],
                             jnp.where(row4 < 3 * qn, sink_ref[layer, 1], sink_ref[layer, 2])))
    gi = lax.broadcasted_iota(jnp.int32, (gl, gl), 0)
    gj = lax.broadcasted_iota(jnp.int32, (gl, gl), 1)
    tri = (gj // CHUNK) <= (gi // CHUNK)
    wcat = jnp.concatenate([jnp.where(tri, gw_ref[g, 0:gl, 0:gl], 0.0).astype(jnp.bfloat16)
                            for g in range(GMLP_GROUPS)], axis=1)
    gmask = [(lane256 == g).astype(jnp.bfloat16) for g in range(GMLP_GROUPS)]
    bias_g = gb_ref[0:gl, :]
    scale = HEAD_DIM ** -0.5

    def subtile(r0, last):
        rows = slice(r0, r0 + sub)
        krows = slice(WINDOW + r0, WINDOW + r0 + sub)
        x = x_ref[rows, :]
        hb[rows, :] = _rmsnorm(x, gmix_ref[...]).astype(jnp.bfloat16)
        zq = _dot(hb[rows, :], win_ref[:, _Q0:_CB0])
        trows = pl.ds(pl.multiple_of(t * tm + r0, SUBLANES), sub)
        cos_t, sin_lo, sin_hi = cos_ref[trows, :], sinlo_ref[trows, :], sinhi_ref[trows, :]
        qa = _rope(zq[:, 0:128], cos_t, sin_lo, sin_hi) * scale
        qb = _rope(zq[:, 128:256], cos_t, sin_lo, sin_hi) * scale
        k = _rope(zq[:, 256:384], cos_t, sin_lo, sin_hi)
        v = zq[:, 384:512]
        ktb = k.T.astype(jnp.bfloat16)
        kt[:, krows] = ktb
        kt2[:, krows] = _swap_row_halves(ktb)
        vext[krows, :] = v.astype(jnp.bfloat16)
        vext2[krows, :] = _swap_halves(v).astype(jnp.bfloat16)
        if last:
            kst_ref[...] = k[sub - kr:sub, :]
            vst_ref[...] = v[sub - kr:sub, :]
        yield
        scores = []
        for b in range(sub // qn):
            q0 = b * qn
            wrows = slice(r0 + q0, r0 + q0 + nkeys)
            qa_b, qb_b = qa[q0:q0 + qn, :], qb[q0:q0 + qn, :]
            q03 = jnp.concatenate([jnp.where(lo, qa_b, 0.0), jnp.where(lo, 0.0, qb_b)], axis=0)
            q12 = jnp.concatenate([jnp.where(lo, 0.0, qa_b), jnp.where(lo, qb_b, 0.0)], axis=0)
            s = jnp.concatenate([_dot(q03.astype(jnp.bfloat16), kt[:, wrows]),
                                 _dot(q12.astype(jnp.bfloat16), kt2[:, wrows])], axis=0)
            scores.append(s + (bias_ref[0] if r0 + q0 > 0 else bias_ref[jnp.where(t == 0, 1, 0)]))
        yield
        zc = _dot(hb[rows, :], win_ref[:, _CB0:_GU0])
        zg = _dot(hb[rows, :], win_ref[:, _GU0:_P0])
        zp = _dot(hb[rows, :], win_ref[:, _P0:PROJ_WIDTH])
        yield
        probs, dens = [], []
        for s in scores:
            m = jnp.maximum(jnp.max(s, axis=-1, keepdims=True), sk)
            p = jnp.exp(s - m)
            dens.append(jnp.sum(p, axis=-1, keepdims=True) + jnp.exp(sk - m))
            probs.append(p.astype(jnp.bfloat16))
        yield
        for b in range(sub // qn):
            q0 = r0 + b * qn
            wrows = slice(q0, q0 + nkeys)
            o03 = _dot(probs[b][0:2 * qn], vext[wrows, :]) / dens[b][0:2 * qn]
            o12 = _dot(probs[b][2 * qn:4 * qn], vext2[wrows, :]) / dens[b][2 * qn:4 * qn]
            oa = jnp.where(lo, o03[0:qn], o12[0:qn])
            ob = jnp.where(lo, o12[qn:2 * qn], o03[qn:2 * qn])
            mix[q0:q0 + qn, 0:128] = oa.astype(jnp.bfloat16)
            mix[q0:q0 + qn, 128:256] = ob.astype(jnp.bfloat16)
        yield
        mprod = zc[:, 256:512] * zc[:, 512:768]
        mext[SUBLANES + r0:SUBLANES + r0 + sub, :] = mprod
        cw = convw_ref[...]
        yc = (mext[SUBLANES - 2 + r0:SUBLANES - 2 + r0 + sub, :] * cw[0:1, :]
              + mext[SUBLANES - 1 + r0:SUBLANES - 1 + r0 + sub, :] * cw[1:2, :]
              + mprod * cw[2:3, :])
        mix[rows, 256:512] = (zc[:, 0:256] * yc).astype(jnp.bfloat16)
        if last:
            cst_ref[...] = mext[SUBLANES + tm - 2:SUBLANES + tm, :]

        u = jax.nn.gelu(zg[:, 0:256])
        gv = jax.nn.gelu(zg[:, 256:512])
        mu = jnp.mean(gv, axis=-1, keepdims=True)
        var = jnp.mean(jnp.square(gv - mu), axis=-1, keepdims=True)
        vn = (gv - mu) * lax.rsqrt(var + NORM_EPS) * lng_ref[...] + lnb_ref[...]
        vnb = vn.astype(jnp.bfloat16)
        for c in range(sub // gl):
            c0 = c * gl
            vc = vnb[c0:c0 + gl, :]
            s = _dot(wcat, jnp.concatenate([vc * gmask[g] for g in range(GMLP_GROUPS)], axis=0))
            mix[r0 + c0:r0 + c0 + gl, 512:768] = (u[c0:c0 + gl, :] * (s + bias_g)).astype(jnp.bfloat16)

        pext[POOL_PAD + r0:POOL_PAD + r0 + sub, :] = zp
        n = POOL_PAD - SUBLANES + sub
        lo_r = SUBLANES + r0
        pa[lo_r:lo_r + n, :] = pext[lo_r:lo_r + n, :] + pext[lo_r - 1:lo_r - 1 + n, :]
        pb[lo_r:lo_r + n, :] = pa[lo_r:lo_r + n, :] + pa[lo_r - 2:lo_r - 2 + n, :]
        tr = slice(POOL_PAD + r0, POOL_PAD + r0 + sub)
        s2 = pa[tr, 0:LANES]
        s4 = pb[tr, 0:LANES]
        pa[lo_r:lo_r + n, LANES:] = pb[lo_r:lo_r + n, LANES:] + pb[lo_r - 4:lo_r - 4 + n, LANES:]
        s8 = pa[tr, LANES:]
        s16 = s8 + pa[POOL_PAD - 8 + r0:POOL_PAD - 8 + r0 + sub, LANES:]
        wsum = jnp.concatenate([jnp.where(lo, s2, s4), jnp.where(lo, s8, s16)], axis=1)
        prow = lax.broadcasted_iota(jnp.int32, (sub, 1), 0) + (pos0 + 1 + r0) + t * tm
        winl = jnp.left_shift(2, lane256).astype(jnp.float32)
        cnt = jnp.minimum(prow.astype(jnp.float32), winl)
        pooled = wsum / cnt - zp
        mix[rows, 768:1024] = (_dot(pooled.astype(jnp.bfloat16), pw_ref[...]) * pscale_ref[...]).astype(jnp.bfloat16)
        if last:
            pst_ref[...] = pext[POOL_PAD + tm - POOL_HIST:POOL_PAD + tm, :]
        yield
        xo_ref[rows, :] = x_ref[rows, :] + _dot(mix[rows, :], wout_ref[...])
        yield

    nsub = tm // sub
    gens = [subtile(i * sub, i == nsub - 1) for i in range(nsub)]
    n_stages = 7
    for step in range(n_stages + MIXER_STAGGER * (nsub - 1)):
        for i, g in enumerate(gens):
            if 0 <= step - MIXER_STAGGER * i < n_stages:
                next(g)

    if nt > 1:
        for buf in (vext, vext2):
            buf[0:WINDOW, :] = buf[tm:tm + WINDOW, :]
        for buf in (kt, kt2):
            buf[:, 0:WINDOW] = buf[:, tm:tm + WINDOW]
        mext[0:SUBLANES, :] = mext[tm:tm + SUBLANES, :]
        pext[0:POOL_PAD, :] = pext[tm:tm + POOL_PAD, :]


def _const_spec(shape):
    nd = len(shape)
    return pl.BlockSpec(shape, lambda b, t, _nd=nd: (0,) * _nd, pipeline_mode=pl.Buffered(1))


def _layer_spec(stacked_shape, l):
    nd = len(stacked_shape) - 1
    return pl.BlockSpec((None,) + tuple(stacked_shape[1:]), lambda b, t, _nd=nd, _l=l: (_l,) + (0,) * _nd,
                        pipeline_mode=pl.Buffered(1))


def _window_bias(qn):
    rows = (np.arange(4 * qn) % qn) // CHUNK
    keys = np.arange(WINDOW + qn)
    ok = (keys[None, :] >= rows[:, None] * CHUNK) & (keys[None, :] < (rows[:, None] + 3) * CHUNK)
    first = ok & (keys[None, :] >= WINDOW)
    return np.where(np.stack([ok, first]), 0.0, -np.inf).astype(np.float32)


def _mixer_call(x, tabs, lw, l, *, tm, pos0):
    bsz, t_len, _ = x.shape
    nt = t_len // tm
    assert tm % MIXER_SUB == 0 and MIXER_SUB % ATTN_BLOCK == 0
    kr = min(WINDOW, tm)
    qn = min(ATTN_BLOCK, tm)
    in_specs = [pl.BlockSpec((None, tm, D_MODEL), lambda b, t: (b, t, 0))]
    in_specs += [_const_spec(tabs[0].shape)] * 3
    bias = jnp.asarray(_window_bias(qn))
    in_specs.append(_const_spec(bias.shape))
    args = [x, *tabs, bias]
    wnames = ('g_mix', 'w_in', 'sink', 'conv_w', 'ln_g', 'ln_b', 'gmlp_w', 'gmlp_b', 'pool_w', 'pool_scale', 'w_out')
    for n in wnames:
        a = lw[n]
        if n == 'sink':
            in_specs.append(pl.BlockSpec(memory_space=pltpu.SMEM))
        else:
            in_specs.append(_layer_spec(a.shape, l))
        args.append(a)
    out_shape = [jax.ShapeDtypeStruct((bsz, t_len, D_MODEL), jnp.float32),
                 jax.ShapeDtypeStruct((bsz, kr, KV_WIDTH), jnp.float32),
                 jax.ShapeDtypeStruct((bsz, kr, KV_WIDTH), jnp.float32),
                 jax.ShapeDtypeStruct((bsz, CONV_WIDTH - 1, W_CONV), jnp.float32),
                 jax.ShapeDtypeStruct((bsz, POOL_HIST, W_POOL), jnp.float32)]
    out_specs = [pl.BlockSpec((None, tm, D_MODEL), lambda b, t: (b, t, 0)),
                 pl.BlockSpec((None, kr, KV_WIDTH), lambda b, t: (b, 0, 0)),
                 pl.BlockSpec((None, kr, KV_WIDTH), lambda b, t: (b, 0, 0)),
                 pl.BlockSpec((None, CONV_WIDTH - 1, W_CONV), lambda b, t: (b, 0, 0)),
                 pl.BlockSpec((None, POOL_HIST, W_POOL), lambda b, t: (b, 0, 0))]
    scratch = [pltpu.VMEM((KV_WIDTH, WINDOW + tm), jnp.bfloat16), pltpu.VMEM((WINDOW + tm, KV_WIDTH), jnp.bfloat16)] * 2
    scratch += [pltpu.VMEM((SUBLANES + tm, W_CONV), jnp.float32),
                pltpu.VMEM((POOL_PAD + tm, W_POOL), jnp.float32),
                pltpu.VMEM((POOL_PAD + tm, W_POOL), jnp.float32),
                pltpu.VMEM((POOL_PAD + tm, W_POOL), jnp.float32),
                pltpu.VMEM((tm, D_MODEL), jnp.bfloat16),
                pltpu.VMEM((tm, D_MODEL), jnp.bfloat16)]
    return pl.pallas_call(
        functools.partial(_mixer_kernel, tm=tm, nt=nt, pos0=pos0, layer=l),
        grid=(bsz, nt),
        in_specs=in_specs,
        out_specs=out_specs,
        out_shape=out_shape,
        scratch_shapes=scratch,
        compiler_params=pltpu.CompilerParams(
            dimension_semantics=("arbitrary", "arbitrary"), vmem_limit_bytes=VMEM_LIMIT),
        name="mixer",
    )(*args)


def _mixer_sample_kernel(*refs, nseq, ts, pos0, layer):
    it = iter(refs)
    x_ref = next(it)
    cos_ref, sinlo_ref, sinhi_ref = next(it), next(it), next(it)
    kpast_ref, vpast_ref, cpast_ref, ppast_ref = next(it), next(it), next(it), next(it)
    gmix_ref, win_ref, sink_ref, convw_ref = next(it), next(it), next(it), next(it)
    lng_ref, lnb_ref, gw_ref, gb_ref = next(it), next(it), next(it), next(it)
    pw_ref, pscale_ref, wout_ref = next(it), next(it), next(it)
    xo_ref, kst_ref, vst_ref, cst_ref, pst_ref, grows_ref = (next(it), next(it), next(it), next(it), next(it),
                                                             next(it))
    mext, pext, pa, pb, hb, mix = next(it), next(it), next(it), next(it), next(it), next(it)

    lane = lax.broadcasted_iota(jnp.int32, (1, LANES), 1)
    lo = lane < HEAD_DIM
    lane256 = lax.broadcasted_iota(jnp.int32, (1, W_GMLP), 1) // GMLP_GROUP_DIM
    row4 = lax.broadcasted_iota(jnp.int32, (4 * ts, 1), 0)
    sk = jnp.where(row4 < ts, sink_ref[layer, 0],
                   jnp.where(row4 < 2 * ts, sink_ref[layer, 3],
                             jnp.where(row4 < 3 * ts, sink_ref[layer, 1], sink_ref[layer, 2])))
    wcat = jnp.concatenate([gw_ref[g, 0:ts, 0:ts].astype(jnp.bfloat16) for g in range(GMLP_GROUPS)], axis=1)
    gmask = [(lane256 == g).astype(jnp.bfloat16) for g in range(GMLP_GROUPS)]
    bias_g = gb_ref[0:ts, :]
    scale = HEAD_DIM ** -0.5

    x = x_ref[...]
    hb[...] = _rmsnorm(x, gmix_ref[...]).astype(jnp.bfloat16)
    zq = _dot(hb[...], win_ref[:, _Q0:_CB0])
    zc = _dot(hb[...], win_ref[:, _CB0:_GU0])
    zg = _dot(hb[...], win_ref[:, _GU0:_P0])
    zp = _dot(hb[...], win_ref[:, _P0:PROJ_WIDTH])
    cos_t, sin_lo, sin_hi = cos_ref[...], sinlo_ref[...], sinhi_ref[...]
    qa = _rope(zq[:, 0:128], cos_t, sin_lo, sin_hi) * scale
    qb = _rope(zq[:, 128:256], cos_t, sin_lo, sin_hi) * scale
    k = _rope(zq[:, 256:384], cos_t, sin_lo, sin_hi)
    v = zq[:, 384:512]
    kst_ref[...] = k
    vst_ref[...] = v
    mprod = zc[:, 256:512] * zc[:, 512:768]
    u = jax.nn.gelu(zg[:, 0:256])
    gv = jax.nn.gelu(zg[:, 256:512])
    mu = jnp.mean(gv, axis=-1, keepdims=True)
    var = jnp.mean(jnp.square(gv - mu), axis=-1, keepdims=True)
    vn = (gv - mu) * lax.rsqrt(var + NORM_EPS) * lng_ref[...] + lnb_ref[...]
    grows_ref[...] = vn
    vnb = vn.astype(jnp.bfloat16)
    cw = convw_ref[...]
    winl = jnp.left_shift(2, lane256).astype(jnp.float32)
    prow = lax.broadcasted_iota(jnp.int32, (ts, 1), 0) + (pos0 + 1)
    cnt = jnp.minimum(prow.astype(jnp.float32), winl)
    pooled = []
    for i in range(nseq):
        r = slice(i * ts, (i + 1) * ts)
        kp, vp = kpast_ref[i], vpast_ref[i]
        ktw = jnp.concatenate([kp.T, k[r].T], axis=1).astype(jnp.bfloat16)
        vw = jnp.concatenate([vp, v[r]], axis=0)
        q03 = jnp.concatenate([jnp.where(lo, qa[r], 0.0), jnp.where(lo, 0.0, qb[r])], axis=0)
        q12 = jnp.concatenate([jnp.where(lo, 0.0, qa[r]), jnp.where(lo, qb[r], 0.0)], axis=0)
        s = jnp.concatenate([_dot(q03.astype(jnp.bfloat16), ktw),
                             _dot(q12.astype(jnp.bfloat16), _swap_row_halves(ktw))], axis=0)
        m = jnp.maximum(jnp.max(s, axis=-1, keepdims=True), sk)
        p = jnp.exp(s - m)
        den = jnp.sum(p, axis=-1, keepdims=True) + jnp.exp(sk - m)
        pb16 = p.astype(jnp.bfloat16)
        o03 = _dot(pb16[0:2 * ts], vw.astype(jnp.bfloat16)) / den[0:2 * ts]
        o12 = _dot(pb16[2 * ts:4 * ts], _swap_halves(vw).astype(jnp.bfloat16)) / den[2 * ts:4 * ts]
        mix[r, 0:128] = jnp.where(lo, o03[0:ts], o12[0:ts]).astype(jnp.bfloat16)
        mix[r, 128:256] = jnp.where(lo, o12[ts:2 * ts], o03[ts:2 * ts]).astype(jnp.bfloat16)
        mext[i, 0:SUBLANES, :] = jnp.zeros((SUBLANES, W_CONV), jnp.float32)
        mext[i, SUBLANES - (CONV_WIDTH - 1):SUBLANES, :] = cpast_ref[i]
        mext[i, SUBLANES:SUBLANES + ts, :] = mprod[r]
        yc = (mext[i, SUBLANES - 2:SUBLANES - 2 + ts, :] * cw[0:1, :]
              + mext[i, SUBLANES - 1:SUBLANES - 1 + ts, :] * cw[1:2, :] + mprod[r] * cw[2:3, :])
        mix[r, 256:512] = (zc[r, 0:256] * yc).astype(jnp.bfloat16)
        cst_ref[i] = mext[i, SUBLANES + ts - 2:SUBLANES + ts, :]
        vc = vnb[r]
        sg = _dot(wcat, jnp.concatenate([vc * gmask[g] for g in range(GMLP_GROUPS)], axis=0))
        mix[r, 512:768] = (u[r] * (sg + bias_g)).astype(jnp.bfloat16)
        pext[i, 0:POOL_PAD, :] = jnp.zeros((POOL_PAD, W_POOL), jnp.float32)
        pext[i, POOL_PAD - POOL_HIST:POOL_PAD, :] = ppast_ref[i]
        pext[i, POOL_PAD:POOL_PAD + ts, :] = zp[r]
        pa[i, 0:SUBLANES, :] = jnp.zeros((SUBLANES, W_POOL), jnp.float32)
        pb[i, 0:SUBLANES, :] = jnp.zeros((SUBLANES, W_POOL), jnp.float32)
        n = POOL_PAD - SUBLANES + ts
        b0 = SUBLANES
        pa[i, b0:b0 + n, :] = pext[i, b0:b0 + n, :] + pext[i, b0 - 1:b0 - 1 + n, :]
        pb[i, b0:b0 + n, :] = pa[i, b0:b0 + n, :] + pa[i, b0 - 2:b0 - 2 + n, :]
        s2 = pa[i, POOL_PAD:POOL_PAD + ts, 0:LANES]
        s4 = pb[i, POOL_PAD:POOL_PAD + ts, 0:LANES]
        pa[i, b0:b0 + n, LANES:] = pb[i, b0:b0 + n, LANES:] + pb[i, b0 - 4:b0 - 4 + n, LANES:]
        s8 = pa[i, POOL_PAD:POOL_PAD + ts, LANES:]
        s16 = s8 + pa[i, POOL_PAD - 8:POOL_PAD - 8 + ts, LANES:]
        wsum = jnp.concatenate([jnp.where(lo, s2, s4), jnp.where(lo, s8, s16)], axis=1)
        pooled.append(wsum / cnt - zp[r])
        pst_ref[i] = pext[i, POOL_PAD + ts - POOL_HIST:POOL_PAD + ts, :]
    pooled = jnp.concatenate(pooled, axis=0).astype(jnp.bfloat16)
    mix[:, 768:1024] = (_dot(pooled, pw_ref[...]) * pscale_ref[...]).astype(jnp.bfloat16)
    xo_ref[...] = x + _dot(mix[...], wout_ref[...])


def _mixer_sample_call(x, tabs, past, lw, l, *, pos0):
    nseq, ts, _ = x.shape
    assert ts <= CHUNK and ts % SUBLANES == 0
    tm = nseq * ts
    zero3 = lambda b: (0, 0, 0)
    in_specs = [pl.BlockSpec((None, tm, D_MODEL), zero3)]
    in_specs += [pl.BlockSpec((tm, LANES), lambda b: (0, 0))] * 3
    args = [x.reshape(1, tm, D_MODEL)] + [jnp.tile(a, (nseq, 1)) for a in tabs]
    for a in past:
        in_specs.append(pl.BlockSpec((nseq, None) + a.shape[2:], lambda b, _l=l: (0, _l, 0, 0)))
        args.append(a)
    wnames = ('g_mix', 'w_in', 'sink', 'conv_w', 'ln_g', 'ln_b', 'gmlp_w', 'gmlp_b', 'pool_w', 'pool_scale', 'w_out')
    for n in wnames:
        a = lw[n]
        if n == 'sink':
            in_specs.append(pl.BlockSpec(memory_space=pltpu.SMEM))
        else:
            nd = a.ndim - 1
            in_specs.append(pl.BlockSpec((None,) + a.shape[1:], lambda b, _nd=nd, _l=l: (_l,) + (0,) * _nd))
        args.append(a)
    out_shape = [jax.ShapeDtypeStruct((1, tm, D_MODEL), jnp.float32),
                 jax.ShapeDtypeStruct((tm, KV_WIDTH), jnp.float32),
                 jax.ShapeDtypeStruct((tm, KV_WIDTH), jnp.float32),
                 jax.ShapeDtypeStruct((nseq, CONV_WIDTH - 1, W_CONV), jnp.float32),
                 jax.ShapeDtypeStruct((nseq, POOL_HIST, W_POOL), jnp.float32),
                 jax.ShapeDtypeStruct((tm, W_GMLP), jnp.float32)]
    out_specs = [pl.BlockSpec((None, tm, D_MODEL), zero3),
                 pl.BlockSpec((tm, KV_WIDTH), lambda b: (0, 0)),
                 pl.BlockSpec((tm, KV_WIDTH), lambda b: (0, 0)),
                 pl.BlockSpec((nseq, CONV_WIDTH - 1, W_CONV), zero3),
                 pl.BlockSpec((nseq, POOL_HIST, W_POOL), zero3),
                 pl.BlockSpec((tm, W_GMLP), lambda b: (0, 0))]
    scratch = [pltpu.VMEM((nseq, SUBLANES + ts, W_CONV), jnp.float32),
               pltpu.VMEM((nseq, POOL_PAD + ts, W_POOL), jnp.float32),
               pltpu.VMEM((nseq, POOL_PAD + ts, W_POOL), jnp.float32),
               pltpu.VMEM((nseq, POOL_PAD + ts, W_POOL), jnp.float32),
               pltpu.VMEM((tm, D_MODEL), jnp.bfloat16),
               pltpu.VMEM((tm, D_MODEL), jnp.bfloat16)]
    xo, kst, vst, cst, pst, grows = pl.pallas_call(
        functools.partial(_mixer_sample_kernel, nseq=nseq, ts=ts, pos0=pos0, layer=l),
        grid=(1,),
        in_specs=in_specs,
        out_specs=out_specs,
        out_shape=out_shape,
        scratch_shapes=scratch,
        compiler_params=pltpu.CompilerParams(dimension_semantics=("arbitrary",), vmem_limit_bytes=VMEM_LIMIT),
        name="mixer_past",
    )(*args)
    return (xo.reshape(nseq, ts, D_MODEL), kst.reshape(nseq, ts, KV_WIDTH), vst.reshape(nseq, ts, KV_WIDTH),
            cst, pst, grows.reshape(nseq, ts, W_GMLP))


def _ffn_kernel(*refs, tm, nt, nsteps, chained, final):
    it = iter(refs)
    x_ref = next(it)
    if not chained:
        past_ref = next(it)
    gffn_ref, wup_ref, fcw_ref, wdown_ref = next(it), next(it), next(it), next(it)
    if final:
        gfin_ref = next(it)
    xo_ref, fst_ref = next(it), next(it)
    carry, hbuf, act = next(it), next(it), next(it)
    if chained:
        xbuf, obuf, sem_in, sem_out = next(it), next(it), next(it), next(it)

    t = pl.program_id(1)
    fc = FF_CHUNK
    halves = (0, D_FF)
    seg = tm // SUBLANES

    if chained:
        step = pl.program_id(0) * nt + t
        slot = step % 2

        def tile_copies(tile, sl, outward):
            tb, row0 = tile // nt, (tile % nt) * tm
            cps = []
            for s in range(SUBLANES):
                hbm = (xo_ref if outward else x_ref).at[tb, pl.ds(row0 + s * seg, seg), :]
                if outward:
                    cps.append(pltpu.make_async_copy(obuf.at[sl, :, s, :], hbm, sem_out.at[sl, s]))
                else:
                    cps.append(pltpu.make_async_copy(hbm, xbuf.at[sl, :, s, :], sem_in.at[sl, s]))
            return cps

        @pl.when(step == 0)
        def _first_in():
            for cp in tile_copies(0, 0, False):
                cp.start()

        @pl.when(step + 1 < nsteps)
        def _next_in():
            for cp in tile_copies(step + 1, 1 - slot, False):
                cp.start()

        @pl.when(step >= 2)
        def _free_out():
            for cp in tile_copies(step - 2, slot, True):
                cp.wait()

        for cp in tile_copies(step, slot, False):
            cp.wait()

        @pl.when(t == 0)
        def _init():
            carry[...] = jnp.zeros(carry.shape, jnp.float32)

        xp = xbuf[slot].reshape(tm, D_MODEL)
    else:
        xp = jnp.swapaxes(x_ref[...].reshape(SUBLANES, seg, D_MODEL), 0, 1).reshape(tm, D_MODEL)
    hbuf[...] = _rmsnorm(xp, gffn_ref[...]).astype(jnp.bfloat16)
    sub = lax.broadcasted_iota(jnp.int32, (SUBLANES, fc), 0)
    for c in range(N_FF_CHUNKS):
        ys = []
        for hi, off in enumerate(halves):
            cols = slice(off + c * fc, off + (c + 1) * fc)
            lanes = slice(hi * fc, (hi + 1) * fc)
            up = _dot(hbuf[...], wup_ref[:, cols])
            last1, last2 = up[tm - SUBLANES:tm], up[tm - 2 * SUBLANES:tm - SUBLANES]
            if chained:
                p1 = jnp.where(sub == 0, pltpu.roll(carry[c, 1, :, lanes], 1, axis=0), pltpu.roll(last1, 1, axis=0))
                p2 = jnp.where(sub == 0, pltpu.roll(carry[c, 0, :, lanes], 1, axis=0), pltpu.roll(last2, 1, axis=0))
                fst_ref[0:1, cols] = last2[SUBLANES - 1:SUBLANES]
                fst_ref[1:2, cols] = last1[SUBLANES - 1:SUBLANES]
                if nt > 1:
                    carry[c, 0, :, lanes] = last2
                    carry[c, 1, :, lanes] = last1
            else:
                p2, p1 = past_ref[0, :, cols], past_ref[1, :, cols]
                fst_ref[0, :, cols] = last2
                fst_ref[1, :, cols] = last1
            d1 = jnp.concatenate([p1, up[0:tm - SUBLANES]], axis=0)
            d2 = jnp.concatenate([p2, p1, up[0:tm - 2 * SUBLANES]], axis=0)
            cw = fcw_ref[:, cols]
            ys.append(d2 * cw[0:1, :] + d1 * cw[1:2, :] + up * cw[2:3, :])
        gate, val = ys
        act[:, c * fc:(c + 1) * fc] = (gate * jax.nn.sigmoid(gate) * val).astype(jnp.bfloat16)
    down = _dot(act[...], wdown_ref[...])
    if chained:
        out = xbuf[slot].reshape(tm, D_MODEL) + down
    else:
        out = xp + down
    if final:
        out = _rmsnorm(out, gfin_ref[...])
    if chained:
        obuf[slot] = out.reshape(seg, SUBLANES, D_MODEL)
        for cp in tile_copies(step, slot, True):
            cp.start()

        @pl.when(step == nsteps - 1)
        def _drain():
            if nsteps > 1:
                for cp in tile_copies(step - 1, 1 - slot, True):
                    cp.wait()
            for cp in tile_copies(step, slot, True):
                cp.wait()
    else:
        xo_ref[...] = jnp.swapaxes(out.reshape(seg, SUBLANES, D_MODEL), 0, 1).reshape(tm, D_MODEL)


def _ffn_call(x, fpast, lw, l, g_final, *, tm):
    bsz, t_len, _ = x.shape
    chained = fpast is None
    final = g_final is not None
    args = []
    if chained:
        nt = t_len // tm
        grid = (bsz, nt)
        st_shape = (bsz, CONV_WIDTH - 1, 2 * D_FF)
        st_spec = pl.BlockSpec((None, CONV_WIDTH - 1, 2 * D_FF), lambda b, t: (b, 0, 0))
        in_specs = [pl.BlockSpec(memory_space=pl.ANY)]
        x_out_spec = pl.BlockSpec(memory_space=pl.ANY)
    else:
        assert bsz == SUBLANES and t_len % SUBLANES == 0
        x = x.reshape(1, bsz * t_len, D_MODEL)
        tm, nt, grid = bsz * t_len, 1, (1, 1)
        st_shape = (CONV_WIDTH - 1, bsz, 2 * D_FF)
        st_spec = pl.BlockSpec(st_shape, lambda b, t: (0, 0, 0))
        in_specs = [pl.BlockSpec((None, tm, D_MODEL), lambda b, t: (b, t, 0)),
                    pl.BlockSpec((None,) + st_shape, lambda b, t, _l=l: (_l, 0, 0, 0))]
        x_out_spec = pl.BlockSpec((None, tm, D_MODEL), lambda b, t: (b, t, 0))
        args.append(fpast)
    assert tm % (SUBLANES * SUBLANES) == 0
    seg = tm // SUBLANES
    args = [x] + args
    for n in ('g_ffn', 'w_up', 'ffn_conv_w', 'w_down'):
        in_specs.append(_layer_spec(lw[n].shape, l))
        args.append(lw[n])
    if final:
        in_specs.append(_const_spec(g_final.shape))
        args.append(g_final)
    out_shape = [jax.ShapeDtypeStruct(x.shape, jnp.float32), jax.ShapeDtypeStruct(st_shape, jnp.float32)]
    out_specs = [x_out_spec, st_spec]
    scratch = [pltpu.VMEM((N_FF_CHUNKS, CONV_WIDTH - 1, SUBLANES, 2 * FF_CHUNK), jnp.float32),
               pltpu.VMEM((tm, D_MODEL), jnp.bfloat16),
               pltpu.VMEM((tm, D_FF), jnp.bfloat16)]
    if chained:
        scratch += [pltpu.VMEM((2, seg, SUBLANES, D_MODEL), jnp.float32),
                    pltpu.VMEM((2, seg, SUBLANES, D_MODEL), jnp.float32),
                    pltpu.SemaphoreType.DMA((2, SUBLANES)),
                    pltpu.SemaphoreType.DMA((2, SUBLANES))]
    y, fst = pl.pallas_call(
        functools.partial(_ffn_kernel, tm=tm, nt=nt, nsteps=grid[0] * grid[1], chained=chained, final=final),
        grid=grid,
        in_specs=in_specs,
        out_specs=out_specs,
        out_shape=out_shape,
        scratch_shapes=scratch,
        compiler_params=pltpu.CompilerParams(
            dimension_semantics=("arbitrary", "arbitrary"), vmem_limit_bytes=VMEM_LIMIT),
        name=("ffn" if chained else "ffn_past") + ("_final" if final else ""),
    )(*args)
    if not chained:
        y, fst = y.reshape(bsz, t_len, D_MODEL), jnp.swapaxes(fst, 0, 1)
    return y, fst


def _rope_tables(pos0, t_len):
    half = ROT_DIM // 2
    inv_freq = np.power(np.float64(ROPE_THETA), -np.arange(half, dtype=np.float64) * (2.0 / ROT_DIM))
    ang = (pos0 + np.arange(t_len, dtype=np.float64))[:, None] * inv_freq[None, :]
    cos, sin = np.cos(ang), np.sin(ang)
    rest = HEAD_DIM - ROT_DIM
    cos_h = np.concatenate([cos, cos, np.ones((t_len, rest))], axis=1)
    sinlo_h = np.concatenate([-sin, np.zeros((t_len, half + rest))], axis=1)
    sinhi_h = np.concatenate([np.zeros((t_len, half)), sin, np.zeros((t_len, rest))], axis=1)
    return tuple(jnp.asarray(np.concatenate([a, a], axis=1), dtype=jnp.float32) for a in (cos_h, sinlo_h, sinhi_h))


def _stacked_weights(g_mix, w_in, attn_sink, conv_w, gmlp_ln_g, gmlp_ln_b, gmlp_w, gmlp_b, pool_w, pool_scale,
                     w_out, g_ffn, w_up, ffn_conv_w, w_down):
    eye = jnp.eye(len(POOL_WINDOWS), dtype=pool_w.dtype)
    pw = jnp.einsum('lgij,gh->lgihj', pool_w, eye).reshape(DEPTH, W_POOL, W_POOL)
    return {
        'g_mix': g_mix[:, None, :],
        'w_in': w_in.astype(jnp.bfloat16),
        'sink': attn_sink,
        'conv_w': conv_w,
        'ln_g': gmlp_ln_g[:, None, :],
        'ln_b': gmlp_ln_b[:, None, :],
        'gmlp_w': gmlp_w,
        'gmlp_b': jnp.repeat(jnp.swapaxes(gmlp_b, 1, 2), GMLP_GROUP_DIM, axis=2),
        'pool_w': pw.astype(jnp.bfloat16),
        'pool_scale': pool_scale[:, None, :],
        'w_out': w_out.astype(jnp.bfloat16),
        'g_ffn': g_ffn[:, None, :],
        'w_up': w_up.astype(jnp.bfloat16),
        'ffn_conv_w': ffn_conv_w,
        'w_down': w_down.astype(jnp.bfloat16),
    }


def _run(x, pos0, pasts, lw, g_final, tm):
    bsz, t_len, _ = x.shape
    tabs = _rope_tables(pos0, t_len)
    if pasts is not None:
        ck, cv, sc, sp, sf = pasts
        mpast = (ck.reshape(bsz, DEPTH, WINDOW, KV_WIDTH), cv.reshape(bsz, DEPTH, WINDOW, KV_WIDTH), sc, sp)
        fpast = jnp.transpose(sf, (1, 2, 0, 3))
    else:
        mpast, fpast = None, None
    states = []
    for l in range(DEPTH):
        if pasts is None:
            mo = _mixer_call(x, tabs, lw, l, tm=tm, pos0=pos0)
        else:
            mo = _mixer_sample_call(x, tabs, mpast, lw, l, pos0=pos0)
        x_mid, kst, vst, cst, pst = mo[:5]
        gfin = g_final[None, :] if l == DEPTH - 1 else None
        x, fst = _ffn_call(x_mid, fpast, lw, l, gfin, tm=tm)
        st = [kst.reshape(bsz, -1, N_KV_HEADS, HEAD_DIM), vst.reshape(bsz, -1, N_KV_HEADS, HEAD_DIM),
              cst, pst, fst]
        if pasts is not None:
            st.append(mo[5])
        states.append(st)
    stacked = [jnp.stack([st[i] for st in states], axis=1) for i in range(len(states[0]))]
    return x, stacked


def kernel(x_prompt, x_sample, cache_attn_k, cache_attn_v, state_conv, state_pool, state_ffn_conv, g_mix, w_in,
           attn_sink, conv_w, gmlp_ln_g, gmlp_ln_b, gmlp_w, gmlp_b, pool_w, pool_scale, w_out, g_ffn, w_up,
           ffn_conv_w, w_down, g_final):
    lw = _stacked_weights(g_mix, w_in, attn_sink, conv_w, gmlp_ln_g, gmlp_ln_b, gmlp_w, gmlp_b, pool_w,
                          pool_scale, w_out, g_ffn, w_up, ffn_conv_w, w_down)
    y_prompt, sp = _run(x_prompt, 0, None, lw, g_final, tm=1024)
    y_sample, ss = _run(x_sample, PAST_LEN, (cache_attn_k, cache_attn_v, state_conv, state_pool, state_ffn_conv),
                        lw, g_final, tm=x_sample.shape[1])
    return (y_prompt, y_sample, sp[0], sp[1], sp[2], sp[3], sp[4], ss[0], ss[1], ss[2], ss[3], ss[4], ss[5])
```

```python
import functools

import numpy as np
import jax
import jax.numpy as jnp
from jax import lax
from jax.experimental import pallas as pl
from jax.experimental.pallas import tpu as pltpu

D_MODEL = 1024
DEPTH = 2
CHUNK = 64
HEAD_DIM = 64
N_Q_HEADS = 4
N_KV_HEADS = 2
KV_WIDTH = N_KV_HEADS * HEAD_DIM
W_ATTN = 256
W_CONV = 256
W_GMLP = 256
W_POOL = 256
WINDOW = 128
ROPE_THETA = 500000.0
ROT_DIM = 16
CONV_WIDTH = 3
GMLP_CHUNK = 128
GMLP_GROUPS = 4
GMLP_GROUP_DIM = 64
POOL_WINDOWS = (2, 4, 8, 16)
POOL_HIST = 15
D_FF = 2816
NORM_EPS = 1e-6
PROJ_WIDTH = 2048
PAST_LEN = 2048

LANES = 128
SUBLANES = 8
FF_CHUNK = 256
N_FF_CHUNKS = D_FF // FF_CHUNK
POOL_PAD = 24
ATTN_BLOCK = 2 * CHUNK
MIXER_SUB = 256
MIXER_STAGGER = 0
VMEM_LIMIT = 56 * 1024 * 1024

_Q0, _K0, _V0 = 0, 256, 384
_CB0, _CC0, _CH0 = 512, 768, 1024
_GU0, _GV0 = 1280, 1536
_P0 = 1792


def _rmsnorm(x, g):
    ms = jnp.mean(x * x, axis=-1, keepdims=True)
    return x * lax.rsqrt(ms + NORM_EPS) * g


def _dot(a, b):
    return jnp.dot(a, b, preferred_element_type=jnp.float32)


def _rope(x, cos_t, sin_lo, sin_hi):
    up = pltpu.roll(x, LANES - ROT_DIM // 2, axis=1)
    dn = pltpu.roll(x, ROT_DIM // 2, axis=1)
    return x * cos_t + up * sin_lo + dn * sin_hi


def _swap_halves(x):
    return pltpu.roll(x, HEAD_DIM, axis=1)


def _swap_row_halves(x):
    return jnp.concatenate([x[HEAD_DIM:], x[:HEAD_DIM]], axis=0)


def _mixer_kernel(*refs, tm, nt, pos0, layer):
    it = iter(refs)
    x_ref = next(it)
    cos_ref, sinlo_ref, sinhi_ref = next(it), next(it), next(it)
    bias_ref = next(it)
    gmix_ref, win_ref, sink_ref, convw_ref = next(it), next(it), next(it), next(it)
    lng_ref, lnb_ref, gw_ref, gb_ref = next(it), next(it), next(it), next(it)
    pw_ref, pscale_ref, wout_ref = next(it), next(it), next(it)
    xo_ref, kst_ref, vst_ref, cst_ref, pst_ref = next(it), next(it), next(it), next(it), next(it)
    kt, vext, kt2, vext2 = next(it), next(it), next(it), next(it)
    mext, pext, pa, pb, hb, mix = next(it), next(it), next(it), next(it), next(it), next(it)

    t = pl.program_id(1)
    qn = min(ATTN_BLOCK, tm)
    nkeys = WINDOW + qn
    kr = min(WINDOW, tm)
    gl = min(GMLP_CHUNK, tm)
    sub = min(MIXER_SUB, tm)

    @pl.when(t == 0)
    def _init():
        mext[0:SUBLANES, :] = jnp.zeros((SUBLANES, W_CONV), jnp.float32)
        pext[0:POOL_PAD, :] = jnp.zeros((POOL_PAD, W_POOL), jnp.float32)
        pa[0:SUBLANES, :] = jnp.zeros((SUBLANES, W_POOL), jnp.float32)
        pb[0:SUBLANES, :] = jnp.zeros((SUBLANES, W_POOL), jnp.float32)
        for buf in (vext, vext2):
            buf[0:WINDOW, :] = jnp.zeros((WINDOW, KV_WIDTH), jnp.bfloat16)
        for buf in (kt, kt2):
            buf[:, 0:WINDOW] = jnp.zeros((KV_WIDTH, WINDOW), jnp.bfloat16)

    lane = lax.broadcasted_iota(jnp.int32, (1, LANES), 1)
    lo = lane < HEAD_DIM
    lane256 = lax.broadcasted_iota(jnp.int32, (1, W_GMLP), 1) // GMLP_GROUP_DIM
    row4 = lax.broadcasted_iota(jnp.int32, (4 * qn, 1), 0)
    sk = jnp.where(row4 < qn, sink_ref[layer, 0],
                   jnp.where(row4 < 2 * qn, sink_ref[layer, 3],
                             jnp.where(row4 < 3 * qn, sink_ref[layer, 1], sink_ref[layer, 2])))
    gi = lax.broadcasted_iota(jnp.int32, (gl, gl), 0)
    gj = lax.broadcasted_iota(jnp.int32, (gl, gl), 1)
    tri = (gj // CHUNK) <= (gi // CHUNK)
    wcat = jnp.concatenate([jnp.where(tri, gw_ref[g, 0:gl, 0:gl], 0.0).astype(jnp.bfloat16)
                            for g in range(GMLP_GROUPS)], axis=1)
    gmask = [(lane256 == g).astype(jnp.bfloat16) for g in range(GMLP_GROUPS)]
    bias_g = gb_ref[0:gl, :]
    scale = HEAD_DIM ** -0.5

    def subtile(r0, last):
        rows = slice(r0, r0 + sub)
        krows = slice(WINDOW + r0, WINDOW + r0 + sub)
        x = x_ref[rows, :]
        hb[rows, :] = _rmsnorm(x, gmix_ref[...]).astype(jnp.bfloat16)
        zq = _dot(hb[rows, :], win_ref[:, _Q0:_CB0])
        trows = pl.ds(pl.multiple_of(t * tm + r0, SUBLANES), sub)
        cos_t, sin_lo, sin_hi = cos_ref[trows, :], sinlo_ref[trows, :], sinhi_ref[trows, :]
        qa = _rope(zq[:, 0:128], cos_t, sin_lo, sin_hi) * scale
        qb = _rope(zq[:, 128:256], cos_t, sin_lo, sin_hi) * scale
        k = _rope(zq[:, 256:384], cos_t, sin_lo, sin_hi)
        v = zq[:, 384:512]
        ktb = k.T.astype(jnp.bfloat16)
        kt[:, krows] = ktb
        kt2[:, krows] = _swap_row_halves(ktb)
        vext[krows, :] = v.astype(jnp.bfloat16)
        vext2[krows, :] = _swap_halves(v).astype(jnp.bfloat16)
        if last:
            kst_ref[...] = k[sub - kr:sub, :]
            vst_ref[...] = v[sub - kr:sub, :]
        yield
        scores = []
        for b in range(sub // qn):
            q0 = b * qn
            wrows = slice(r0 + q0, r0 + q0 + nkeys)
            qa_b, qb_b = qa[q0:q0 + qn, :], qb[q0:q0 + qn, :]
            q03 = jnp.concatenate([jnp.where(lo, qa_b, 0.0), jnp.where(lo, 0.0, qb_b)], axis=0)
            q12 = jnp.concatenate([jnp.where(lo, 0.0, qa_b), jnp.where(lo, qb_b, 0.0)], axis=0)
            s = jnp.concatenate([_dot(q03.astype(jnp.bfloat16), kt[:, wrows]),
                                 _dot(q12.astype(jnp.bfloat16), kt2[:, wrows])], axis=0)
            scores.append(s + (bias_ref[0] if r0 + q0 > 0 else bias_ref[jnp.where(t == 0, 1, 0)]))
        yield
        zc = _dot(hb[rows, :], win_ref[:, _CB0:_GU0])
        zg = _dot(hb[rows, :], win_ref[:, _GU0:_P0])
        zp = _dot(hb[rows, :], win_ref[:, _P0:PROJ_WIDTH])
        yield
        probs, dens = [], []
        for s in scores:
            m = jnp.maximum(jnp.max(s, axis=-1, keepdims=True), sk)
            p = jnp.exp(s - m)
            dens.append(jnp.sum(p, axis=-1, keepdims=True) + jnp.exp(sk - m))
            probs.append(p.astype(jnp.bfloat16))
        yield
        for b in range(sub // qn):
            q0 = r0 + b * qn
            wrows = slice(q0, q0 + nkeys)
            o03 = _dot(probs[b][0:2 * qn], vext[wrows, :]) / dens[b][0:2 * qn]
            o12 = _dot(probs[b][2 * qn:4 * qn], vext2[wrows, :]) / dens[b][2 * qn:4 * qn]
            oa = jnp.where(lo, o03[0:qn], o12[0:qn])
            ob = jnp.where(lo, o12[qn:2 * qn], o03[qn:2 * qn])
            mix[q0:q0 + qn, 0:128] = oa.astype(jnp.bfloat16)
            mix[q0:q0 + qn, 128:256] = ob.astype(jnp.bfloat16)
        yield
        mprod = zc[:, 256:512] * zc[:, 512:768]
        mext[SUBLANES + r0:SUBLANES + r0 + sub, :] = mprod
        cw = convw_ref[...]
        yc = (mext[SUBLANES - 2 + r0:SUBLANES - 2 + r0 + sub, :] * cw[0:1, :]
              + mext[SUBLANES - 1 + r0:SUBLANES - 1 + r0 + sub, :] * cw[1:2, :]
              + mprod * cw[2:3, :])
        mix[rows, 256:512] = (zc[:, 0:256] * yc).astype(jnp.bfloat16)
        if last:
            cst_ref[...] = mext[SUBLANES + tm - 2:SUBLANES + tm, :]

        u = jax.nn.gelu(zg[:, 0:256])
        gv = jax.nn.gelu(zg[:, 256:512])
        mu = jnp.mean(gv, axis=-1, keepdims=True)
        var = jnp.mean(jnp.square(gv - mu), axis=-1, keepdims=True)
        vn = (gv - mu) * lax.rsqrt(var + NORM_EPS) * lng_ref[...] + lnb_ref[...]
        vnb = vn.astype(jnp.bfloat16)
        for c in range(sub // gl):
            c0 = c * gl
            vc = vnb[c0:c0 + gl, :]
            s = _dot(wcat, jnp.concatenate([vc * gmask[g] for g in range(GMLP_GROUPS)], axis=0))
            mix[r0 + c0:r0 + c0 + gl, 512:768] = (u[c0:c0 + gl, :] * (s + bias_g)).astype(jnp.bfloat16)

        pext[POOL_PAD + r0:POOL_PAD + r0 + sub, :] = zp
        n = POOL_PAD - SUBLANES + sub
        lo_r = SUBLANES + r0
        pa[lo_r:lo_r + n, :] = pext[lo_r:lo_r + n, :] + pext[lo_r - 1:lo_r - 1 + n, :]
        pb[lo_r:lo_r + n, :] = pa[lo_r:lo_r + n, :] + pa[lo_r - 2:lo_r - 2 + n, :]
        tr = slice(POOL_PAD + r0, POOL_PAD + r0 + sub)
        s2 = pa[tr, 0:LANES]
        s4 = pb[tr, 0:LANES]
        pa[lo_r:lo_r + n, LANES:] = pb[lo_r:lo_r + n, LANES:] + pb[lo_r - 4:lo_r - 4 + n, LANES:]
        s8 = pa[tr, LANES:]
        s16 = s8 + pa[POOL_PAD - 8 + r0:POOL_PAD - 8 + r0 + sub, LANES:]
        wsum = jnp.concatenate([jnp.where(lo, s2, s4), jnp.where(lo, s8, s16)], axis=1)
        prow = lax.broadcasted_iota(jnp.int32, (sub, 1), 0) + (pos0 + 1 + r0) + t * tm
        winl = jnp.left_shift(2, lane256).astype(jnp.float32)
        cnt = jnp.minimum(prow.astype(jnp.float32), winl)
        pooled = wsum / cnt - zp
        mix[rows, 768:1024] = (_dot(pooled.astype(jnp.bfloat16), pw_ref[...]) * pscale_ref[...]).astype(jnp.bfloat16)
        if last:
            pst_ref[...] = pext[POOL_PAD + tm - POOL_HIST:POOL_PAD + tm, :]
        yield
        xo_ref[rows, :] = x_ref[rows, :] + _dot(mix[rows, :], wout_ref[...])
        yield

    nsub = tm // sub
    gens = [subtile(i * sub, i == nsub - 1) for i in range(nsub)]
    n_stages = 7
    for step in range(n_stages + MIXER_STAGGER * (nsub - 1)):
        for i, g in enumerate(gens):
            if 0 <= step - MIXER_STAGGER * i < n_stages:
                next(g)

    if nt > 1:
        for buf in (vext, vext2):
            buf[0:WINDOW, :] = buf[tm:tm + WINDOW, :]
        for buf in (kt, kt2):
            buf[:, 0:WINDOW] = buf[:, tm:tm + WINDOW]
        mext[0:SUBLANES, :] = mext[tm:tm + SUBLANES, :]
        pext[0:POOL_PAD, :] = pext[tm:tm + POOL_PAD, :]


def _const_spec(shape):
    nd = len(shape)
    return pl.BlockSpec(shape, lambda b, t, _nd=nd: (0,) * _nd, pipeline_mode=pl.Buffered(1))


def _layer_spec(stacked_shape, l):
    nd = len(stacked_shape) - 1
    return pl.BlockSpec((None,) + tuple(stacked_shape[1:]), lambda b, t, _nd=nd, _l=l: (_l,) + (0,) * _nd,
                        pipeline_mode=pl.Buffered(1))


def _cast_operands(args):
    return [a.dtype == jnp.bfloat16 for a in args]


def _window_bias(qn):
    rows = (np.arange(4 * qn) % qn) // CHUNK
    keys = np.arange(WINDOW + qn)
    ok = (keys[None, :] >= rows[:, None] * CHUNK) & (keys[None, :] < (rows[:, None] + 3) * CHUNK)
    first = ok & (keys[None, :] >= WINDOW)
    return np.where(np.stack([ok, first]), 0.0, -np.inf).astype(np.float32)


def _mixer_call(x, tabs, lw, l, *, tm, pos0):
    bsz, t_len, _ = x.shape
    nt = t_len // tm
    assert tm % MIXER_SUB == 0 and MIXER_SUB % ATTN_BLOCK == 0
    kr = min(WINDOW, tm)
    qn = min(ATTN_BLOCK, tm)
    in_specs = [pl.BlockSpec((None, tm, D_MODEL), lambda b, t: (b, t, 0))]
    in_specs += [_const_spec(tabs[0].shape)] * 3
    bias = jnp.asarray(_window_bias(qn))
    in_specs.append(_const_spec(bias.shape))
    args = [x, *tabs, bias]
    wnames = ('g_mix', 'w_in', 'sink', 'conv_w', 'ln_g', 'ln_b', 'gmlp_w', 'gmlp_b', 'pool_w', 'pool_scale', 'w_out')
    for n in wnames:
        a = lw[n]
        if n == 'sink':
            in_specs.append(pl.BlockSpec(memory_space=pltpu.SMEM))
        else:
            in_specs.append(_layer_spec(a.shape, l))
        args.append(a)
    out_shape = [jax.ShapeDtypeStruct((bsz, t_len, D_MODEL), jnp.float32),
                 jax.ShapeDtypeStruct((bsz, kr, KV_WIDTH), jnp.float32),
                 jax.ShapeDtypeStruct((bsz, kr, KV_WIDTH), jnp.float32),
                 jax.ShapeDtypeStruct((bsz, CONV_WIDTH - 1, W_CONV), jnp.float32),
                 jax.ShapeDtypeStruct((bsz, POOL_HIST, W_POOL), jnp.float32)]
    out_specs = [pl.BlockSpec((None, tm, D_MODEL), lambda b, t: (b, t, 0)),
                 pl.BlockSpec((None, kr, KV_WIDTH), lambda b, t: (b, 0, 0)),
                 pl.BlockSpec((None, kr, KV_WIDTH), lambda b, t: (b, 0, 0)),
                 pl.BlockSpec((None, CONV_WIDTH - 1, W_CONV), lambda b, t: (b, 0, 0)),
                 pl.BlockSpec((None, POOL_HIST, W_POOL), lambda b, t: (b, 0, 0))]
    scratch = [pltpu.VMEM((KV_WIDTH, WINDOW + tm), jnp.bfloat16), pltpu.VMEM((WINDOW + tm, KV_WIDTH), jnp.bfloat16)] * 2
    scratch += [pltpu.VMEM((SUBLANES + tm, W_CONV), jnp.float32),
                pltpu.VMEM((POOL_PAD + tm, W_POOL), jnp.float32),
                pltpu.VMEM((POOL_PAD + tm, W_POOL), jnp.float32),
                pltpu.VMEM((POOL_PAD + tm, W_POOL), jnp.float32),
                pltpu.VMEM((tm, D_MODEL), jnp.bfloat16),
                pltpu.VMEM((tm, D_MODEL), jnp.bfloat16)]
    return pl.pallas_call(
        functools.partial(_mixer_kernel, tm=tm, nt=nt, pos0=pos0, layer=l),
        grid=(bsz, nt),
        in_specs=in_specs,
        out_specs=out_specs,
        out_shape=out_shape,
        scratch_shapes=scratch,
        compiler_params=pltpu.CompilerParams(
            dimension_semantics=("arbitrary", "arbitrary"), vmem_limit_bytes=VMEM_LIMIT,
            allow_input_fusion=_cast_operands(args)),
        name="mixer",
    )(*args)


def _mixer_sample_kernel(*refs, nseq, ts, pos0, layer):
    it = iter(refs)
    x_ref = next(it)
    cos_ref, sinlo_ref, sinhi_ref = next(it), next(it), next(it)
    kpast_ref, vpast_ref, cpast_ref, ppast_ref = next(it), next(it), next(it), next(it)
    gmix_ref, win_ref, sink_ref, convw_ref = next(it), next(it), next(it), next(it)
    lng_ref, lnb_ref, gw_ref, gb_ref = next(it), next(it), next(it), next(it)
    pw_ref, pscale_ref, wout_ref = next(it), next(it), next(it)
    xo_ref, kst_ref, vst_ref, cst_ref, pst_ref, grows_ref = (next(it), next(it), next(it), next(it), next(it),
                                                             next(it))
    mext, pext, pa, pb, hb, mix = next(it), next(it), next(it), next(it), next(it), next(it)

    lane = lax.broadcasted_iota(jnp.int32, (1, LANES), 1)
    lo = lane < HEAD_DIM
    lane256 = lax.broadcasted_iota(jnp.int32, (1, W_GMLP), 1) // GMLP_GROUP_DIM
    row4 = lax.broadcasted_iota(jnp.int32, (4 * ts, 1), 0)
    sk = jnp.where(row4 < ts, sink_ref[layer, 0],
                   jnp.where(row4 < 2 * ts, sink_ref[layer, 3],
                             jnp.where(row4 < 3 * ts, sink_ref[layer, 1], sink_ref[layer, 2])))
    wcat = jnp.concatenate([gw_ref[g, 0:ts, 0:ts].astype(jnp.bfloat16) for g in range(GMLP_GROUPS)], axis=1)
    gmask = [(lane256 == g).astype(jnp.bfloat16) for g in range(GMLP_GROUPS)]
    bias_g = gb_ref[0:ts, :]
    scale = HEAD_DIM ** -0.5

    x = x_ref[...]
    hb[...] = _rmsnorm(x, gmix_ref[...]).astype(jnp.bfloat16)
    zq = _dot(hb[...], win_ref[:, _Q0:_CB0])
    zc = _dot(hb[...], win_ref[:, _CB0:_GU0])
    zg = _dot(hb[...], win_ref[:, _GU0:_P0])
    zp = _dot(hb[...], win_ref[:, _P0:PROJ_WIDTH])
    cos_t, sin_lo, sin_hi = cos_ref[...], sinlo_ref[...], sinhi_ref[...]
    qa = _rope(zq[:, 0:128], cos_t, sin_lo, sin_hi) * scale
    qb = _rope(zq[:, 128:256], cos_t, sin_lo, sin_hi) * scale
    k = _rope(zq[:, 256:384], cos_t, sin_lo, sin_hi)
    v = zq[:, 384:512]
    kst_ref[...] = k
    vst_ref[...] = v
    mprod = zc[:, 256:512] * zc[:, 512:768]
    u = jax.nn.gelu(zg[:, 0:256])
    gv = jax.nn.gelu(zg[:, 256:512])
    mu = jnp.mean(gv, axis=-1, keepdims=True)
    var = jnp.mean(jnp.square(gv - mu), axis=-1, keepdims=True)
    vn = (gv - mu) * lax.rsqrt(var + NORM_EPS) * lng_ref[...] + lnb_ref[...]
    grows_ref[...] = vn
    vnb = vn.astype(jnp.bfloat16)
    cw = convw_ref[...]
    winl = jnp.left_shift(2, lane256).astype(jnp.float32)
    prow = lax.broadcasted_iota(jnp.int32, (ts, 1), 0) + (pos0 + 1)
    cnt = jnp.minimum(prow.astype(jnp.float32), winl)
    pooled = []
    for i in range(nseq):
        r = slice(i * ts, (i + 1) * ts)
        kp, vp = kpast_ref[i], vpast_ref[i]
        ktw = jnp.concatenate([kp.T, k[r].T], axis=1).astype(jnp.bfloat16)
        vw = jnp.concatenate([vp, v[r]], axis=0)
        q03 = jnp.concatenate([jnp.where(lo, qa[r], 0.0), jnp.where(lo, 0.0, qb[r])], axis=0)
        q12 = jnp.concatenate([jnp.where(lo, 0.0, qa[r]), jnp.where(lo, qb[r], 0.0)], axis=0)
        s = jnp.concatenate([_dot(q03.astype(jnp.bfloat16), ktw),
                             _dot(q12.astype(jnp.bfloat16), _swap_row_halves(ktw))], axis=0)
        m = jnp.maximum(jnp.max(s, axis=-1, keepdims=True), sk)
        p = jnp.exp(s - m)
        den = jnp.sum(p, axis=-1, keepdims=True) + jnp.exp(sk - m)
        pb16 = p.astype(jnp.bfloat16)
        o03 = _dot(pb16[0:2 * ts], vw.astype(jnp.bfloat16)) / den[0:2 * ts]
        o12 = _dot(pb16[2 * ts:4 * ts], _swap_halves(vw).astype(jnp.bfloat16)) / den[2 * ts:4 * ts]
        mix[r, 0:128] = jnp.where(lo, o03[0:ts], o12[0:ts]).astype(jnp.bfloat16)
        mix[r, 128:256] = jnp.where(lo, o12[ts:2 * ts], o03[ts:2 * ts]).astype(jnp.bfloat16)
        mext[i, 0:SUBLANES, :] = jnp.zeros((SUBLANES, W_CONV), jnp.float32)
        mext[i, SUBLANES - (CONV_WIDTH - 1):SUBLANES, :] = cpast_ref[i]
        mext[i, SUBLANES:SUBLANES + ts, :] = mprod[r]
        yc = (mext[i, SUBLANES - 2:SUBLANES - 2 + ts, :] * cw[0:1, :]
              + mext[i, SUBLANES - 1:SUBLANES - 1 + ts, :] * cw[1:2, :] + mprod[r] * cw[2:3, :])
        mix[r, 256:512] = (zc[r, 0:256] * yc).astype(jnp.bfloat16)
        cst_ref[i] = mext[i, SUBLANES + ts - 2:SUBLANES + ts, :]
        vc = vnb[r]
        sg = _dot(wcat, jnp.concatenate([vc * gmask[g] for g in range(GMLP_GROUPS)], axis=0))
        mix[r, 512:768] = (u[r] * (sg + bias_g)).astype(jnp.bfloat16)
        pext[i, 0:POOL_PAD, :] = jnp.zeros((POOL_PAD, W_POOL), jnp.float32)
        pext[i, POOL_PAD - POOL_HIST:POOL_PAD, :] = ppast_ref[i]
        pext[i, POOL_PAD:POOL_PAD + ts, :] = zp[r]
        pa[i, 0:SUBLANES, :] = jnp.zeros((SUBLANES, W_POOL), jnp.float32)
        pb[i, 0:SUBLANES, :] = jnp.zeros((SUBLANES, W_POOL), jnp.float32)
        n = POOL_PAD - SUBLANES + ts
        b0 = SUBLANES
        pa[i, b0:b0 + n, :] = pext[i, b0:b0 + n, :] + pext[i, b0 - 1:b0 - 1 + n, :]
        pb[i, b0:b0 + n, :] = pa[i, b0:b0 + n, :] + pa[i, b0 - 2:b0 - 2 + n, :]
        s2 = pa[i, POOL_PAD:POOL_PAD + ts, 0:LANES]
        s4 = pb[i, POOL_PAD:POOL_PAD + ts, 0:LANES]
        pa[i, b0:b0 + n, LANES:] = pb[i, b0:b0 + n, LANES:] + pb[i, b0 - 4:b0 - 4 + n, LANES:]
        s8 = pa[i, POOL_PAD:POOL_PAD + ts, LANES:]
        s16 = s8 + pa[i, POOL_PAD - 8:POOL_PAD - 8 + ts, LANES:]
        wsum = jnp.concatenate([jnp.where(lo, s2, s4), jnp.where(lo, s8, s16)], axis=1)
        pooled.append(wsum / cnt - zp[r])
        pst_ref[i] = pext[i, POOL_PAD + ts - POOL_HIST:POOL_PAD + ts, :]
    pooled = jnp.concatenate(pooled, axis=0).astype(jnp.bfloat16)
    mix[:, 768:1024] = (_dot(pooled, pw_ref[...]) * pscale_ref[...]).astype(jnp.bfloat16)
    xo_ref[...] = x + _dot(mix[...], wout_ref[...])


def _mixer_sample_call(x, tabs, past, lw, l, *, pos0):
    nseq, ts, _ = x.shape
    assert ts <= CHUNK and ts % SUBLANES == 0
    tm = nseq * ts
    zero3 = lambda b: (0, 0, 0)
    in_specs = [pl.BlockSpec((None, tm, D_MODEL), zero3)]
    in_specs += [pl.BlockSpec((tm, LANES), lambda b: (0, 0))] * 3
    args = [x.reshape(1, tm, D_MODEL)] + [jnp.tile(a, (nseq, 1)) for a in tabs]
    for a in past:
        in_specs.append(pl.BlockSpec((nseq, None) + a.shape[2:], lambda b, _l=l: (0, _l, 0, 0)))
        args.append(a)
    wnames = ('g_mix', 'w_in', 'sink', 'conv_w', 'ln_g', 'ln_b', 'gmlp_w', 'gmlp_b', 'pool_w', 'pool_scale', 'w_out')
    for n in wnames:
        a = lw[n]
        if n == 'sink':
            in_specs.append(pl.BlockSpec(memory_space=pltpu.SMEM))
        else:
            nd = a.ndim - 1
            in_specs.append(pl.BlockSpec((None,) + a.shape[1:], lambda b, _nd=nd, _l=l: (_l,) + (0,) * _nd))
        args.append(a)
    out_shape = [jax.ShapeDtypeStruct((1, tm, D_MODEL), jnp.float32),
                 jax.ShapeDtypeStruct((tm, KV_WIDTH), jnp.float32),
                 jax.ShapeDtypeStruct((tm, KV_WIDTH), jnp.float32),
                 jax.ShapeDtypeStruct((nseq, CONV_WIDTH - 1, W_CONV), jnp.float32),
                 jax.ShapeDtypeStruct((nseq, POOL_HIST, W_POOL), jnp.float32),
                 jax.ShapeDtypeStruct((tm, W_GMLP), jnp.float32)]
    out_specs = [pl.BlockSpec((None, tm, D_MODEL), zero3),
                 pl.BlockSpec((tm, KV_WIDTH), lambda b: (0, 0)),
                 pl.BlockSpec((tm, KV_WIDTH), lambda b: (0, 0)),
                 pl.BlockSpec((nseq, CONV_WIDTH - 1, W_CONV), zero3),
                 pl.BlockSpec((nseq, POOL_HIST, W_POOL), zero3),
                 pl.BlockSpec((tm, W_GMLP), lambda b: (0, 0))]
    scratch = [pltpu.VMEM((nseq, SUBLANES + ts, W_CONV), jnp.float32),
               pltpu.VMEM((nseq, POOL_PAD + ts, W_POOL), jnp.float32),
               pltpu.VMEM((nseq, POOL_PAD + ts, W_POOL), jnp.float32),
               pltpu.VMEM((nseq, POOL_PAD + ts, W_POOL), jnp.float32),
               pltpu.VMEM((tm, D_MODEL), jnp.bfloat16),
               pltpu.VMEM((tm, D_MODEL), jnp.bfloat16)]
    xo, kst, vst, cst, pst, grows = pl.pallas_call(
        functools.partial(_mixer_sample_kernel, nseq=nseq, ts=ts, pos0=pos0, layer=l),
        grid=(1,),
        in_specs=in_specs,
        out_specs=out_specs,
        out_shape=out_shape,
        scratch_shapes=scratch,
        compiler_params=pltpu.CompilerParams(dimension_semantics=("arbitrary",), vmem_limit_bytes=VMEM_LIMIT,
                                             allow_input_fusion=_cast_operands(args)),
        name="mixer_past",
    )(*args)
    return (xo.reshape(nseq, ts, D_MODEL), kst.reshape(nseq, ts, KV_WIDTH), vst.reshape(nseq, ts, KV_WIDTH),
            cst, pst, grows.reshape(nseq, ts, W_GMLP))


def _ffn_kernel(*refs, tm, nt, nsteps, chained, final):
    it = iter(refs)
    x_ref = next(it)
    if not chained:
        past_ref = next(it)
    gffn_ref, wup_ref, fcw_ref, wdown_ref = next(it), next(it), next(it), next(it)
    if final:
        gfin_ref = next(it)
    xo_ref, fst_ref = next(it), next(it)
    carry, hbuf, act = next(it), next(it), next(it)
    if chained:
        xbuf, obuf, sem_in, sem_out = next(it), next(it), next(it), next(it)

    t = pl.program_id(1)
    fc = FF_CHUNK
    halves = (0, D_FF)
    seg = tm // SUBLANES

    if chained:
        step = pl.program_id(0) * nt + t
        slot = step % 2

        def tile_copies(tile, sl, outward):
            tb, row0 = tile // nt, (tile % nt) * tm
            cps = []
            for s in range(SUBLANES):
                hbm = (xo_ref if outward else x_ref).at[tb, pl.ds(row0 + s * seg, seg), :]
                if outward:
                    cps.append(pltpu.make_async_copy(obuf.at[sl, :, s, :], hbm, sem_out.at[sl, s]))
                else:
                    cps.append(pltpu.make_async_copy(hbm, xbuf.at[sl, :, s, :], sem_in.at[sl, s]))
            return cps

        @pl.when(step == 0)
        def _first_in():
            for cp in tile_copies(0, 0, False):
                cp.start()

        @pl.when(step + 1 < nsteps)
        def _next_in():
            for cp in tile_copies(step + 1, 1 - slot, False):
                cp.start()

        @pl.when(step >= 2)
        def _free_out():
            for cp in tile_copies(step - 2, slot, True):
                cp.wait()

        for cp in tile_copies(step, slot, False):
            cp.wait()

        @pl.when(t == 0)
        def _init():
            carry[...] = jnp.zeros(carry.shape, jnp.float32)

        xp = xbuf[slot].reshape(tm, D_MODEL)
    else:
        xp = jnp.swapaxes(x_ref[...].reshape(SUBLANES, seg, D_MODEL), 0, 1).reshape(tm, D_MODEL)
    hbuf[...] = _rmsnorm(xp, gffn_ref[...]).astype(jnp.bfloat16)
    sub = lax.broadcasted_iota(jnp.int32, (SUBLANES, fc), 0)
    for c in range(N_FF_CHUNKS):
        ys = []
        for hi, off in enumerate(halves):
            cols = slice(off + c * fc, off + (c + 1) * fc)
            lanes = slice(hi * fc, (hi + 1) * fc)
            up = _dot(hbuf[...], wup_ref[:, cols])
            last1, last2 = up[tm - SUBLANES:tm], up[tm - 2 * SUBLANES:tm - SUBLANES]
            if chained:
                p1 = jnp.where(sub == 0, pltpu.roll(carry[c, 1, :, lanes], 1, axis=0), pltpu.roll(last1, 1, axis=0))
                p2 = jnp.where(sub == 0, pltpu.roll(carry[c, 0, :, lanes], 1, axis=0), pltpu.roll(last2, 1, axis=0))
                fst_ref[0:1, cols] = last2[SUBLANES - 1:SUBLANES]
                fst_ref[1:2, cols] = last1[SUBLANES - 1:SUBLANES]
                if nt > 1:
                    carry[c, 0, :, lanes] = last2
                    carry[c, 1, :, lanes] = last1
            else:
                p2, p1 = past_ref[0, :, cols], past_ref[1, :, cols]
                fst_ref[0, :, cols] = last2
                fst_ref[1, :, cols] = last1
            d1 = jnp.concatenate([p1, up[0:tm - SUBLANES]], axis=0)
            d2 = jnp.concatenate([p2, p1, up[0:tm - 2 * SUBLANES]], axis=0)
            cw = fcw_ref[:, cols]
            ys.append(d2 * cw[0:1, :] + d1 * cw[1:2, :] + up * cw[2:3, :])
        gate, val = ys
        act[:, c * fc:(c + 1) * fc] = (gate * jax.nn.sigmoid(gate) * val).astype(jnp.bfloat16)
    down = _dot(act[...], wdown_ref[...])
    if chained:
        out = xbuf[slot].reshape(tm, D_MODEL) + down
    else:
        out = xp + down
    if final:
        out = _rmsnorm(out, gfin_ref[...])
    if chained:
        obuf[slot] = out.reshape(seg, SUBLANES, D_MODEL)
        for cp in tile_copies(step, slot, True):
            cp.start()

        @pl.when(step == nsteps - 1)
        def _drain():
            if nsteps > 1:
                for cp in tile_copies(step - 1, 1 - slot, True):
                    cp.wait()
            for cp in tile_copies(step, slot, True):
                cp.wait()
    else:
        xo_ref[...] = jnp.swapaxes(out.reshape(seg, SUBLANES, D_MODEL), 0, 1).reshape(tm, D_MODEL)


def _ffn_call(x, fpast, lw, l, g_final, *, tm):
    bsz, t_len, _ = x.shape
    chained = fpast is None
    final = g_final is not None
    args = []
    if chained:
        nt = t_len // tm
        grid = (bsz, nt)
        st_shape = (bsz, CONV_WIDTH - 1, 2 * D_FF)
        st_spec = pl.BlockSpec((None, CONV_WIDTH - 1, 2 * D_FF), lambda b, t: (b, 0, 0))
        in_specs = [pl.BlockSpec(memory_space=pl.ANY)]
        x_out_spec = pl.BlockSpec(memory_space=pl.ANY)
    else:
        assert bsz == SUBLANES and t_len % SUBLANES == 0
        x = x.reshape(1, bsz * t_len, D_MODEL)
        tm, nt, grid = bsz * t_len, 1, (1, 1)
        st_shape = (CONV_WIDTH - 1, bsz, 2 * D_FF)
        st_spec = pl.BlockSpec(st_shape, lambda b, t: (0, 0, 0))
        in_specs = [pl.BlockSpec((None, tm, D_MODEL), lambda b, t: (b, t, 0)),
                    pl.BlockSpec((None,) + st_shape, lambda b, t, _l=l: (_l, 0, 0, 0))]
        x_out_spec = pl.BlockSpec((None, tm, D_MODEL), lambda b, t: (b, t, 0))
        args.append(fpast)
    assert tm % (SUBLANES * SUBLANES) == 0
    seg = tm // SUBLANES
    args = [x] + args
    for n in ('g_ffn', 'w_up', 'ffn_conv_w', 'w_down'):
        in_specs.append(_layer_spec(lw[n].shape, l))
        args.append(lw[n])
    if final:
        in_specs.append(_const_spec(g_final.shape))
        args.append(g_final)
    out_shape = [jax.ShapeDtypeStruct(x.shape, jnp.float32), jax.ShapeDtypeStruct(st_shape, jnp.float32)]
    out_specs = [x_out_spec, st_spec]
    scratch = [pltpu.VMEM((N_FF_CHUNKS, CONV_WIDTH - 1, SUBLANES, 2 * FF_CHUNK), jnp.float32),
               pltpu.VMEM((tm, D_MODEL), jnp.bfloat16),
               pltpu.VMEM((tm, D_FF), jnp.bfloat16)]
    if chained:
        scratch += [pltpu.VMEM((2, seg, SUBLANES, D_MODEL), jnp.float32),
                    pltpu.VMEM((2, seg, SUBLANES, D_MODEL), jnp.float32),
                    pltpu.SemaphoreType.DMA((2, SUBLANES)),
                    pltpu.SemaphoreType.DMA((2, SUBLANES))]
    y, fst = pl.pallas_call(
        functools.partial(_ffn_kernel, tm=tm, nt=nt, nsteps=grid[0] * grid[1], chained=chained, final=final),
        grid=grid,
        in_specs=in_specs,
        out_specs=out_specs,
        out_shape=out_shape,
        scratch_shapes=scratch,
        compiler_params=pltpu.CompilerParams(
            dimension_semantics=("arbitrary", "arbitrary"), vmem_limit_bytes=VMEM_LIMIT,
            allow_input_fusion=_cast_operands(args)),
        name=("ffn" if chained else "ffn_past") + ("_final" if final else ""),
    )(*args)
    if not chained:
        y, fst = y.reshape(bsz, t_len, D_MODEL), jnp.swapaxes(fst, 0, 1)
    return y, fst


def _rope_tables(pos0, t_len):
    half = ROT_DIM // 2
    inv_freq = np.power(np.float64(ROPE_THETA), -np.arange(half, dtype=np.float64) * (2.0 / ROT_DIM))
    ang = (pos0 + np.arange(t_len, dtype=np.float64))[:, None] * inv_freq[None, :]
    cos, sin = np.cos(ang), np.sin(ang)
    rest = HEAD_DIM - ROT_DIM
    cos_h = np.concatenate([cos, cos, np.ones((t_len, rest))], axis=1)
    sinlo_h = np.concatenate([-sin, np.zeros((t_len, half + rest))], axis=1)
    sinhi_h = np.concatenate([np.zeros((t_len, half)), sin, np.zeros((t_len, rest))], axis=1)
    return tuple(jnp.asarray(np.concatenate([a, a], axis=1), dtype=jnp.float32) for a in (cos_h, sinlo_h, sinhi_h))


def _stacked_weights(g_mix, w_in, attn_sink, conv_w, gmlp_ln_g, gmlp_ln_b, gmlp_w, gmlp_b, pool_w, pool_scale,
                     w_out, g_ffn, w_up, ffn_conv_w, w_down):
    eye = jnp.eye(len(POOL_WINDOWS), dtype=pool_w.dtype)
    pw = jnp.einsum('lgij,gh->lgihj', pool_w, eye).reshape(DEPTH, W_POOL, W_POOL)
    return {
        'g_mix': g_mix[:, None, :],
        'w_in': w_in.astype(jnp.bfloat16),
        'sink': attn_sink,
        'conv_w': conv_w,
        'ln_g': gmlp_ln_g[:, None, :],
        'ln_b': gmlp_ln_b[:, None, :],
        'gmlp_w': gmlp_w,
        'gmlp_b': jnp.repeat(jnp.swapaxes(gmlp_b, 1, 2), GMLP_GROUP_DIM, axis=2),
        'pool_w': pw.astype(jnp.bfloat16),
        'pool_scale': pool_scale[:, None, :],
        'w_out': w_out.astype(jnp.bfloat16),
        'g_ffn': g_ffn[:, None, :],
        'w_up': w_up.astype(jnp.bfloat16),
        'ffn_conv_w': ffn_conv_w,
        'w_down': w_down.astype(jnp.bfloat16),
    }


def _run(x, pos0, pasts, lw, g_final, tm):
    bsz, t_len, _ = x.shape
    tabs = _rope_tables(pos0, t_len)
    if pasts is not None:
        ck, cv, sc, sp, sf = pasts
        mpast = (ck.reshape(bsz, DEPTH, WINDOW, KV_WIDTH), cv.reshape(bsz, DEPTH, WINDOW, KV_WIDTH), sc, sp)
        fpast = jnp.transpose(sf, (1, 2, 0, 3))
    else:
        mpast, fpast = None, None
    states = []
    for l in range(DEPTH):
        if pasts is None:
            mo = _mixer_call(x, tabs, lw, l, tm=tm, pos0=pos0)
        else:
            mo = _mixer_sample_call(x, tabs, mpast, lw, l, pos0=pos0)
        x_mid, kst, vst, cst, pst = mo[:5]
        gfin = g_final[None, :] if l == DEPTH - 1 else None
        x, fst = _ffn_call(x_mid, fpast, lw, l, gfin, tm=tm)
        st = [kst.reshape(bsz, -1, N_KV_HEADS, HEAD_DIM), vst.reshape(bsz, -1, N_KV_HEADS, HEAD_DIM),
              cst, pst, fst]
        if pasts is not None:
            st.append(mo[5])
        states.append(st)
    stacked = [jnp.stack([st[i] for st in states], axis=1) for i in range(len(states[0]))]
    return x, stacked


def kernel(x_prompt, x_sample, cache_attn_k, cache_attn_v, state_conv, state_pool, state_ffn_conv, g_mix, w_in,
           attn_sink, conv_w, gmlp_ln_g, gmlp_ln_b, gmlp_w, gmlp_b, pool_w, pool_scale, w_out, g_ffn, w_up,
           ffn_conv_w, w_down, g_final):
    lw = _stacked_weights(g_mix, w_in, attn_sink, conv_w, gmlp_ln_g, gmlp_ln_b, gmlp_w, gmlp_b, pool_w,
                          pool_scale, w_out, g_ffn, w_up, ffn_conv_w, w_down)
    y_prompt, sp = _run(x_prompt, 0, None, lw, g_final, tm=1024)
    y_sample, ss = _run(x_sample, PAST_LEN, (cache_attn_k, cache_attn_v, state_conv, state_pool, state_ffn_conv),
                        lw, g_final, tm=x_sample.shape[1])
    return (y_prompt, y_sample, sp[0], sp[1], sp[2], sp[3], sp[4], ss[0], ss[1], ss[2], ss[3], ss[4], ss[5])
```

```python
import functools

import numpy as np
import jax
import jax.numpy as jnp
from jax import lax
from jax.experimental import pallas as pl
from jax.experimental.pallas import tpu as pltpu

D_MODEL = 1024
DEPTH = 2
CHUNK = 64
HEAD_DIM = 64
N_Q_HEADS = 4
N_KV_HEADS = 2
KV_WIDTH = N_KV_HEADS * HEAD_DIM
W_ATTN = 256
W_CONV = 256
W_GMLP = 256
W_POOL = 256
WINDOW = 128
ROPE_THETA = 500000.0
ROT_DIM = 16
CONV_WIDTH = 3
GMLP_CHUNK = 128
GMLP_GROUPS = 4
GMLP_GROUP_DIM = 64
POOL_WINDOWS = (2, 4, 8, 16)
POOL_HIST = 15
D_FF = 2816
NORM_EPS = 1e-6
PROJ_WIDTH = 2048
PAST_LEN = 2048

LANES = 128
SUBLANES = 8
FF_CHUNK = 256
N_FF_CHUNKS = D_FF // FF_CHUNK
POOL_PAD = 24
ATTN_BLOCK = 2 * CHUNK
MIXER_SUB = 512
MIXER_STAGGER = 0
VMEM_LIMIT = 56 * 1024 * 1024

_Q0, _K0, _V0 = 0, 256, 384
_CB0, _CC0, _CH0 = 512, 768, 1024
_GU0, _GV0 = 1280, 1536
_P0 = 1792


def _rmsnorm(x, g):
    ms = jnp.mean(x * x, axis=-1, keepdims=True)
    return x * lax.rsqrt(ms + NORM_EPS) * g


def _dot(a, b):
    return jnp.dot(a, b, preferred_element_type=jnp.float32)


def _rope(x, cos_t, sin_lo, sin_hi):
    up = pltpu.roll(x, LANES - ROT_DIM // 2, axis=1)
    dn = pltpu.roll(x, ROT_DIM // 2, axis=1)
    return x * cos_t + up * sin_lo + dn * sin_hi


def _swap_halves(x):
    return pltpu.roll(x, HEAD_DIM, axis=1)


def _swap_row_halves(x):
    return jnp.concatenate([x[HEAD_DIM:], x[:HEAD_DIM]], axis=0)


def _mixer_kernel(*refs, tm, nt, pos0, layer):
    it = iter(refs)
    x_ref = next(it)
    cos_ref, sinlo_ref, sinhi_ref = next(it), next(it), next(it)
    bias_ref = next(it)
    gmix_ref, win_ref, sink_ref, convw_ref = next(it), next(it), next(it), next(it)
    lng_ref, lnb_ref, gw_ref, gb_ref = next(it), next(it), next(it), next(it)
    pw_ref, pscale_ref, wout_ref = next(it), next(it), next(it)
    xo_ref, kst_ref, vst_ref, cst_ref, pst_ref = next(it), next(it), next(it), next(it), next(it)
    kt, vext, kt2, vext2 = next(it), next(it), next(it), next(it)
    mext, pext, pa, pb, hb, mix = next(it), next(it), next(it), next(it), next(it), next(it)

    t = pl.program_id(1)
    qn = min(ATTN_BLOCK, tm)
    nkeys = WINDOW + qn
    kr = min(WINDOW, tm)
    gl = min(GMLP_CHUNK, tm)
    sub = min(MIXER_SUB, tm)

    @pl.when(t == 0)
    def _init():
        mext[0:SUBLANES, :] = jnp.zeros((SUBLANES, W_CONV), jnp.float32)
        pext[0:POOL_PAD, :] = jnp.zeros((POOL_PAD, W_POOL), jnp.float32)
        pa[0:SUBLANES, :] = jnp.zeros((SUBLANES, W_POOL), jnp.float32)
        pb[0:SUBLANES, :] = jnp.zeros((SUBLANES, W_POOL), jnp.float32)
        for buf in (vext, vext2):
            buf[0:WINDOW, :] = jnp.zeros((WINDOW, KV_WIDTH), jnp.bfloat16)
        for buf in (kt, kt2):
            buf[:, 0:WINDOW] = jnp.zeros((KV_WIDTH, WINDOW), jnp.bfloat16)

    lane = lax.broadcasted_iota(jnp.int32, (1, LANES), 1)
    lo = lane < HEAD_DIM
    lane256 = lax.broadcasted_iota(jnp.int32, (1, W_GMLP), 1) // GMLP_GROUP_DIM
    row4 = lax.broadcasted_iota(jnp.int32, (4 * qn, 1), 0)
    sk = jnp.where(row4 < qn, sink_ref[layer, 0],
                   jnp.where(row4 < 2 * qn, sink_ref[layer, 3],
                             jnp.where(row4 < 3 * qn, sink_ref[layer, 1], sink_ref[layer, 2])))
    gi = lax.broadcasted_iota(jnp.int32, (gl, gl), 0)
    gj = lax.broadcasted_iota(jnp.int32, (gl, gl), 1)
    tri = (gj // CHUNK) <= (gi // CHUNK)
    wcat = jnp.concatenate([jnp.where(tri, gw_ref[g, 0:gl, 0:gl], 0.0).astype(jnp.bfloat16)
                            for g in range(GMLP_GROUPS)], axis=1)
    gmask = [(lane256 == g).astype(jnp.bfloat16) for g in range(GMLP_GROUPS)]
    bias_g = gb_ref[0:gl, :]
    scale = HEAD_DIM ** -0.5

    def subtile(r0, last):
        rows = slice(r0, r0 + sub)
        krows = slice(WINDOW + r0, WINDOW + r0 + sub)
        x = x_ref[rows, :]
        hb[rows, :] = _rmsnorm(x, gmix_ref[...]).astype(jnp.bfloat16)
        zq = _dot(hb[rows, :], win_ref[:, _Q0:_CB0])
        trows = pl.ds(pl.multiple_of(t * tm + r0, SUBLANES), sub)
        cos_t, sin_lo, sin_hi = cos_ref[trows, :], sinlo_ref[trows, :], sinhi_ref[trows, :]
        qa = _rope(zq[:, 0:128], cos_t, sin_lo, sin_hi) * scale
        qb = _rope(zq[:, 128:256], cos_t, sin_lo, sin_hi) * scale
        k = _rope(zq[:, 256:384], cos_t, sin_lo, sin_hi)
        v = zq[:, 384:512]
        ktb = k.T.astype(jnp.bfloat16)
        kt[:, krows] = ktb
        kt2[:, krows] = _swap_row_halves(ktb)
        vext[krows, :] = v.astype(jnp.bfloat16)
        vext2[krows, :] = _swap_halves(v).astype(jnp.bfloat16)
        if last:
            kst_ref[...] = k[sub - kr:sub, :]
            vst_ref[...] = v[sub - kr:sub, :]
        yield
        scores = []
        for b in range(sub // qn):
            q0 = b * qn
            wrows = slice(r0 + q0, r0 + q0 + nkeys)
            qa_b, qb_b = qa[q0:q0 + qn, :], qb[q0:q0 + qn, :]
            q03 = jnp.concatenate([jnp.where(lo, qa_b, 0.0), jnp.where(lo, 0.0, qb_b)], axis=0)
            q12 = jnp.concatenate([jnp.where(lo, 0.0, qa_b), jnp.where(lo, qb_b, 0.0)], axis=0)
            s = jnp.concatenate([_dot(q03.astype(jnp.bfloat16), kt[:, wrows]),
                                 _dot(q12.astype(jnp.bfloat16), kt2[:, wrows])], axis=0)
            scores.append(s + (bias_ref[0] if r0 + q0 > 0 else bias_ref[jnp.where(t == 0, 1, 0)]))
        yield
        zc = _dot(hb[rows, :], win_ref[:, _CB0:_GU0])
        zg = _dot(hb[rows, :], win_ref[:, _GU0:_P0])
        zp = _dot(hb[rows, :], win_ref[:, _P0:PROJ_WIDTH])
        yield
        probs, dens = [], []
        for s in scores:
            m = jnp.maximum(jnp.max(s, axis=-1, keepdims=True), sk)
            p = jnp.exp(s - m)
            dens.append(jnp.sum(p, axis=-1, keepdims=True) + jnp.exp(sk - m))
            probs.append(p.astype(jnp.bfloat16))
        yield
        for b in range(sub // qn):
            q0 = r0 + b * qn
            wrows = slice(q0, q0 + nkeys)
            o03 = _dot(probs[b][0:2 * qn], vext[wrows, :]) / dens[b][0:2 * qn]
            o12 = _dot(probs[b][2 * qn:4 * qn], vext2[wrows, :]) / dens[b][2 * qn:4 * qn]
            oa = jnp.where(lo, o03[0:qn], o12[0:qn])
            ob = jnp.where(lo, o12[qn:2 * qn], o03[qn:2 * qn])
            mix[q0:q0 + qn, 0:128] = oa.astype(jnp.bfloat16)
            mix[q0:q0 + qn, 128:256] = ob.astype(jnp.bfloat16)
        yield
        mprod = zc[:, 256:512] * zc[:, 512:768]
        mext[SUBLANES + r0:SUBLANES + r0 + sub, :] = mprod
        cw = convw_ref[...]
        yc = (mext[SUBLANES - 2 + r0:SUBLANES - 2 + r0 + sub, :] * cw[0:1, :]
              + mext[SUBLANES - 1 + r0:SUBLANES - 1 + r0 + sub, :] * cw[1:2, :]
              + mprod * cw[2:3, :])
        mix[rows, 256:512] = (zc[:, 0:256] * yc).astype(jnp.bfloat16)
        if last:
            cst_ref[...] = mext[SUBLANES + tm - 2:SUBLANES + tm, :]

        u = jax.nn.gelu(zg[:, 0:256])
        gv = jax.nn.gelu(zg[:, 256:512])
        mu = jnp.mean(gv, axis=-1, keepdims=True)
        var = jnp.mean(jnp.square(gv - mu), axis=-1, keepdims=True)
        vn = (gv - mu) * lax.rsqrt(var + NORM_EPS) * lng_ref[...] + lnb_ref[...]
        vnb = vn.astype(jnp.bfloat16)
        for c in range(sub // gl):
            c0 = c * gl
            vc = vnb[c0:c0 + gl, :]
            s = _dot(wcat, jnp.concatenate([vc * gmask[g] for g in range(GMLP_GROUPS)], axis=0))
            mix[r0 + c0:r0 + c0 + gl, 512:768] = (u[c0:c0 + gl, :] * (s + bias_g)).astype(jnp.bfloat16)

        pext[POOL_PAD + r0:POOL_PAD + r0 + sub, :] = zp
        n = POOL_PAD - SUBLANES + sub
        lo_r = SUBLANES + r0
        pa[lo_r:lo_r + n, :] = pext[lo_r:lo_r + n, :] + pext[lo_r - 1:lo_r - 1 + n, :]
        pb[lo_r:lo_r + n, :] = pa[lo_r:lo_r + n, :] + pa[lo_r - 2:lo_r - 2 + n, :]
        tr = slice(POOL_PAD + r0, POOL_PAD + r0 + sub)
        s2 = pa[tr, 0:LANES]
        s4 = pb[tr, 0:LANES]
        pa[lo_r:lo_r + n, LANES:] = pb[lo_r:lo_r + n, LANES:] + pb[lo_r - 4:lo_r - 4 + n, LANES:]
        s8 = pa[tr, LANES:]
        s16 = s8 + pa[POOL_PAD - 8 + r0:POOL_PAD - 8 + r0 + sub, LANES:]
        wsum = jnp.concatenate([jnp.where(lo, s2, s4), jnp.where(lo, s8, s16)], axis=1)
        prow = lax.broadcasted_iota(jnp.int32, (sub, 1), 0) + (pos0 + 1 + r0) + t * tm
        winl = jnp.left_shift(2, lane256).astype(jnp.float32)
        cnt = jnp.minimum(prow.astype(jnp.float32), winl)
        pooled = wsum / cnt - zp
        mix[rows, 768:1024] = (_dot(pooled.astype(jnp.bfloat16), pw_ref[...]) * pscale_ref[...]).astype(jnp.bfloat16)
        if last:
            pst_ref[...] = pext[POOL_PAD + tm - POOL_HIST:POOL_PAD + tm, :]
        yield
        xo_ref[rows, :] = x_ref[rows, :] + _dot(mix[rows, :], wout_ref[...])
        yield

    nsub = tm // sub
    gens = [subtile(i * sub, i == nsub - 1) for i in range(nsub)]
    n_stages = 7
    for step in range(n_stages + MIXER_STAGGER * (nsub - 1)):
        for i, g in enumerate(gens):
            if 0 <= step - MIXER_STAGGER * i < n_stages:
                next(g)

    if nt > 1:
        for buf in (vext, vext2):
            buf[0:WINDOW, :] = buf[tm:tm + WINDOW, :]
        for buf in (kt, kt2):
            buf[:, 0:WINDOW] = buf[:, tm:tm + WINDOW]
        mext[0:SUBLANES, :] = mext[tm:tm + SUBLANES, :]
        pext[0:POOL_PAD, :] = pext[tm:tm + POOL_PAD, :]


def _const_spec(shape):
    nd = len(shape)
    return pl.BlockSpec(shape, lambda b, t, _nd=nd: (0,) * _nd, pipeline_mode=pl.Buffered(1))


def _layer_spec(stacked_shape, l):
    nd = len(stacked_shape) - 1
    return pl.BlockSpec((None,) + tuple(stacked_shape[1:]), lambda b, t, _nd=nd, _l=l: (_l,) + (0,) * _nd,
                        pipeline_mode=pl.Buffered(1))


def _window_bias(qn):
    rows = (np.arange(4 * qn) % qn) // CHUNK
    keys = np.arange(WINDOW + qn)
    ok = (keys[None, :] >= rows[:, None] * CHUNK) & (keys[None, :] < (rows[:, None] + 3) * CHUNK)
    first = ok & (keys[None, :] >= WINDOW)
    return np.where(np.stack([ok, first]), 0.0, -np.inf).astype(np.float32)


def _mixer_call(x, tabs, lw, l, *, tm, pos0):
    bsz, t_len, _ = x.shape
    nt = t_len // tm
    assert tm % MIXER_SUB == 0 and MIXER_SUB % ATTN_BLOCK == 0
    kr = min(WINDOW, tm)
    qn = min(ATTN_BLOCK, tm)
    in_specs = [pl.BlockSpec((None, tm, D_MODEL), lambda b, t: (b, t, 0))]
    in_specs += [_const_spec(tabs[0].shape)] * 3
    bias = jnp.asarray(_window_bias(qn))
    in_specs.append(_const_spec(bias.shape))
    args = [x, *tabs, bias]
    wnames = ('g_mix', 'w_in', 'sink', 'conv_w', 'ln_g', 'ln_b', 'gmlp_w', 'gmlp_b', 'pool_w', 'pool_scale', 'w_out')
    for n in wnames:
        a = lw[n]
        if n == 'sink':
            in_specs.append(pl.BlockSpec(memory_space=pltpu.SMEM))
        else:
            in_specs.append(_layer_spec(a.shape, l))
        args.append(a)
    out_shape = [jax.ShapeDtypeStruct((bsz, t_len, D_MODEL), jnp.float32),
                 jax.ShapeDtypeStruct((bsz, kr, KV_WIDTH), jnp.float32),
                 jax.ShapeDtypeStruct((bsz, kr, KV_WIDTH), jnp.float32),
                 jax.ShapeDtypeStruct((bsz, CONV_WIDTH - 1, W_CONV), jnp.float32),
                 jax.ShapeDtypeStruct((bsz, POOL_HIST, W_POOL), jnp.float32)]
    out_specs = [pl.BlockSpec((None, tm, D_MODEL), lambda b, t: (b, t, 0)),
                 pl.BlockSpec((None, kr, KV_WIDTH), lambda b, t: (b, 0, 0)),
                 pl.BlockSpec((None, kr, KV_WIDTH), lambda b, t: (b, 0, 0)),
                 pl.BlockSpec((None, CONV_WIDTH - 1, W_CONV), lambda b, t: (b, 0, 0)),
                 pl.BlockSpec((None, POOL_HIST, W_POOL), lambda b, t: (b, 0, 0))]
    scratch = [pltpu.VMEM((KV_WIDTH, WINDOW + tm), jnp.bfloat16), pltpu.VMEM((WINDOW + tm, KV_WIDTH), jnp.bfloat16)] * 2
    scratch += [pltpu.VMEM((SUBLANES + tm, W_CONV), jnp.float32),
                pltpu.VMEM((POOL_PAD + tm, W_POOL), jnp.float32),
                pltpu.VMEM((POOL_PAD + tm, W_POOL), jnp.float32),
                pltpu.VMEM((POOL_PAD + tm, W_POOL), jnp.float32),
                pltpu.VMEM((tm, D_MODEL), jnp.bfloat16),
                pltpu.VMEM((tm, D_MODEL), jnp.bfloat16)]
    return pl.pallas_call(
        functools.partial(_mixer_kernel, tm=tm, nt=nt, pos0=pos0, layer=l),
        grid=(bsz, nt),
        in_specs=in_specs,
        out_specs=out_specs,
        out_shape=out_shape,
        scratch_shapes=scratch,
        compiler_params=pltpu.CompilerParams(
            dimension_semantics=("arbitrary", "arbitrary"), vmem_limit_bytes=VMEM_LIMIT),
        name="mixer",
    )(*args)


def _mixer_sample_kernel(*refs, nseq, ts, pos0, layer):
    it = iter(refs)
    x_ref = next(it)
    cos_ref, sinlo_ref, sinhi_ref = next(it), next(it), next(it)
    kpast_ref, vpast_ref, cpast_ref, ppast_ref = next(it), next(it), next(it), next(it)
    gmix_ref, win_ref, sink_ref, convw_ref = next(it), next(it), next(it), next(it)
    lng_ref, lnb_ref, gw_ref, gb_ref = next(it), next(it), next(it), next(it)
    pw_ref, pscale_ref, wout_ref = next(it), next(it), next(it)
    xo_ref, kst_ref, vst_ref, cst_ref, pst_ref, grows_ref = (next(it), next(it), next(it), next(it), next(it),
                                                             next(it))
    mext, pext, pa, pb, hb, mix = next(it), next(it), next(it), next(it), next(it), next(it)

    lane = lax.broadcasted_iota(jnp.int32, (1, LANES), 1)
    lo = lane < HEAD_DIM
    lane256 = lax.broadcasted_iota(jnp.int32, (1, W_GMLP), 1) // GMLP_GROUP_DIM
    row4 = lax.broadcasted_iota(jnp.int32, (4 * ts, 1), 0)
    sk = jnp.where(row4 < ts, sink_ref[layer, 0],
                   jnp.where(row4 < 2 * ts, sink_ref[layer, 3],
                             jnp.where(row4 < 3 * ts, sink_ref[layer, 1], sink_ref[layer, 2])))
    wcat = jnp.concatenate([gw_ref[g, 0:ts, 0:ts].astype(jnp.bfloat16) for g in range(GMLP_GROUPS)], axis=1)
    gmask = [(lane256 == g).astype(jnp.bfloat16) for g in range(GMLP_GROUPS)]
    bias_g = gb_ref[0:ts, :]
    scale = HEAD_DIM ** -0.5

    x = x_ref[...]
    hb[...] = _rmsnorm(x, gmix_ref[...]).astype(jnp.bfloat16)
    zq = _dot(hb[...], win_ref[:, _Q0:_CB0])
    zc = _dot(hb[...], win_ref[:, _CB0:_GU0])
    zg = _dot(hb[...], win_ref[:, _GU0:_P0])
    zp = _dot(hb[...], win_ref[:, _P0:PROJ_WIDTH])
    cos_t, sin_lo, sin_hi = cos_ref[...], sinlo_ref[...], sinhi_ref[...]
    qa = _rope(zq[:, 0:128], cos_t, sin_lo, sin_hi) * scale
    qb = _rope(zq[:, 128:256], cos_t, sin_lo, sin_hi) * scale
    k = _rope(zq[:, 256:384], cos_t, sin_lo, sin_hi)
    v = zq[:, 384:512]
    kst_ref[...] = k
    vst_ref[...] = v
    mprod = zc[:, 256:512] * zc[:, 512:768]
    u = jax.nn.gelu(zg[:, 0:256])
    gv = jax.nn.gelu(zg[:, 256:512])
    mu = jnp.mean(gv, axis=-1, keepdims=True)
    var = jnp.mean(jnp.square(gv - mu), axis=-1, keepdims=True)
    vn = (gv - mu) * lax.rsqrt(var + NORM_EPS) * lng_ref[...] + lnb_ref[...]
    grows_ref[...] = vn
    vnb = vn.astype(jnp.bfloat16)
    cw = convw_ref[...]
    winl = jnp.left_shift(2, lane256).astype(jnp.float32)
    prow = lax.broadcasted_iota(jnp.int32, (ts, 1), 0) + (pos0 + 1)
    cnt = jnp.minimum(prow.astype(jnp.float32), winl)
    pooled = []
    for i in range(nseq):
        r = slice(i * ts, (i + 1) * ts)
        kp, vp = kpast_ref[i], vpast_ref[i]
        ktw = jnp.concatenate([kp.T, k[r].T], axis=1).astype(jnp.bfloat16)
        vw = jnp.concatenate([vp, v[r]], axis=0)
        q03 = jnp.concatenate([jnp.where(lo, qa[r], 0.0), jnp.where(lo, 0.0, qb[r])], axis=0)
        q12 = jnp.concatenate([jnp.where(lo, 0.0, qa[r]), jnp.where(lo, qb[r], 0.0)], axis=0)
        s = jnp.concatenate([_dot(q03.astype(jnp.bfloat16), ktw),
                             _dot(q12.astype(jnp.bfloat16), _swap_row_halves(ktw))], axis=0)
        m = jnp.maximum(jnp.max(s, axis=-1, keepdims=True), sk)
        p = jnp.exp(s - m)
        den = jnp.sum(p, axis=-1, keepdims=True) + jnp.exp(sk - m)
        pb16 = p.astype(jnp.bfloat16)
        o03 = _dot(pb16[0:2 * ts], vw.astype(jnp.bfloat16)) / den[0:2 * ts]
        o12 = _dot(pb16[2 * ts:4 * ts], _swap_halves(vw).astype(jnp.bfloat16)) / den[2 * ts:4 * ts]
        mix[r, 0:128] = jnp.where(lo, o03[0:ts], o12[0:ts]).astype(jnp.bfloat16)
        mix[r, 128:256] = jnp.where(lo, o12[ts:2 * ts], o03[ts:2 * ts]).astype(jnp.bfloat16)
        mext[i, 0:SUBLANES, :] = jnp.zeros((SUBLANES, W_CONV), jnp.float32)
        mext[i, SUBLANES - (CONV_WIDTH - 1):SUBLANES, :] = cpast_ref[i]
        mext[i, SUBLANES:SUBLANES + ts, :] = mprod[r]
        yc = (mext[i, SUBLANES - 2:SUBLANES - 2 + ts, :] * cw[0:1, :]
              + mext[i, SUBLANES - 1:SUBLANES - 1 + ts, :] * cw[1:2, :] + mprod[r] * cw[2:3, :])
        mix[r, 256:512] = (zc[r, 0:256] * yc).astype(jnp.bfloat16)
        cst_ref[i] = mext[i, SUBLANES + ts - 2:SUBLANES + ts, :]
        vc = vnb[r]
        sg = _dot(wcat, jnp.concatenate([vc * gmask[g] for g in range(GMLP_GROUPS)], axis=0))
        mix[r, 512:768] = (u[r] * (sg + bias_g)).astype(jnp.bfloat16)
        pext[i, 0:POOL_PAD, :] = jnp.zeros((POOL_PAD, W_POOL), jnp.float32)
        pext[i, POOL_PAD - POOL_HIST:POOL_PAD, :] = ppast_ref[i]
        pext[i, POOL_PAD:POOL_PAD + ts, :] = zp[r]
        pa[i, 0:SUBLANES, :] = jnp.zeros((SUBLANES, W_POOL), jnp.float32)
        pb[i, 0:SUBLANES, :] = jnp.zeros((SUBLANES, W_POOL), jnp.float32)
        n = POOL_PAD - SUBLANES + ts
        b0 = SUBLANES
        pa[i, b0:b0 + n, :] = pext[i, b0:b0 + n, :] + pext[i, b0 - 1:b0 - 1 + n, :]
        pb[i, b0:b0 + n, :] = pa[i, b0:b0 + n, :] + pa[i, b0 - 2:b0 - 2 + n, :]
        s2 = pa[i, POOL_PAD:POOL_PAD + ts, 0:LANES]
        s4 = pb[i, POOL_PAD:POOL_PAD + ts, 0:LANES]
        pa[i, b0:b0 + n, LANES:] = pb[i, b0:b0 + n, LANES:] + pb[i, b0 - 4:b0 - 4 + n, LANES:]
        s8 = pa[i, POOL_PAD:POOL_PAD + ts, LANES:]
        s16 = s8 + pa[i, POOL_PAD - 8:POOL_PAD - 8 + ts, LANES:]
        wsum = jnp.concatenate([jnp.where(lo, s2, s4), jnp.where(lo, s8, s16)], axis=1)
        pooled.append(wsum / cnt - zp[r])
        pst_ref[i] = pext[i, POOL_PAD + ts - POOL_HIST:POOL_PAD + ts, :]
    pooled = jnp.concatenate(pooled, axis=0).astype(jnp.bfloat16)
    mix[:, 768:1024] = (_dot(pooled, pw_ref[...]) * pscale_ref[...]).astype(jnp.bfloat16)
    xo_ref[...] = x + _dot(mix[...], wout_ref[...])


def _mixer_sample_call(x, tabs, past, lw, l, *, pos0):
    nseq, ts, _ = x.shape
    assert ts <= CHUNK and ts % SUBLANES == 0
    tm = nseq * ts
    zero3 = lambda b: (0, 0, 0)
    in_specs = [pl.BlockSpec((None, tm, D_MODEL), zero3)]
    in_specs += [pl.BlockSpec((tm, LANES), lambda b: (0, 0))] * 3
    args = [x.reshape(1, tm, D_MODEL)] + [jnp.tile(a, (nseq, 1)) for a in tabs]
    for a in past:
        in_specs.append(pl.BlockSpec((nseq, None) + a.shape[2:], lambda b, _l=l: (0, _l, 0, 0)))
        args.append(a)
    wnames = ('g_mix', 'w_in', 'sink', 'conv_w', 'ln_g', 'ln_b', 'gmlp_w', 'gmlp_b', 'pool_w', 'pool_scale', 'w_out')
    for n in wnames:
        a = lw[n]
        if n == 'sink':
            in_specs.append(pl.BlockSpec(memory_space=pltpu.SMEM))
        else:
            nd = a.ndim - 1
            in_specs.append(pl.BlockSpec((None,) + a.shape[1:], lambda b, _nd=nd, _l=l: (_l,) + (0,) * _nd))
        args.append(a)
    out_shape = [jax.ShapeDtypeStruct((1, tm, D_MODEL), jnp.float32),
                 jax.ShapeDtypeStruct((tm, KV_WIDTH), jnp.float32),
                 jax.ShapeDtypeStruct((tm, KV_WIDTH), jnp.float32),
                 jax.ShapeDtypeStruct((nseq, CONV_WIDTH - 1, W_CONV), jnp.float32),
                 jax.ShapeDtypeStruct((nseq, POOL_HIST, W_POOL), jnp.float32),
                 jax.ShapeDtypeStruct((tm, W_GMLP), jnp.float32)]
    out_specs = [pl.BlockSpec((None, tm, D_MODEL), zero3),
                 pl.BlockSpec((tm, KV_WIDTH), lambda b: (0, 0)),
                 pl.BlockSpec((tm, KV_WIDTH), lambda b: (0, 0)),
                 pl.BlockSpec((nseq, CONV_WIDTH - 1, W_CONV), zero3),
                 pl.BlockSpec((nseq, POOL_HIST, W_POOL), zero3),
                 pl.BlockSpec((tm, W_GMLP), lambda b: (0, 0))]
    scratch = [pltpu.VMEM((nseq, SUBLANES + ts, W_CONV), jnp.float32),
               pltpu.VMEM((nseq, POOL_PAD + ts, W_POOL), jnp.float32),
               pltpu.VMEM((nseq, POOL_PAD + ts, W_POOL), jnp.float32),
               pltpu.VMEM((nseq, POOL_PAD + ts, W_POOL), jnp.float32),
               pltpu.VMEM((tm, D_MODEL), jnp.bfloat16),
               pltpu.VMEM((tm, D_MODEL), jnp.bfloat16)]
    xo, kst, vst, cst, pst, grows = pl.pallas_call(
        functools.partial(_mixer_sample_kernel, nseq=nseq, ts=ts, pos0=pos0, layer=l),
        grid=(1,),
        in_specs=in_specs,
        out_specs=out_specs,
        out_shape=out_shape,
        scratch_shapes=scratch,
        compiler_params=pltpu.CompilerParams(dimension_semantics=("arbitrary",), vmem_limit_bytes=VMEM_LIMIT),
        name="mixer_past",
    )(*args)
    return (xo.reshape(nseq, ts, D_MODEL), kst.reshape(nseq, ts, KV_WIDTH), vst.reshape(nseq, ts, KV_WIDTH),
            cst, pst, grows.reshape(nseq, ts, W_GMLP))


def _ffn_kernel(*refs, tm, nt, nsteps, chained, final):
    it = iter(refs)
    x_ref = next(it)
    if not chained:
        past_ref = next(it)
    gffn_ref, wup_ref, fcw_ref, wdown_ref = next(it), next(it), next(it), next(it)
    if final:
        gfin_ref = next(it)
    xo_ref, fst_ref = next(it), next(it)
    carry, hbuf, act = next(it), next(it), next(it)
    if chained:
        xbuf, obuf, sem_in, sem_out = next(it), next(it), next(it), next(it)

    t = pl.program_id(1)
    fc = FF_CHUNK
    halves = (0, D_FF)
    seg = tm // SUBLANES

    if chained:
        step = pl.program_id(0) * nt + t
        slot = step % 2

        def tile_copies(tile, sl, outward):
            tb, row0 = tile // nt, (tile % nt) * tm
            cps = []
            for s in range(SUBLANES):
                hbm = (xo_ref if outward else x_ref).at[tb, pl.ds(row0 + s * seg, seg), :]
                if outward:
                    cps.append(pltpu.make_async_copy(obuf.at[sl, :, s, :], hbm, sem_out.at[sl, s]))
                else:
                    cps.append(pltpu.make_async_copy(hbm, xbuf.at[sl, :, s, :], sem_in.at[sl, s]))
            return cps

        @pl.when(step == 0)
        def _first_in():
            for cp in tile_copies(0, 0, False):
                cp.start()

        @pl.when(step + 1 < nsteps)
        def _next_in():
            for cp in tile_copies(step + 1, 1 - slot, False):
                cp.start()

        @pl.when(step >= 2)
        def _free_out():
            for cp in tile_copies(step - 2, slot, True):
                cp.wait()

        for cp in tile_copies(step, slot, False):
            cp.wait()

        @pl.when(t == 0)
        def _init():
            carry[...] = jnp.zeros(carry.shape, jnp.float32)

        xp = xbuf[slot].reshape(tm, D_MODEL)
    else:
        xp = jnp.swapaxes(x_ref[...].reshape(SUBLANES, seg, D_MODEL), 0, 1).reshape(tm, D_MODEL)
    hbuf[...] = _rmsnorm(xp, gffn_ref[...]).astype(jnp.bfloat16)
    sub = lax.broadcasted_iota(jnp.int32, (SUBLANES, fc), 0)
    for c in range(N_FF_CHUNKS):
        ys = []
        for hi, off in enumerate(halves):
            cols = slice(off + c * fc, off + (c + 1) * fc)
            lanes = slice(hi * fc, (hi + 1) * fc)
            up = _dot(hbuf[...], wup_ref[:, cols])
            last1, last2 = up[tm - SUBLANES:tm], up[tm - 2 * SUBLANES:tm - SUBLANES]
            if chained:
                p1 = jnp.where(sub == 0, pltpu.roll(carry[c, 1, :, lanes], 1, axis=0), pltpu.roll(last1, 1, axis=0))
                p2 = jnp.where(sub == 0, pltpu.roll(carry[c, 0, :, lanes], 1, axis=0), pltpu.roll(last2, 1, axis=0))
                fst_ref[0:1, cols] = last2[SUBLANES - 1:SUBLANES]
                fst_ref[1:2, cols] = last1[SUBLANES - 1:SUBLANES]
                if nt > 1:
                    carry[c, 0, :, lanes] = last2
                    carry[c, 1, :, lanes] = last1
            else:
                p2, p1 = past_ref[0, :, cols], past_ref[1, :, cols]
                fst_ref[0, :, cols] = last2
                fst_ref[1, :, cols] = last1
            d1 = jnp.concatenate([p1, up[0:tm - SUBLANES]], axis=0)
            d2 = jnp.concatenate([p2, p1, up[0:tm - 2 * SUBLANES]], axis=0)
            cw = fcw_ref[:, cols]
            ys.append(d2 * cw[0:1, :] + d1 * cw[1:2, :] + up * cw[2:3, :])
        gate, val = ys
        act[:, c * fc:(c + 1) * fc] = (gate * jax.nn.sigmoid(gate) * val).astype(jnp.bfloat16)
    down = _dot(act[...], wdown_ref[...])
    if chained:
        out = xbuf[slot].reshape(tm, D_MODEL) + down
    else:
        out = xp + down
    if final:
        out = _rmsnorm(out, gfin_ref[...])
    if chained:
        obuf[slot] = out.reshape(seg, SUBLANES, D_MODEL)
        for cp in tile_copies(step, slot, True):
            cp.start()

        @pl.when(step == nsteps - 1)
        def _drain():
            if nsteps > 1:
                for cp in tile_copies(step - 1, 1 - slot, True):
                    cp.wait()
            for cp in tile_copies(step, slot, True):
                cp.wait()
    else:
        xo_ref[...] = jnp.swapaxes(out.reshape(seg, SUBLANES, D_MODEL), 0, 1).reshape(tm, D_MODEL)


def _ffn_call(x, fpast, lw, l, g_final, *, tm):
    bsz, t_len, _ = x.shape
    chained = fpast is None
    final = g_final is not None
    args = []
    if chained:
        nt = t_len // tm
        grid = (bsz, nt)
        st_shape = (bsz, CONV_WIDTH - 1, 2 * D_FF)
        st_spec = pl.BlockSpec((None, CONV_WIDTH - 1, 2 * D_FF), lambda b, t: (b, 0, 0))
        in_specs = [pl.BlockSpec(memory_space=pl.ANY)]
        x_out_spec = pl.BlockSpec(memory_space=pl.ANY)
    else:
        assert bsz == SUBLANES and t_len % SUBLANES == 0
        x = x.reshape(1, bsz * t_len, D_MODEL)
        tm, nt, grid = bsz * t_len, 1, (1, 1)
        st_shape = (CONV_WIDTH - 1, bsz, 2 * D_FF)
        st_spec = pl.BlockSpec(st_shape, lambda b, t: (0, 0, 0))
        in_specs = [pl.BlockSpec((None, tm, D_MODEL), lambda b, t: (b, t, 0)),
                    pl.BlockSpec((None,) + st_shape, lambda b, t, _l=l: (_l, 0, 0, 0))]
        x_out_spec = pl.BlockSpec((None, tm, D_MODEL), lambda b, t: (b, t, 0))
        args.append(fpast)
    assert tm % (SUBLANES * SUBLANES) == 0
    seg = tm // SUBLANES
    args = [x] + args
    for n in ('g_ffn', 'w_up', 'ffn_conv_w', 'w_down'):
        in_specs.append(_layer_spec(lw[n].shape, l))
        args.append(lw[n])
    if final:
        in_specs.append(_const_spec(g_final.shape))
        args.append(g_final)
    out_shape = [jax.ShapeDtypeStruct(x.shape, jnp.float32), jax.ShapeDtypeStruct(st_shape, jnp.float32)]
    out_specs = [x_out_spec, st_spec]
    scratch = [pltpu.VMEM((N_FF_CHUNKS, CONV_WIDTH - 1, SUBLANES, 2 * FF_CHUNK), jnp.float32),
               pltpu.VMEM((tm, D_MODEL), jnp.bfloat16),
               pltpu.VMEM((tm, D_FF), jnp.bfloat16)]
    if chained:
        scratch += [pltpu.VMEM((2, seg, SUBLANES, D_MODEL), jnp.float32),
                    pltpu.VMEM((2, seg, SUBLANES, D_MODEL), jnp.float32),
                    pltpu.SemaphoreType.DMA((2, SUBLANES)),
                    pltpu.SemaphoreType.DMA((2, SUBLANES))]
    y, fst = pl.pallas_call(
        functools.partial(_ffn_kernel, tm=tm, nt=nt, nsteps=grid[0] * grid[1], chained=chained, final=final),
        grid=grid,
        in_specs=in_specs,
        out_specs=out_specs,
        out_shape=out_shape,
        scratch_shapes=scratch,
        compiler_params=pltpu.CompilerParams(
            dimension_semantics=("arbitrary", "arbitrary"), vmem_limit_bytes=VMEM_LIMIT),
        name=("ffn" if chained else "ffn_past") + ("_final" if final else ""),
    )(*args)
    if not chained:
        y, fst = y.reshape(bsz, t_len, D_MODEL), jnp.swapaxes(fst, 0, 1)
    return y, fst


def _rope_tables(pos0, t_len):
    half = ROT_DIM // 2
    inv_freq = np.power(np.float64(ROPE_THETA), -np.arange(half, dtype=np.float64) * (2.0 / ROT_DIM))
    ang = (pos0 + np.arange(t_len, dtype=np.float64))[:, None] * inv_freq[None, :]
    cos, sin = np.cos(ang), np.sin(ang)
    rest = HEAD_DIM - ROT_DIM
    cos_h = np.concatenate([cos, cos, np.ones((t_len, rest))], axis=1)
    sinlo_h = np.concatenate([-sin, np.zeros((t_len, half + rest))], axis=1)
    sinhi_h = np.concatenate([np.zeros((t_len, half)), sin, np.zeros((t_len, rest))], axis=1)
    return tuple(jnp.asarray(np.concatenate([a, a], axis=1), dtype=jnp.float32) for a in (cos_h, sinlo_h, sinhi_h))


def _stacked_weights(g_mix, w_in, attn_sink, conv_w, gmlp_ln_g, gmlp_ln_b, gmlp_w, gmlp_b, pool_w, pool_scale,
                     w_out, g_ffn, w_up, ffn_conv_w, w_down):
    eye = jnp.eye(len(POOL_WINDOWS), dtype=pool_w.dtype)
    pw = jnp.einsum('lgij,gh->lgihj', pool_w, eye).reshape(DEPTH, W_POOL, W_POOL)
    return {
        'g_mix': g_mix[:, None, :],
        'w_in': w_in.astype(jnp.bfloat16),
        'sink': attn_sink,
        'conv_w': conv_w,
        'ln_g': gmlp_ln_g[:, None, :],
        'ln_b': gmlp_ln_b[:, None, :],
        'gmlp_w': gmlp_w,
        'gmlp_b': jnp.repeat(jnp.swapaxes(gmlp_b, 1, 2), GMLP_GROUP_DIM, axis=2),
        'pool_w': pw.astype(jnp.bfloat16),
        'pool_scale': pool_scale[:, None, :],
        'w_out': w_out.astype(jnp.bfloat16),
        'g_ffn': g_ffn[:, None, :],
        'w_up': w_up.astype(jnp.bfloat16),
        'ffn_conv_w': ffn_conv_w,
        'w_down': w_down.astype(jnp.bfloat16),
    }


def _run(x, pos0, pasts, lw, g_final, tm):
    bsz, t_len, _ = x.shape
    tabs = _rope_tables(pos0, t_len)
    if pasts is not None:
        ck, cv, sc, sp, sf = pasts
        mpast = (ck.reshape(bsz, DEPTH, WINDOW, KV_WIDTH), cv.reshape(bsz, DEPTH, WINDOW, KV_WIDTH), sc, sp)
        fpast = jnp.transpose(sf, (1, 2, 0, 3))
    else:
        mpast, fpast = None, None
    states = []
    for l in range(DEPTH):
        if pasts is None:
            mo = _mixer_call(x, tabs, lw, l, tm=tm, pos0=pos0)
        else:
            mo = _mixer_sample_call(x, tabs, mpast, lw, l, pos0=pos0)
        x_mid, kst, vst, cst, pst = mo[:5]
        gfin = g_final[None, :] if l == DEPTH - 1 else None
        x, fst = _ffn_call(x_mid, fpast, lw, l, gfin, tm=tm)
        st = [kst.reshape(bsz, -1, N_KV_HEADS, HEAD_DIM), vst.reshape(bsz, -1, N_KV_HEADS, HEAD_DIM),
              cst, pst, fst]
        if pasts is not None:
            st.append(mo[5])
        states.append(st)
    stacked = [jnp.stack([st[i] for st in states], axis=1) for i in range(len(states[0]))]
    return x, stacked


def kernel(x_prompt, x_sample, cache_attn_k, cache_attn_v, state_conv, state_pool, state_ffn_conv, g_mix, w_in,
           attn_sink, conv_w, gmlp_ln_g, gmlp_ln_b, gmlp_w, gmlp_b, pool_w, pool_scale, w_out, g_ffn, w_up,
           ffn_conv_w, w_down, g_final):
    lw = _stacked_weights(g_mix, w_in, attn_sink, conv_w, gmlp_ln_g, gmlp_ln_b, gmlp_w, gmlp_b, pool_w,
                          pool_scale, w_out, g_ffn, w_up, ffn_conv_w, w_down)
    y_prompt, sp = _run(x_prompt, 0, None, lw, g_final, tm=1024)
    y_sample, ss = _run(x_sample, PAST_LEN, (cache_attn_k, cache_attn_v, state_conv, state_pool, state_ffn_conv),
                        lw, g_final, tm=x_sample.shape[1])
    return (y_prompt, y_sample, sp[0], sp[1], sp[2], sp[3], sp[4], ss[0], ss[1], ss[2], ss[3], ss[4], ss[5])
```

```python
import functools

import numpy as np
import jax
import jax.numpy as jnp
from jax import lax
from jax.experimental import pallas as pl
from jax.experimental.pallas import tpu as pltpu

D_MODEL = 1024
DEPTH = 2
CHUNK = 64
HEAD_DIM = 64
N_Q_HEADS = 4
N_KV_HEADS = 2
KV_WIDTH = N_KV_HEADS * HEAD_DIM
W_ATTN = 256
W_CONV = 256
W_GMLP = 256
W_POOL = 256
WINDOW = 128
ROPE_THETA = 500000.0
ROT_DIM = 16
CONV_WIDTH = 3
GMLP_CHUNK = 128
GMLP_GROUPS = 4
GMLP_GROUP_DIM = 64
POOL_WINDOWS = (2, 4, 8, 16)
POOL_HIST = 15
D_FF = 2816
NORM_EPS = 1e-6
PROJ_WIDTH = 2048
PAST_LEN = 2048

LANES = 128
SUBLANES = 8
FF_CHUNK = 256
N_FF_CHUNKS = D_FF // FF_CHUNK
POOL_PAD = 24
ATTN_BLOCK = 2 * CHUNK
MIXER_SUB = 256
MIXER_STAGGER = 0
VMEM_LIMIT = 56 * 1024 * 1024

_Q0, _K0, _V0 = 0, 256, 384
_CB0, _CC0, _CH0 = 512, 768, 1024
_GU0, _GV0 = 1280, 1536
_P0 = 1792


def _rmsnorm(x, g):
    ms = jnp.mean(x * x, axis=-1, keepdims=True)
    return x * lax.rsqrt(ms + NORM_EPS) * g


def _dot(a, b):
    return jnp.dot(a, b, preferred_element_type=jnp.float32)


def _rope(x, cos_t, sin_lo, sin_hi):
    up = pltpu.roll(x, LANES - ROT_DIM // 2, axis=1)
    dn = pltpu.roll(x, ROT_DIM // 2, axis=1)
    return x * cos_t + up * sin_lo + dn * sin_hi


def _swap_halves(x):
    return pltpu.roll(x, HEAD_DIM, axis=1)


def _swap_row_halves(x):
    return jnp.concatenate([x[HEAD_DIM:], x[:HEAD_DIM]], axis=0)


def _mixer_kernel(*refs, tm, nt, pos0, layer):
    it = iter(refs)
    x_ref = next(it)
    cos_ref, sinlo_ref, sinhi_ref = next(it), next(it), next(it)
    bias_ref = next(it)
    gmix_ref, win_ref, sink_ref, convw_ref = next(it), next(it), next(it), next(it)
    lng_ref, lnb_ref, gw_ref, gb_ref = next(it), next(it), next(it), next(it)
    pw_ref, pscale_ref, wout_ref = next(it), next(it), next(it)
    xo_ref, kst_ref, vst_ref, cst_ref, pst_ref = next(it), next(it), next(it), next(it), next(it)
    kt, vext, kt2, vext2 = next(it), next(it), next(it), next(it)
    mext, pext, pa, pb, hb, mix = next(it), next(it), next(it), next(it), next(it), next(it)

    t = pl.program_id(1)
    qn = min(ATTN_BLOCK, tm)
    nkeys = WINDOW + qn
    kr = min(WINDOW, tm)
    gl = min(GMLP_CHUNK, tm)
    sub = min(MIXER_SUB, tm)

    @pl.when(t == 0)
    def _init():
        mext[0:SUBLANES, :] = jnp.zeros((SUBLANES, W_CONV), jnp.float32)
        pext[0:POOL_PAD, :] = jnp.zeros((POOL_PAD, W_POOL), jnp.float32)
        pa[0:SUBLANES, :] = jnp.zeros((SUBLANES, W_POOL), jnp.float32)
        pb[0:SUBLANES, :] = jnp.zeros((SUBLANES, W_POOL), jnp.float32)
        for buf in (vext, vext2):
            buf[0:WINDOW, :] = jnp.zeros((WINDOW, KV_WIDTH), jnp.bfloat16)
        for buf in (kt, kt2):
            buf[:, 0:WINDOW] = jnp.zeros((KV_WIDTH, WINDOW), jnp.bfloat16)

    lane = lax.broadcasted_iota(jnp.int32, (1, LANES), 1)
    lo = lane < HEAD_DIM
    lane256 = lax.broadcasted_iota(jnp.int32, (1, W_GMLP), 1) // GMLP_GROUP_DIM
    row4 = lax.broadcasted_iota(jnp.int32, (4 * qn, 1), 0)
    sk = jnp.where(row4 < qn, sink_ref[layer, 0],
                   jnp.where(row4 < 2 * qn, sink_ref[layer, 3],
                             jnp.where(row4 < 3 * qn, sink_ref[layer, 1], sink_ref[layer, 2])))
    gi = lax.broadcasted_iota(jnp.int32, (gl, gl), 0)
    gj = lax.broadcasted_iota(jnp.int32, (gl, gl), 1)
    tri = (gj // CHUNK) <= (gi // CHUNK)
    wcat = jnp.concatenate([jnp.where(tri, gw_ref[g, 0:gl, 0:gl], 0.0).astype(jnp.bfloat16)
                            for g in range(GMLP_GROUPS)], axis=1)
    gmask = [(lane256 == g).astype(jnp.bfloat16) for g in range(GMLP_GROUPS)]
    bias_g = gb_ref[0:gl, :]
    scale = HEAD_DIM ** -0.5

    def subtile(r0, last):
        rows = slice(r0, r0 + sub)
        krows = slice(WINDOW + r0, WINDOW + r0 + sub)
        x = x_ref[rows, :]
        hb[rows, :] = _rmsnorm(x, gmix_ref[...]).astype(jnp.bfloat16)
        zq = _dot(hb[rows, :], win_ref[:, _Q0:_CB0])
        trows = pl.ds(pl.multiple_of(t * tm + r0, SUBLANES), sub)
        cos_t, sin_lo, sin_hi = cos_ref[trows, :], sinlo_ref[trows, :], sinhi_ref[trows, :]
        qa = _rope(zq[:, 0:128], cos_t, sin_lo, sin_hi) * scale
        qb = _rope(zq[:, 128:256], cos_t, sin_lo, sin_hi) * scale
        k = _rope(zq[:, 256:384], cos_t, sin_lo, sin_hi)
        v = zq[:, 384:512]
        ktb = k.T.astype(jnp.bfloat16)
        kt[:, krows] = ktb
        kt2[:, krows] = _swap_row_halves(ktb)
        vext[krows, :] = v.astype(jnp.bfloat16)
        vext2[krows, :] = _swap_halves(v).astype(jnp.bfloat16)
        if last:
            kst_ref[...] = k[sub - kr:sub, :]
            vst_ref[...] = v[sub - kr:sub, :]
        yield
        scores = []
        for b in range(sub // qn):
            q0 = b * qn
            wrows = slice(r0 + q0, r0 + q0 + nkeys)
            qa_b, qb_b = qa[q0:q0 + qn, :], qb[q0:q0 + qn, :]
            q03 = jnp.concatenate([jnp.where(lo, qa_b, 0.0), jnp.where(lo, 0.0, qb_b)], axis=0)
            q12 = jnp.concatenate([jnp.where(lo, 0.0, qa_b), jnp.where(lo, qb_b, 0.0)], axis=0)
            s = jnp.concatenate([_dot(q03.astype(jnp.bfloat16), kt[:, wrows]),
                                 _dot(q12.astype(jnp.bfloat16), kt2[:, wrows])], axis=0)
            scores.append(s + (bias_ref[0] if r0 + q0 > 0 else bias_ref[jnp.where(t == 0, 1, 0)]))
        yield
        zc = _dot(hb[rows, :], win_ref[:, _CB0:_GU0])
        zg = _dot(hb[rows, :], win_ref[:, _GU0:_P0])
        zp = _dot(hb[rows, :], win_ref[:, _P0:PROJ_WIDTH])
        yield
        probs, dens = [], []
        for s in scores:
            m = jnp.maximum(jnp.max(s, axis=-1, keepdims=True), sk)
            p = jnp.exp(s - m)
            dens.append(jnp.sum(p, axis=-1, keepdims=True) + jnp.exp(sk - m))
            probs.append(p.astype(jnp.bfloat16))
        yield
        for b in range(sub // qn):
            q0 = r0 + b * qn
            wrows = slice(q0, q0 + nkeys)
            o03 = _dot(probs[b][0:2 * qn], vext[wrows, :]) / dens[b][0:2 * qn]
            o12 = _dot(probs[b][2 * qn:4 * qn], vext2[wrows, :]) / dens[b][2 * qn:4 * qn]
            oa = jnp.where(lo, o03[0:qn], o12[0:qn])
            ob = jnp.where(lo, o12[qn:2 * qn], o03[qn:2 * qn])
            mix[q0:q0 + qn, 0:128] = oa.astype(jnp.bfloat16)
            mix[q0:q0 + qn, 128:256] = ob.astype(jnp.bfloat16)
        yield
        mprod = zc[:, 256:512] * zc[:, 512:768]
        mext[SUBLANES + r0:SUBLANES + r0 + sub, :] = mprod
        cw = convw_ref[...]
        yc = (mext[SUBLANES - 2 + r0:SUBLANES - 2 + r0 + sub, :] * cw[0:1, :]
              + mext[SUBLANES - 1 + r0:SUBLANES - 1 + r0 + sub, :] * cw[1:2, :]
              + mprod * cw[2:3, :])
        mix[rows, 256:512] = (zc[:, 0:256] * yc).astype(jnp.bfloat16)
        if last:
            cst_ref[...] = mext[SUBLANES + tm - 2:SUBLANES + tm, :]

        u = jax.nn.gelu(zg[:, 0:256])
        gv = jax.nn.gelu(zg[:, 256:512])
        mu = jnp.mean(gv, axis=-1, keepdims=True)
        var = jnp.mean(jnp.square(gv - mu), axis=-1, keepdims=True)
        vn = (gv - mu) * lax.rsqrt(var + NORM_EPS) * lng_ref[...] + lnb_ref[...]
        vnb = vn.astype(jnp.bfloat16)
        for c in range(sub // gl):
            c0 = c * gl
            vc = vnb[c0:c0 + gl, :]
            s = _dot(wcat, jnp.concatenate([vc * gmask[g] for g in range(GMLP_GROUPS)], axis=0))
            mix[r0 + c0:r0 + c0 + gl, 512:768] = (u[c0:c0 + gl, :] * (s + bias_g)).astype(jnp.bfloat16)

        pext[POOL_PAD + r0:POOL_PAD + r0 + sub, :] = zp
        n = POOL_PAD - SUBLANES + sub
        lo_r = SUBLANES + r0
        pa[lo_r:lo_r + n, :] = pext[lo_r:lo_r + n, :] + pext[lo_r - 1:lo_r - 1 + n, :]
        pb[lo_r:lo_r + n, :] = pa[lo_r:lo_r + n, :] + pa[lo_r - 2:lo_r - 2 + n, :]
        tr = slice(POOL_PAD + r0, POOL_PAD + r0 + sub)
        s2 = pa[tr, 0:LANES]
        s4 = pb[tr, 0:LANES]
        pa[lo_r:lo_r + n, LANES:] = pb[lo_r:lo_r + n, LANES:] + pb[lo_r - 4:lo_r - 4 + n, LANES:]
        s8 = pa[tr, LANES:]
        s16 = s8 + pa[POOL_PAD - 8 + r0:POOL_PAD - 8 + r0 + sub, LANES:]
        wsum = jnp.concatenate([jnp.where(lo, s2, s4), jnp.where(lo, s8, s16)], axis=1)
        prow = lax.broadcasted_iota(jnp.int32, (sub, 1), 0) + (pos0 + 1 + r0) + t * tm
        winl = jnp.left_shift(2, lane256).astype(jnp.float32)
        cnt = jnp.minimum(prow.astype(jnp.float32), winl)
        pooled = wsum / cnt - zp
        mix[rows, 768:1024] = (_dot(pooled.astype(jnp.bfloat16), pw_ref[...]) * pscale_ref[...]).astype(jnp.bfloat16)
        if last:
            pst_ref[...] = pext[POOL_PAD + tm - POOL_HIST:POOL_PAD + tm, :]
        yield
        xo_ref[rows, :] = x_ref[rows, :] + _dot(mix[rows, :], wout_ref[...])
        yield

    nsub = tm // sub
    gens = [subtile(i * sub, i == nsub - 1) for i in range(nsub)]
    n_stages = 7
    for step in range(n_stages + MIXER_STAGGER * (nsub - 1)):
        for i, g in enumerate(gens):
            if 0 <= step - MIXER_STAGGER * i < n_stages:
                next(g)

    if nt > 1:
        for buf in (vext, vext2):
            buf[0:WINDOW, :] = buf[tm:tm + WINDOW, :]
        for buf in (kt, kt2):
            buf[:, 0:WINDOW] = buf[:, tm:tm + WINDOW]
        mext[0:SUBLANES, :] = mext[tm:tm + SUBLANES, :]
        pext[0:POOL_PAD, :] = pext[tm:tm + POOL_PAD, :]


def _const_spec(shape):
    nd = len(shape)
    return pl.BlockSpec(shape, lambda b, t, _nd=nd: (0,) * _nd, pipeline_mode=pl.Buffered(1))


def _layer_spec(stacked_shape, l):
    nd = len(stacked_shape) - 1
    return pl.BlockSpec((None,) + tuple(stacked_shape[1:]), lambda b, t, _nd=nd, _l=l: (_l,) + (0,) * _nd,
                        pipeline_mode=pl.Buffered(1))


def _window_bias(qn):
    rows = (np.arange(4 * qn) % qn) // CHUNK
    keys = np.arange(WINDOW + qn)
    ok = (keys[None, :] >= rows[:, None] * CHUNK) & (keys[None, :] < (rows[:, None] + 3) * CHUNK)
    first = ok & (keys[None, :] >= WINDOW)
    return np.where(np.stack([ok, first]), 0.0, -np.inf).astype(np.float32)


def _mixer_call(x, tabs, lw, l, *, tm, pos0):
    bsz, t_len, _ = x.shape
    nt = t_len // tm
    assert tm % MIXER_SUB == 0 and MIXER_SUB % ATTN_BLOCK == 0
    kr = min(WINDOW, tm)
    qn = min(ATTN_BLOCK, tm)
    in_specs = [pl.BlockSpec((None, tm, D_MODEL), lambda b, t: (b, t, 0))]
    in_specs += [_const_spec(tabs[0].shape)] * 3
    bias = jnp.asarray(_window_bias(qn))
    in_specs.append(_const_spec(bias.shape))
    args = [x, *tabs, bias]
    wnames = ('g_mix', 'w_in', 'sink', 'conv_w', 'ln_g', 'ln_b', 'gmlp_w', 'gmlp_b', 'pool_w', 'pool_scale', 'w_out')
    for n in wnames:
        a = lw[n]
        if n == 'sink':
            in_specs.append(pl.BlockSpec(memory_space=pltpu.SMEM))
        else:
            in_specs.append(_layer_spec(a.shape, l))
        args.append(a)
    out_shape = [jax.ShapeDtypeStruct((bsz, t_len, D_MODEL), jnp.float32),
                 jax.ShapeDtypeStruct((bsz, kr, KV_WIDTH), jnp.float32),
                 jax.ShapeDtypeStruct((bsz, kr, KV_WIDTH), jnp.float32),
                 jax.ShapeDtypeStruct((bsz, CONV_WIDTH - 1, W_CONV), jnp.float32),
                 jax.ShapeDtypeStruct((bsz, POOL_HIST, W_POOL), jnp.float32)]
    out_specs = [pl.BlockSpec((None, tm, D_MODEL), lambda b, t: (b, t, 0)),
                 pl.BlockSpec((None, kr, KV_WIDTH), lambda b, t: (b, 0, 0)),
                 pl.BlockSpec((None, kr, KV_WIDTH), lambda b, t: (b, 0, 0)),
                 pl.BlockSpec((None, CONV_WIDTH - 1, W_CONV), lambda b, t: (b, 0, 0)),
                 pl.BlockSpec((None, POOL_HIST, W_POOL), lambda b, t: (b, 0, 0))]
    scratch = [pltpu.VMEM((KV_WIDTH, WINDOW + tm), jnp.bfloat16), pltpu.VMEM((WINDOW + tm, KV_WIDTH), jnp.bfloat16)] * 2
    scratch += [pltpu.VMEM((SUBLANES + tm, W_CONV), jnp.float32),
                pltpu.VMEM((POOL_PAD + tm, W_POOL), jnp.float32),
                pltpu.VMEM((POOL_PAD + tm, W_POOL), jnp.float32),
                pltpu.VMEM((POOL_PAD + tm, W_POOL), jnp.float32),
                pltpu.VMEM((tm, D_MODEL), jnp.bfloat16),
                pltpu.VMEM((tm, D_MODEL), jnp.bfloat16)]
    return pl.pallas_call(
        functools.partial(_mixer_kernel, tm=tm, nt=nt, pos0=pos0, layer=l),
        grid=(bsz, nt),
        in_specs=in_specs,
        out_specs=out_specs,
        out_shape=out_shape,
        scratch_shapes=scratch,
        compiler_params=pltpu.CompilerParams(
            dimension_semantics=("arbitrary", "arbitrary"), vmem_limit_bytes=VMEM_LIMIT),
        name="mixer",
    )(*args)


def _mixer_sample_kernel(*refs, nseq, ts, pos0, layer):
    it = iter(refs)
    x_ref = next(it)
    cos_ref, sinlo_ref, sinhi_ref = next(it), next(it), next(it)
    kpast_ref, vpast_ref, cpast_ref, ppast_ref = next(it), next(it), next(it), next(it)
    gmix_ref, win_ref, sink_ref, convw_ref = next(it), next(it), next(it), next(it)
    lng_ref, lnb_ref, gw_ref, gb_ref = next(it), next(it), next(it), next(it)
    pw_ref, pscale_ref, wout_ref = next(it), next(it), next(it)
    xo_ref, kst_ref, vst_ref, cst_ref, pst_ref, grows_ref = (next(it), next(it), next(it), next(it), next(it),
                                                             next(it))
    mext, pext, pa, pb, hb, mix = next(it), next(it), next(it), next(it), next(it), next(it)

    lane = lax.broadcasted_iota(jnp.int32, (1, LANES), 1)
    lo = lane < HEAD_DIM
    lane256 = lax.broadcasted_iota(jnp.int32, (1, W_GMLP), 1) // GMLP_GROUP_DIM
    row4 = lax.broadcasted_iota(jnp.int32, (4 * ts, 1), 0)
    sk = jnp.where(row4 < ts, sink_ref[layer, 0],
                   jnp.where(row4 < 2 * ts, sink_ref[layer, 3],
                             jnp.where(row4 < 3 * ts, sink_ref[layer, 1], sink_ref[layer, 2])))
    wcat = jnp.concatenate([gw_ref[g, 0:ts, 0:ts].astype(jnp.bfloat16) for g in range(GMLP_GROUPS)], axis=1)
    gmask = [(lane256 == g).astype(jnp.bfloat16) for g in range(GMLP_GROUPS)]
    bias_g = gb_ref[0:ts, :]
    scale = HEAD_DIM ** -0.5

    x = x_ref[...]
    hb[...] = _rmsnorm(x, gmix_ref[...]).astype(jnp.bfloat16)
    zq = _dot(hb[...], win_ref[:, _Q0:_CB0])
    zc = _dot(hb[...], win_ref[:, _CB0:_GU0])
    zg = _dot(hb[...], win_ref[:, _GU0:_P0])
    zp = _dot(hb[...], win_ref[:, _P0:PROJ_WIDTH])
    cos_t, sin_lo, sin_hi = cos_ref[...], sinlo_ref[...], sinhi_ref[...]
    qa = _rope(zq[:, 0:128], cos_t, sin_lo, sin_hi) * scale
    qb = _rope(zq[:, 128:256], cos_t, sin_lo, sin_hi) * scale
    k = _rope(zq[:, 256:384], cos_t, sin_lo, sin_hi)
    v = zq[:, 384:512]
    kst_ref[...] = k
    vst_ref[...] = v
    mprod = zc[:, 256:512] * zc[:, 512:768]
    u = jax.nn.gelu(zg[:, 0:256])
    gv = jax.nn.gelu(zg[:, 256:512])
    mu = jnp.mean(gv, axis=-1, keepdims=True)
    var = jnp.mean(jnp.square(gv - mu), axis=-1, keepdims=True)
    vn = (gv - mu) * lax.rsqrt(var + NORM_EPS) * lng_ref[...] + lnb_ref[...]
    grows_ref[...] = vn
    vnb = vn.astype(jnp.bfloat16)
    cw = convw_ref[...]
    winl = jnp.left_shift(2, lane256).astype(jnp.float32)
    prow = lax.broadcasted_iota(jnp.int32, (ts, 1), 0) + (pos0 + 1)
    cnt = jnp.minimum(prow.astype(jnp.float32), winl)
    pooled = []
    for i in range(nseq):
        r = slice(i * ts, (i + 1) * ts)
        kp, vp = kpast_ref[i], vpast_ref[i]
        ktw = jnp.concatenate([kp.T, k[r].T], axis=1).astype(jnp.bfloat16)
        vw = jnp.concatenate([vp, v[r]], axis=0)
        q03 = jnp.concatenate([jnp.where(lo, qa[r], 0.0), jnp.where(lo, 0.0, qb[r])], axis=0)
        q12 = jnp.concatenate([jnp.where(lo, 0.0, qa[r]), jnp.where(lo, qb[r], 0.0)], axis=0)
        s = jnp.concatenate([_dot(q03.astype(jnp.bfloat16), ktw),
                             _dot(q12.astype(jnp.bfloat16), _swap_row_halves(ktw))], axis=0)
        m = jnp.maximum(jnp.max(s, axis=-1, keepdims=True), sk)
        p = jnp.exp(s - m)
        den = jnp.sum(p, axis=-1, keepdims=True) + jnp.exp(sk - m)
        pb16 = p.astype(jnp.bfloat16)
        o03 = _dot(pb16[0:2 * ts], vw.astype(jnp.bfloat16)) / den[0:2 * ts]
        o12 = _dot(pb16[2 * ts:4 * ts], _swap_halves(vw).astype(jnp.bfloat16)) / den[2 * ts:4 * ts]
        mix[r, 0:128] = jnp.where(lo, o03[0:ts], o12[0:ts]).astype(jnp.bfloat16)
        mix[r, 128:256] = jnp.where(lo, o12[ts:2 * ts], o03[ts:2 * ts]).astype(jnp.bfloat16)
        mext[i, 0:SUBLANES, :] = jnp.zeros((SUBLANES, W_CONV), jnp.float32)
        mext[i, SUBLANES - (CONV_WIDTH - 1):SUBLANES, :] = cpast_ref[i]
        mext[i, SUBLANES:SUBLANES + ts, :] = mprod[r]
        yc = (mext[i, SUBLANES - 2:SUBLANES - 2 + ts, :] * cw[0:1, :]
              + mext[i, SUBLANES - 1:SUBLANES - 1 + ts, :] * cw[1:2, :] + mprod[r] * cw[2:3, :])
        mix[r, 256:512] = (zc[r, 0:256] * yc).astype(jnp.bfloat16)
        cst_ref[i] = mext[i, SUBLANES + ts - 2:SUBLANES + ts, :]
        vc = vnb[r]
        sg = _dot(wcat, jnp.concatenate([vc * gmask[g] for g in range(GMLP_GROUPS)], axis=0))
        mix[r, 512:768] = (u[r] * (sg + bias_g)).astype(jnp.bfloat16)
        pext[i, 0:POOL_PAD, :] = jnp.zeros((POOL_PAD, W_POOL), jnp.float32)
        pext[i, POOL_PAD - POOL_HIST:POOL_PAD, :] = ppast_ref[i]
        pext[i, POOL_PAD:POOL_PAD + ts, :] = zp[r]
        pa[i, 0:SUBLANES, :] = jnp.zeros((SUBLANES, W_POOL), jnp.float32)
        pb[i, 0:SUBLANES, :] = jnp.zeros((SUBLANES, W_POOL), jnp.float32)
        n = POOL_PAD - SUBLANES + ts
        b0 = SUBLANES
        pa[i, b0:b0 + n, :] = pext[i, b0:b0 + n, :] + pext[i, b0 - 1:b0 - 1 + n, :]
        pb[i, b0:b0 + n, :] = pa[i, b0:b0 + n, :] + pa[i, b0 - 2:b0 - 2 + n, :]
        s2 = pa[i, POOL_PAD:POOL_PAD + ts, 0:LANES]
        s4 = pb[i, POOL_PAD:POOL_PAD + ts, 0:LANES]
        pa[i, b0:b0 + n, LANES:] = pb[i, b0:b0 + n, LANES:] + pb[i, b0 - 4:b0 - 4 + n, LANES:]
        s8 = pa[i, POOL_PAD:POOL_PAD + ts, LANES:]
        s16 = s8 + pa[i, POOL_PAD - 8:POOL_PAD - 8 + ts, LANES:]
        wsum = jnp.concatenate([jnp.where(lo, s2, s4), jnp.where(lo, s8, s16)], axis=1)
        pooled.append(wsum / cnt - zp[r])
        pst_ref[i] = pext[i, POOL_PAD + ts - POOL_HIST:POOL_PAD + ts, :]
    pooled = jnp.concatenate(pooled, axis=0).astype(jnp.bfloat16)
    mix[:, 768:1024] = (_dot(pooled, pw_ref[...]) * pscale_ref[...]).astype(jnp.bfloat16)
    xo_ref[...] = x + _dot(mix[...], wout_ref[...])


def _mixer_sample_call(x, tabs, past, lw, l, *, pos0):
    nseq, ts, _ = x.shape
    assert ts <= CHUNK and ts % SUBLANES == 0
    tm = nseq * ts
    zero3 = lambda b: (0, 0, 0)
    in_specs = [pl.BlockSpec((None, tm, D_MODEL), zero3)]
    in_specs += [pl.BlockSpec((tm, LANES), lambda b: (0, 0))] * 3
    args = [x.reshape(1, tm, D_MODEL)] + [jnp.tile(a, (nseq, 1)) for a in tabs]
    for a in past:
        in_specs.append(pl.BlockSpec((nseq, None) + a.shape[2:], lambda b, _l=l: (0, _l, 0, 0)))
        args.append(a)
    wnames = ('g_mix', 'w_in', 'sink', 'conv_w', 'ln_g', 'ln_b', 'gmlp_w', 'gmlp_b', 'pool_w', 'pool_scale', 'w_out')
    for n in wnames:
        a = lw[n]
        if n == 'sink':
            in_specs.append(pl.BlockSpec(memory_space=pltpu.SMEM))
        else:
            nd = a.ndim - 1
            in_specs.append(pl.BlockSpec((None,) + a.shape[1:], lambda b, _nd=nd, _l=l: (_l,) + (0,) * _nd))
        args.append(a)
    out_shape = [jax.ShapeDtypeStruct((1, tm, D_MODEL), jnp.float32),
                 jax.ShapeDtypeStruct((tm, KV_WIDTH), jnp.float32),
                 jax.ShapeDtypeStruct((tm, KV_WIDTH), jnp.float32),
                 jax.ShapeDtypeStruct((nseq, CONV_WIDTH - 1, W_CONV), jnp.float32),
                 jax.ShapeDtypeStruct((nseq, POOL_HIST, W_POOL), jnp.float32),
                 jax.ShapeDtypeStruct((tm, W_GMLP), jnp.float32)]
    out_specs = [pl.BlockSpec((None, tm, D_MODEL), zero3),
                 pl.BlockSpec((tm, KV_WIDTH), lambda b: (0, 0)),
                 pl.BlockSpec((tm, KV_WIDTH), lambda b: (0, 0)),
                 pl.BlockSpec((nseq, CONV_WIDTH - 1, W_CONV), zero3),
                 pl.BlockSpec((nseq, POOL_HIST, W_POOL), zero3),
                 pl.BlockSpec((tm, W_GMLP), lambda b: (0, 0))]
    scratch = [pltpu.VMEM((nseq, SUBLANES + ts, W_CONV), jnp.float32),
               pltpu.VMEM((nseq, POOL_PAD + ts, W_POOL), jnp.float32),
               pltpu.VMEM((nseq, POOL_PAD + ts, W_POOL), jnp.float32),
               pltpu.VMEM((nseq, POOL_PAD + ts, W_POOL), jnp.float32),
               pltpu.VMEM((tm, D_MODEL), jnp.bfloat16),
               pltpu.VMEM((tm, D_MODEL), jnp.bfloat16)]
    xo, kst, vst, cst, pst, grows = pl.pallas_call(
        functools.partial(_mixer_sample_kernel, nseq=nseq, ts=ts, pos0=pos0, layer=l),
        grid=(1,),
        in_specs=in_specs,
        out_specs=out_specs,
        out_shape=out_shape,
        scratch_shapes=scratch,
        compiler_params=pltpu.CompilerParams(dimension_semantics=("arbitrary",), vmem_limit_bytes=VMEM_LIMIT),
        name="mixer_past",
    )(*args)
    return (xo.reshape(nseq, ts, D_MODEL), kst.reshape(nseq, ts, KV_WIDTH), vst.reshape(nseq, ts, KV_WIDTH),
            cst, pst, grows.reshape(nseq, ts, W_GMLP))


def _ffn_kernel(*refs, tm, nt, nsteps, chained, final):
    it = iter(refs)
    x_ref = next(it)
    if not chained:
        past_ref = next(it)
    gffn_ref, wup_ref, fcw_ref, wdown_ref = next(it), next(it), next(it), next(it)
    if final:
        gfin_ref = next(it)
    xo_ref, fst_ref = next(it), next(it)
    carry, hbuf, act = next(it), next(it), next(it)
    if chained:
        xbuf, obuf, sem_in, sem_out = next(it), next(it), next(it), next(it)

    t = pl.program_id(1)
    fc = FF_CHUNK
    halves = (0, D_FF)
    seg = tm // SUBLANES

    if chained:
        step = pl.program_id(0) * nt + t
        slot = step % 2

        def tile_copies(tile, sl, outward):
            tb, row0 = tile // nt, (tile % nt) * tm
            cps = []
            for s in range(SUBLANES):
                hbm = (xo_ref if outward else x_ref).at[tb, pl.ds(row0 + s * seg, seg), :]
                if outward:
                    cps.append(pltpu.make_async_copy(obuf.at[sl, :, s, :], hbm, sem_out.at[sl, s]))
                else:
                    cps.append(pltpu.make_async_copy(hbm, xbuf.at[sl, :, s, :], sem_in.at[sl, s]))
            return cps

        @pl.when(step == 0)
        def _first_in():
            for cp in tile_copies(0, 0, False):
                cp.start()

        @pl.when(step + 1 < nsteps)
        def _next_in():
            for cp in tile_copies(step + 1, 1 - slot, False):
                cp.start()

        @pl.when(step >= 2)
        def _free_out():
            for cp in tile_copies(step - 2, slot, True):
                cp.wait()

        for cp in tile_copies(step, slot, False):
            cp.wait()

        @pl.when(t == 0)
        def _init():
            carry[...] = jnp.zeros(carry.shape, jnp.float32)

        xp = xbuf[slot].reshape(tm, D_MODEL)
    else:
        xp = jnp.swapaxes(x_ref[...].reshape(SUBLANES, seg, D_MODEL), 0, 1).reshape(tm, D_MODEL)
    hbuf[...] = _rmsnorm(xp, gffn_ref[...]).astype(jnp.bfloat16)
    sub = lax.broadcasted_iota(jnp.int32, (SUBLANES, fc), 0)
    for c in range(N_FF_CHUNKS):
        ys = []
        for hi, off in enumerate(halves):
            cols = slice(off + c * fc, off + (c + 1) * fc)
            lanes = slice(hi * fc, (hi + 1) * fc)
            up = _dot(hbuf[...], wup_ref[:, cols])
            last1, last2 = up[tm - SUBLANES:tm], up[tm - 2 * SUBLANES:tm - SUBLANES]
            if chained:
                p1 = jnp.where(sub == 0, pltpu.roll(carry[c, 1, :, lanes], 1, axis=0), pltpu.roll(last1, 1, axis=0))
                p2 = jnp.where(sub == 0, pltpu.roll(carry[c, 0, :, lanes], 1, axis=0), pltpu.roll(last2, 1, axis=0))
                fst_ref[0:1, cols] = last2[SUBLANES - 1:SUBLANES]
                fst_ref[1:2, cols] = last1[SUBLANES - 1:SUBLANES]
                if nt > 1:
                    carry[c, 0, :, lanes] = last2
                    carry[c, 1, :, lanes] = last1
            else:
                p2, p1 = past_ref[0, :, cols], past_ref[1, :, cols]
                fst_ref[0, :, cols] = last2
                fst_ref[1, :, cols] = last1
            d1 = jnp.concatenate([p1, up[0:tm - SUBLANES]], axis=0)
            d2 = jnp.concatenate([p2, p1, up[0:tm - 2 * SUBLANES]], axis=0)
            cw = fcw_ref[:, cols]
            ys.append(d2 * cw[0:1, :] + d1 * cw[1:2, :] + up * cw[2:3, :])
        gate, val = ys
        act[:, c * fc:(c + 1) * fc] = (gate * jax.nn.sigmoid(gate) * val).astype(jnp.bfloat16)
    down = _dot(act[...], wdown_ref[...])
    if chained:
        out = xbuf[slot].reshape(tm, D_MODEL) + down
    else:
        out = xp + down
    if final:
        out = _rmsnorm(out, gfin_ref[...])
    if chained:
        obuf[slot] = out.reshape(seg, SUBLANES, D_MODEL)
        for cp in tile_copies(step, slot, True):
            cp.start(priority=1)

        @pl.when(step == nsteps - 1)
        def _drain():
            if nsteps > 1:
                for cp in tile_copies(step - 1, 1 - slot, True):
                    cp.wait()
            for cp in tile_copies(step, slot, True):
                cp.wait()
    else:
        xo_ref[...] = jnp.swapaxes(out.reshape(seg, SUBLANES, D_MODEL), 0, 1).reshape(tm, D_MODEL)


def _ffn_call(x, fpast, lw, l, g_final, *, tm):
    bsz, t_len, _ = x.shape
    chained = fpast is None
    final = g_final is not None
    args = []
    if chained:
        nt = t_len // tm
        grid = (bsz, nt)
        st_shape = (bsz, CONV_WIDTH - 1, 2 * D_FF)
        st_spec = pl.BlockSpec((None, CONV_WIDTH - 1, 2 * D_FF), lambda b, t: (b, 0, 0))
        in_specs = [pl.BlockSpec(memory_space=pl.ANY)]
        x_out_spec = pl.BlockSpec(memory_space=pl.ANY)
    else:
        assert bsz == SUBLANES and t_len % SUBLANES == 0
        x = x.reshape(1, bsz * t_len, D_MODEL)
        tm, nt, grid = bsz * t_len, 1, (1, 1)
        st_shape = (CONV_WIDTH - 1, bsz, 2 * D_FF)
        st_spec = pl.BlockSpec(st_shape, lambda b, t: (0, 0, 0))
        in_specs = [pl.BlockSpec((None, tm, D_MODEL), lambda b, t: (b, t, 0)),
                    pl.BlockSpec((None,) + st_shape, lambda b, t, _l=l: (_l, 0, 0, 0))]
        x_out_spec = pl.BlockSpec((None, tm, D_MODEL), lambda b, t: (b, t, 0))
        args.append(fpast)
    assert tm % (SUBLANES * SUBLANES) == 0
    seg = tm // SUBLANES
    args = [x] + args
    for n in ('g_ffn', 'w_up', 'ffn_conv_w', 'w_down'):
        in_specs.append(_layer_spec(lw[n].shape, l))
        args.append(lw[n])
    if final:
        in_specs.append(_const_spec(g_final.shape))
        args.append(g_final)
    out_shape = [jax.ShapeDtypeStruct(x.shape, jnp.float32), jax.ShapeDtypeStruct(st_shape, jnp.float32)]
    out_specs = [x_out_spec, st_spec]
    scratch = [pltpu.VMEM((N_FF_CHUNKS, CONV_WIDTH - 1, SUBLANES, 2 * FF_CHUNK), jnp.float32),
               pltpu.VMEM((tm, D_MODEL), jnp.bfloat16),
               pltpu.VMEM((tm, D_FF), jnp.bfloat16)]
    if chained:
        scratch += [pltpu.VMEM((2, seg, SUBLANES, D_MODEL), jnp.float32),
                    pltpu.VMEM((2, seg, SUBLANES, D_MODEL), jnp.float32),
                    pltpu.SemaphoreType.DMA((2, SUBLANES)),
                    pltpu.SemaphoreType.DMA((2, SUBLANES))]
    y, fst = pl.pallas_call(
        functools.partial(_ffn_kernel, tm=tm, nt=nt, nsteps=grid[0] * grid[1], chained=chained, final=final),
        grid=grid,
        in_specs=in_specs,
        out_specs=out_specs,
        out_shape=out_shape,
        scratch_shapes=scratch,
        compiler_params=pltpu.CompilerParams(
            dimension_semantics=("arbitrary", "arbitrary"), vmem_limit_bytes=VMEM_LIMIT),
        name=("ffn" if chained else "ffn_past") + ("_final" if final else ""),
    )(*args)
    if not chained:
        y, fst = y.reshape(bsz, t_len, D_MODEL), jnp.swapaxes(fst, 0, 1)
    return y, fst


def _rope_tables(pos0, t_len):
    half = ROT_DIM // 2
    inv_freq = np.power(np.float64(ROPE_THETA), -np.arange(half, dtype=np.float64) * (2.0 / ROT_DIM))
    ang = (pos0 + np.arange(t_len, dtype=np.float64))[:, None] * inv_freq[None, :]
    cos, sin = np.cos(ang), np.sin(ang)
    rest = HEAD_DIM - ROT_DIM
    cos_h = np.concatenate([cos, cos, np.ones((t_len, rest))], axis=1)
    sinlo_h = np.concatenate([-sin, np.zeros((t_len, half + rest))], axis=1)
    sinhi_h = np.concatenate([np.zeros((t_len, half)), sin, np.zeros((t_len, rest))], axis=1)
    return tuple(jnp.asarray(np.concatenate([a, a], axis=1), dtype=jnp.float32) for a in (cos_h, sinlo_h, sinhi_h))


def _stacked_weights(g_mix, w_in, attn_sink, conv_w, gmlp_ln_g, gmlp_ln_b, gmlp_w, gmlp_b, pool_w, pool_scale,
                     w_out, g_ffn, w_up, ffn_conv_w, w_down):
    eye = jnp.eye(len(POOL_WINDOWS), dtype=pool_w.dtype)
    pw = jnp.einsum('lgij,gh->lgihj', pool_w, eye).reshape(DEPTH, W_POOL, W_POOL)
    return {
        'g_mix': g_mix[:, None, :],
        'w_in': w_in.astype(jnp.bfloat16),
        'sink': attn_sink,
        'conv_w': conv_w,
        'ln_g': gmlp_ln_g[:, None, :],
        'ln_b': gmlp_ln_b[:, None, :],
        'gmlp_w': gmlp_w,
        'gmlp_b': jnp.repeat(jnp.swapaxes(gmlp_b, 1, 2), GMLP_GROUP_DIM, axis=2),
        'pool_w': pw.astype(jnp.bfloat16),
        'pool_scale': pool_scale[:, None, :],
        'w_out': w_out.astype(jnp.bfloat16),
        'g_ffn': g_ffn[:, None, :],
        'w_up': w_up.astype(jnp.bfloat16),
        'ffn_conv_w': ffn_conv_w,
        'w_down': w_down.astype(jnp.bfloat16),
    }


def _run(x, pos0, pasts, lw, g_final, tm):
    bsz, t_len, _ = x.shape
    tabs = _rope_tables(pos0, t_len)
    if pasts is not None:
        ck, cv, sc, sp, sf = pasts
        mpast = (ck.reshape(bsz, DEPTH, WINDOW, KV_WIDTH), cv.reshape(bsz, DEPTH, WINDOW, KV_WIDTH), sc, sp)
        fpast = jnp.transpose(sf, (1, 2, 0, 3))
    else:
        mpast, fpast = None, None
    states = []
    for l in range(DEPTH):
        if pasts is None:
            mo = _mixer_call(x, tabs, lw, l, tm=tm, pos0=pos0)
        else:
            mo = _mixer_sample_call(x, tabs, mpast, lw, l, pos0=pos0)
        x_mid, kst, vst, cst, pst = mo[:5]
        gfin = g_final[None, :] if l == DEPTH - 1 else None
        x, fst = _ffn_call(x_mid, fpast, lw, l, gfin, tm=tm)
        st = [kst.reshape(bsz, -1, N_KV_HEADS, HEAD_DIM), vst.reshape(bsz, -1, N_KV_HEADS, HEAD_DIM),
              cst, pst, fst]
        if pasts is not None:
            st.append(mo[5])
        states.append(st)
    stacked = [jnp.stack([st[i] for st in states], axis=1) for i in range(len(states[0]))]
    return x, stacked


def kernel(x_prompt, x_sample, cache_attn_k, cache_attn_v, state_conv, state_pool, state_ffn_conv, g_mix, w_in,
           attn_sink, conv_w, gmlp_ln_g, gmlp_ln_b, gmlp_w, gmlp_b, pool_w, pool_scale, w_out, g_ffn, w_up,
           ffn_conv_w, w_down, g_final):
    lw = _stacked_weights(g_mix, w_in, attn_sink, conv_w, gmlp_ln_g, gmlp_ln_b, gmlp_w, gmlp_b, pool_w,
                          pool_scale, w_out, g_ffn, w_up, ffn_conv_w, w_down)
    y_prompt, sp = _run(x_prompt, 0, None, lw, g_final, tm=1024)
    y_sample, ss = _run(x_sample, PAST_LEN, (cache_attn_k, cache_attn_v, state_conv, state_pool, state_ffn_conv),
                        lw, g_final, tm=x_sample.shape[1])
    return (y_prompt, y_sample, sp[0], sp[1], sp[2], sp[3], sp[4], ss[0], ss[1], ss[2], ss[3], ss[4], ss[5])
```
